```python
import jax, jax.numpy as jnp
from jax import lax
import numpy as np

D_MODEL = 1024
BATCH = 8
SEQ = 2048
DEPTH = 2
DEC_BATCH = 128
DEC_SEQ = 1
PAST_LEN = 16384
PAGE_SIZE = 128

D_A = D_MODEL // 2
D_B = D_MODEL // 2
CONV_A = 3
CONV_B = 31
D_IN = 3 * D_A + 2 * D_B
POOL_WINDOWS = (2, 4, 8, 16)
N_POOL = len(POOL_WINDOWS)
POOL_GROUP = D_MODEL // N_POOL
POOL_CTX = max(POOL_WINDOWS) - 1
D_FF = 2816
N_EXPERTS = 8
TOP_K = 2
D_FF_EXPERT = 3584
MOE_BLOCK = 128
N_EVEN = (DEPTH + 1) // 2
N_ODD = DEPTH // 2
EPS = 1e-6

kernel_name = "hybrid_conv_pool_moe_adaln_step"


def rmsnorm(x, g):
    xf = x.astype(jnp.float32)
    y = xf * lax.rsqrt(jnp.mean(xf * xf, axis=-1, keepdims=True) + EPS)
    return (y * g.astype(jnp.float32)).astype(x.dtype)


def layernorm(x, g, b):
    xf = x.astype(jnp.float32)
    mu = jnp.mean(xf, axis=-1, keepdims=True)
    var = jnp.mean(jnp.square(xf - mu), axis=-1, keepdims=True)
    y = (xf - mu) * lax.rsqrt(var + EPS)
    return (y * g.astype(jnp.float32) + b.astype(jnp.float32)).astype(x.dtype)


def causal_depthwise_conv(u, buf, w):
    k = w.shape[0]
    upad = jnp.concatenate([buf.astype(u.dtype), u], axis=1)
    y = lax.conv_general_dilated(upad, w[:, None, :].astype(u.dtype), (1,), 'VALID',
                                 dimension_numbers=('NWC', 'WIO', 'NWC'),
                                 feature_group_count=u.shape[-1])
    return y, upad[:, upad.shape[1] - (k - 1):]


def conv_pair_mixer(h, buf_a, buf_b, w_in, w_conv_a, w_conv_b, b_conv_b, ln_g, ln_b, w_out):
    u = h @ w_in
    a_b, a_c, a_x, b_v, b_g = jnp.split(u, [D_A, 2 * D_A, 3 * D_A, 3 * D_A + D_B], axis=-1)
    ya, new_a = causal_depthwise_conv(a_c * a_x, buf_a, w_conv_a)
    ya = a_b * ya
    glu = b_v * jax.nn.sigmoid(b_g)
    yb, new_b = causal_depthwise_conv(glu, buf_b, w_conv_b)
    yb = jax.nn.silu(layernorm(yb + b_conv_b, ln_g, ln_b))
    y = jnp.concatenate([ya, yb], axis=-1) @ w_out
    return y, new_a, new_b


def pool_mixer(h, buf, start_pos, w_pool, scale):
    n, t, d = h.shape
    upad = jnp.concatenate([buf.astype(h.dtype), h], axis=1)
    cs = jnp.cumsum(upad.astype(jnp.float32), axis=1)
    cs = jnp.concatenate([jnp.zeros((n, 1, d), jnp.float32), cs], axis=1)
    pos = start_pos + jnp.arange(t)
    hf = h.astype(jnp.float32)
    off = POOL_CTX + 1
    groups = []
    for gi, win in enumerate(POOL_WINDOWS):
        sl = slice(gi * POOL_GROUP, (gi + 1) * POOL_GROUP)
        s = cs[:, off:off + t, sl] - cs[:, off - win:off - win + t, sl]
        cnt = jnp.minimum(pos + 1, win).astype(jnp.float32)[None, :, None]
        groups.append(s / cnt - hf[..., sl])
    diff = jnp.stack(groups, axis=2).astype(h.dtype)
    y = jnp.einsum('ntgc,gcd->ntgd', diff, w_pool).reshape(n, t, d)
    return y * scale, upad[:, upad.shape[1] - POOL_CTX:]


def swiglu(h, w_gate, w_up, w_down):
    return (jax.nn.silu(h @ w_gate) * (h @ w_up)) @ w_down


def moe_swiglu(h, w_router, b_router, w_gate, w_up, w_down):
    n, t, d = h.shape
    T = n * t
    xt = h.reshape(T, d)
    logits = xt.astype(jnp.float32) @ w_router.astype(jnp.float32) + b_router.astype(jnp.float32)
    probs = jax.nn.softmax(logits, axis=-1)
    top_p, top_e = lax.top_k(probs, TOP_K)
    top_p = top_p / jnp.sum(top_p, axis=-1, keepdims=True)
    flat_e = top_e.reshape(-1).astype(jnp.int32)
    flat_tok = jnp.repeat(jnp.arange(T, dtype=jnp.int32), TOP_K)
    flat_p = top_p.reshape(-1)
    order = jnp.argsort(flat_e, stable=True)
    se = flat_e[order]
    counts = jnp.zeros((N_EXPERTS,), jnp.int32).at[flat_e].add(1)
    padded = (counts + MOE_BLOCK - 1) // MOE_BLOCK * MOE_BLOCK
    pend = jnp.cumsum(padded)
    pstart = pend - padded
    ustart = jnp.cumsum(counts) - counts
    dest = pstart[se] + jnp.arange(T * TOP_K, dtype=jnp.int32) - ustart[se]
    n_blocks = -(-(T * TOP_K) // MOE_BLOCK) + N_EXPERTS
    n_slots = n_blocks * MOE_BLOCK
    slot_tok = jnp.full((n_slots,), T, jnp.int32).at[dest].set(flat_tok[order])
    slot_p = jnp.zeros((n_slots,), jnp.float32).at[dest].set(flat_p[order])
    block_e = jnp.minimum(jnp.searchsorted(pend, jnp.arange(n_blocks) * MOE_BLOCK, side='right'),
                          N_EXPERTS - 1).astype(jnp.int32)
    x_ext = jnp.concatenate([xt, jnp.zeros((1, d), xt.dtype)], axis=0)
    xs = x_ext[slot_tok].reshape(n_blocks, MOE_BLOCK, d)

    def expert_block(args):
        xb, e = args
        return swiglu(xb, w_gate[e], w_up[e], w_down[e])

    ys = lax.map(expert_block, (xs, block_e)).reshape(n_slots, d)
    out = jnp.zeros((T + 1, d), jnp.float32).at[slot_tok].add(ys.astype(jnp.float32) * slot_p[:, None])
    return out[:T].astype(h.dtype).reshape(n, t, d)


def _trunk(x, c, st_a, st_b, st_c, start_pos, p):
    new_a, new_b, new_c = [], [], []
    cs = jax.nn.silu(c)
    for i in range(DEPTH):
        j = i // 2
        mod = (cs @ p['w_ada'][i] + p['b_ada'][i])[:, None, :]
        sh_m, sc_m, g_m, sh_f, sc_f, g_f = jnp.split(mod, 6, axis=-1)
        h = rmsnorm(x, p['norm_mix'][i]) * (1 + sc_m) + sh_m
        if i % 2 == 0:
            y, na, nb = conv_pair_mixer(h, st_a[j], st_b[j], p['w_in'][j], p['w_conv_a'][j],
                                        p['w_conv_b'][j], p['b_conv_b'][j], p['ln_b_g'][j],
                                        p['ln_b_b'][j], p['w_out'][j])
            new_a.append(na)
            new_b.append(nb)
        else:
            y, nc = pool_mixer(h, st_c[j], start_pos, p['w_pool'][j], p['pool_scale'][j])
            new_c.append(nc)
        x = x + g_m * y
        h = rmsnorm(x, p['norm_ffn'][i]) * (1 + sc_f) + sh_f
        if i % 2 == 0:
            f = swiglu(h, p['w_ffn_gate'][j], p['w_ffn_up'][j], p['w_ffn_down'][j])
        else:
            f = moe_swiglu(h, p['w_router'][j], p['b_router'][j], p['w_exp_gate'][j],
                           p['w_exp_up'][j], p['w_exp_down'][j])
        x = x + g_f * f
    y = rmsnorm(x, p['norm_out'])
    return y, jnp.stack(new_a), jnp.stack(new_b), jnp.stack(new_c)


def setup_inputs(seed: int = 0) -> dict:
    key = jax.random.key(seed)
    ks = jax.random.split(key, 32)
    f32 = jnp.float32
    nrm = lambda k, s, sc: jax.random.normal(k, s, f32) * sc
    D = D_MODEL
    return {
        "x_prompt": nrm(ks[0], (BATCH, SEQ, D), 1.0),
        "x_sample": nrm(ks[1], (DEC_BATCH, DEC_SEQ, D), 1.0),
        "state_a": nrm(ks[2], (N_EVEN, DEC_BATCH, CONV_A - 1, D_A), 1.0),
        "state_b": nrm(ks[3], (N_EVEN, DEC_BATCH, CONV_B - 1, D_B), 0.5),
        "state_c": nrm(ks[4], (N_ODD, DEC_BATCH, POOL_CTX, D), 1.0),
        "c_prompt": nrm(ks[5], (BATCH, D), 1.0),
        "c_sample": nrm(ks[6], (DEC_BATCH, D), 1.0),
        "w_ada": nrm(ks[7], (DEPTH, D, 6 * D), 0.5 * D ** -0.5),
        "b_ada": nrm(ks[8], (DEPTH, 6 * D), 0.02),
        "norm_mix": 1.0 + nrm(ks[9], (DEPTH, D), 0.02),
        "norm_ffn": 1.0 + nrm(ks[10], (DEPTH, D), 0.02),
        "norm_out": 1.0 + nrm(ks[11], (D,), 0.02),
        "w_in": nrm(ks[12], (N_EVEN, D, D_IN), D ** -0.5),
        "w_conv_a": nrm(ks[13], (N_EVEN, CONV_A, D_A), CONV_A ** -0.5),
        "w_conv_b": nrm(ks[14], (N_EVEN, CONV_B, D_B), CONV_B ** -0.5),
        "b_conv_b": nrm(ks[15], (N_EVEN, D_B), 0.02),
        "ln_b_g": 1.0 + nrm(ks[16], (N_EVEN, D_B), 0.02),
        "ln_b_b": nrm(ks[17], (N_EVEN, D_B), 0.02),
        "w_out": nrm(ks[18], (N_EVEN, D_A + D_B, D), (D_A + D_B) ** -0.5),
        "w_ffn_gate": nrm(ks[19], (N_EVEN, D, D_FF), D ** -0.5),
        "w_ffn_up": nrm(ks[20], (N_EVEN, D, D_FF), D ** -0.5),
        "w_ffn_down": nrm(ks[21], (N_EVEN, D_FF, D), D_FF ** -0.5),
        "w_pool": nrm(ks[22], (N_ODD, N_POOL, POOL_GROUP, POOL_GROUP), POOL_GROUP ** -0.5),
        "pool_scale": 1.0 + nrm(ks[23], (N_ODD, D), 0.1),
        "w_router": nrm(ks[24], (N_ODD, D, N_EXPERTS), D ** -0.5),
        "b_router": nrm(ks[25], (N_ODD, N_EXPERTS), 0.01),
        "w_exp_gate": nrm(ks[26], (N_ODD, N_EXPERTS, D, D_FF_EXPERT), D ** -0.5),
        "w_exp_up": nrm(ks[27], (N_ODD, N_EXPERTS, D, D_FF_EXPERT), D ** -0.5),
        "w_exp_down": nrm(ks[28], (N_ODD, N_EXPERTS, D_FF_EXPERT, D), D_FF_EXPERT ** -0.5),
    }


def reference(x_prompt, x_sample, state_a, state_b, state_c, c_prompt, c_sample,
              w_ada, b_ada, norm_mix, norm_ffn, norm_out, w_in, w_conv_a, w_conv_b, b_conv_b,
              ln_b_g, ln_b_b, w_out, w_ffn_gate, w_ffn_up, w_ffn_down, w_pool, pool_scale,
              w_router, b_router, w_exp_gate, w_exp_up, w_exp_down):
    p = {
        'w_ada': w_ada, 'b_ada': b_ada, 'norm_mix': norm_mix, 'norm_ffn': norm_ffn,
        'norm_out': norm_out, 'w_in': w_in, 'w_conv_a': w_conv_a, 'w_conv_b': w_conv_b,
        'b_conv_b': b_conv_b, 'ln_b_g': ln_b_g, 'ln_b_b': ln_b_b, 'w_out': w_out,
        'w_ffn_gate': w_ffn_gate, 'w_ffn_up': w_ffn_up, 'w_ffn_down': w_ffn_down,
        'w_pool': w_pool, 'pool_scale': pool_scale, 'w_router': w_router, 'b_router': b_router,
        'w_exp_gate': w_exp_gate, 'w_exp_up': w_exp_up, 'w_exp_down': w_exp_down,
    }
    nb = x_prompt.shape[0]
    dt = x_prompt.dtype
    z_a = jnp.zeros((N_EVEN, nb, CONV_A - 1, D_A), dt)
    z_b = jnp.zeros((N_EVEN, nb, CONV_B - 1, D_B), dt)
    z_c = jnp.zeros((N_ODD, nb, POOL_CTX, D_MODEL), dt)
    y_prompt, pa, pb, pc = _trunk(x_prompt, c_prompt, z_a, z_b, z_c, 0, p)
    y_sample, sa, sb, sc = _trunk(x_sample, c_sample, state_a, state_b, state_c, PAST_LEN, p)
    return (y_prompt, y_sample, pa, sa, pb, sb, pc, sc)
```

```python
import functools

import jax
import jax.numpy as jnp
from jax import lax
from jax.experimental import pallas as pl
from jax.experimental.pallas import tpu as pltpu

EPS = 1e-6
PAST_LEN = 16384
POOL_WINDOWS = (2, 4, 8, 16)
N_MOD = 6
LANES = 128
SUBLANES = 8
VMEM_LIMIT = 56 * 1024 * 1024
NEG_BIG = -1e30

ROW_TILE = 512
MOE_TILE = 1024
MOE_SUB = 256
MOE_FF = 512
GATHER_TILE = 256

F32 = jnp.float32
BF16 = jnp.bfloat16


def _params(sem):
    return pltpu.CompilerParams(dimension_semantics=sem, vmem_limit_bytes=VMEM_LIMIT)


def _silu(x):
    return x * jax.nn.sigmoid(x)


def _rmsnorm(x, g):
    return x * lax.rsqrt(jnp.mean(x * x, axis=-1, keepdims=True) + EPS) * g


def _mod_rmsnorm(x, g, sc, sh):
    return _rmsnorm(x, g) * (1.0 + sc) + sh


def _resident(shape):
    nd = len(shape)
    return pl.BlockSpec(shape, lambda *_: (0,) * nd, pipeline_mode=pl.Buffered(1))


def _small(shape):
    nd = len(shape)
    return pl.BlockSpec(shape, lambda *_: (0,) * nd)


class _Group:
    def __init__(self, n_seq, seq_len, per_row_mod, mod_row0, n_sample):
        self.n_seq, self.seq_len = n_seq, seq_len
        self.per_row_mod = per_row_mod
        self.mod_row0 = mod_row0
        self.n_sample = n_sample
        if per_row_mod:
            self.tile = n_seq
            self.grid = (1, 1)
        else:
            self.tile = min(ROW_TILE, seq_len)
            assert seq_len % self.tile == 0 and self.tile % 32 == 0
            self.grid = (n_seq, seq_len // self.tile)
        self.rows = n_seq * seq_len

    def row_spec(self, width):
        nt = self.grid[1]
        return pl.BlockSpec((self.tile, width), lambda b, j, *_: (b * nt + j, 0))

    def mod_spec(self, layer, chunk, d):
        if self.per_row_mod:
            return pl.BlockSpec((None, self.n_sample, d), lambda b, j, *_: (layer, 0, chunk))
        r0 = self.mod_row0
        return pl.BlockSpec((None, None, 1, d), lambda b, j, *_: (layer, r0 + b, 0, chunk))

    def mod_arg(self, mod3):
        if self.per_row_mod:
            return mod3
        l, r, w = mod3.shape
        return mod3.reshape(l, r, 1, w)


def _ada_kernel(c_ref, w_ref, b_ref, o_ref):
    cs = _silu(c_ref[...]).astype(BF16)
    o_ref[...] = jnp.dot(cs, w_ref[...].astype(BF16), preferred_element_type=F32) + b_ref[...]


def _ada(c_all, w_ada, b_ada):
    depth, d, w6 = w_ada.shape
    r = c_all.shape[0]
    tn = w6 // 4
    return pl.pallas_call(
        _ada_kernel,
        out_shape=jax.ShapeDtypeStruct((depth, r, w6), F32),
        grid=(depth, w6 // tn),
        in_specs=[pl.BlockSpec((r, d), lambda l, n: (0, 0)),
                  pl.BlockSpec((None, d, tn), lambda l, n: (l, 0, n)),
                  pl.BlockSpec((None, 1, tn), lambda l, n: (l, 0, n))],
        out_specs=pl.BlockSpec((None, r, tn), lambda l, n: (l, 0, n)),
        compiler_params=_params(("arbitrary", "arbitrary")),
        name="ada",
    )(c_all, w_ada, b_ada.reshape(depth, 1, w6))


def _in_proj_kernel(x_ref, g_ref, sc_ref, sh_ref, w_ref, u_ref):
    h = _mod_rmsnorm(x_ref[...], g_ref[...], sc_ref[...], sh_ref[...])
    u_ref[...] = jnp.dot(h.astype(BF16), w_ref[...], preferred_element_type=F32)


def _in_proj(grp, x, mod3, layer, g, w_bf):
    d, n = w_bf.shape
    return pl.pallas_call(
        _in_proj_kernel,
        out_shape=jax.ShapeDtypeStruct((grp.rows, n), F32),
        grid=grp.grid,
        in_specs=[grp.row_spec(d), _small((1, d)),
                  grp.mod_spec(layer, 1, d), grp.mod_spec(layer, 0, d),
                  _resident((d, n))],
        out_specs=grp.row_spec(n),
        compiler_params=_params(("arbitrary", "arbitrary")),
        name="in_proj",
    )(x, g, grp.mod_arg(mod3), grp.mod_arg(mod3), w_bf)


def _layernorm(y, g, b):
    mu = jnp.mean(y, axis=-1, keepdims=True)
    yc = y - mu
    var = jnp.mean(yc * yc, axis=-1, keepdims=True)
    return yc * lax.rsqrt(var + EPS) * g + b


def _conv_seq_kernel(n_t, ka, kb, u_ref, uh_ref, x_ref, gm_ref, wa_ref, wb_ref, bb_ref,
                     lng_ref, lnb_ref, wout_ref, x1_ref, na_ref, nb_ref, apad, gpad):
    j = pl.program_id(1)
    tt = u_ref.shape[0]
    da = wa_ref.shape[1]
    hb = uh_ref.shape[0]
    keep = (j > 0).astype(F32)

    u = u_ref[...]
    a_b, a_c, a_x = u[:, 0:da], u[:, da:2 * da], u[:, 2 * da:3 * da]
    b_v, b_g = u[:, 3 * da:4 * da], u[:, 4 * da:5 * da]
    uh = uh_ref[...]
    apad[0:SUBLANES, :] = keep * (uh[hb - SUBLANES:hb, da:2 * da] * uh[hb - SUBLANES:hb, 2 * da:3 * da])
    apad[SUBLANES:SUBLANES + tt, :] = a_c * a_x
    gpad[0:hb, :] = keep * (uh[:, 3 * da:4 * da] * jax.nn.sigmoid(uh[:, 4 * da:5 * da]))
    gpad[hb:hb + tt, :] = b_v * jax.nn.sigmoid(b_g)

    ya = None
    for k in range(ka):
        term = wa_ref[k:k + 1, :] * apad[SUBLANES - (ka - 1) + k:SUBLANES - (ka - 1) + k + tt, :]
        ya = term if ya is None else ya + term
    ya = a_b * ya
    yb = None
    for k in range(kb):
        term = wb_ref[k:k + 1, :] * gpad[hb - (kb - 1) + k:hb - (kb - 1) + k + tt, :]
        yb = term if yb is None else yb + term
    yb = _silu(_layernorm(yb + bb_ref[...], lng_ref[...], lnb_ref[...]))

    ycat = jnp.concatenate([ya, yb], axis=-1).astype(BF16)
    y = jnp.dot(ycat, wout_ref[...], preferred_element_type=F32)
    x1_ref[...] = x_ref[...] + gm_ref[...] * y

    @pl.when(j == n_t - 1)
    def _():
        na_ref[...] = apad[SUBLANES + tt - (ka - 1):SUBLANES + tt, :]
        nb_ref[...] = gpad[hb + tt - (kb - 1):hb + tt, :]


def _conv_seq(grp, u, x, mod3, layer, wa, wb, bb, lng, lnb, wout_bf):
    tt = grp.tile
    n_b, n_t = grp.grid
    d = x.shape[1]
    ka, da = wa.shape
    kb = wb.shape[0]
    hb = 32
    assert kb - 1 <= hb and ka - 1 <= SUBLANES and tt % hb == 0
    per = tt // hb
    halo = pl.BlockSpec((hb, u.shape[1]), lambda b, j: (jnp.maximum((b * n_t + j) * per - 1, 0), 0))
    return pl.pallas_call(
        functools.partial(_conv_seq_kernel, n_t, ka, kb),
        out_shape=(jax.ShapeDtypeStruct((grp.rows, d), F32),
                   jax.ShapeDtypeStruct((n_b, ka - 1, da), F32),
                   jax.ShapeDtypeStruct((n_b, kb - 1, da), F32)),
        grid=grp.grid,
        in_specs=[grp.row_spec(u.shape[1]), halo, grp.row_spec(d), grp.mod_spec(layer, 2, d),
                  _small(wa.shape), _small(wb.shape), _small((1, da)), _small((1, da)), _small((1, da)),
                  _resident(wout_bf.shape)],
        out_specs=(grp.row_spec(d),
                   pl.BlockSpec((None, ka - 1, da), lambda b, j: (b, 0, 0)),
                   pl.BlockSpec((None, kb - 1, da), lambda b, j: (b, 0, 0))),
        scratch_shapes=[pltpu.VMEM((SUBLANES + tt, da), F32), pltpu.VMEM((hb + tt, da), F32)],
        compiler_params=_params(("arbitrary", "arbitrary")),
        name="conv_seq",
    )(u, u, x, grp.mod_arg(mod3), wa, wb, bb, lng, lnb, wout_bf)


def _conv_step_kernel(ka, kb, u_ref, sa_ref, sb_ref, x_ref, gm_ref, wa_ref, wb_ref, bb_ref,
                      lng_ref, lnb_ref, wout_ref, x1_ref, na_ref, nb_ref):
    da = wa_ref.shape[1]
    u = u_ref[...]
    a_b, a_c, a_x = u[:, 0:da], u[:, da:2 * da], u[:, 2 * da:3 * da]
    b_v, b_g = u[:, 3 * da:4 * da], u[:, 4 * da:5 * da]

    cur = a_c * a_x
    ya = wa_ref[ka - 1:ka, :] * cur
    for k in range(ka - 1):
        ya = ya + wa_ref[k:k + 1, :] * sa_ref[:, k * da:(k + 1) * da]
    ya = a_b * ya
    glu = b_v * jax.nn.sigmoid(b_g)
    yb = wb_ref[kb - 1:kb, :] * glu
    for k in range(kb - 1):
        yb = yb + wb_ref[k:k + 1, :] * sb_ref[:, k * da:(k + 1) * da]
    yb = _silu(_layernorm(yb + bb_ref[...], lng_ref[...], lnb_ref[...]))

    ycat = jnp.concatenate([ya, yb], axis=-1).astype(BF16)
    y = jnp.dot(ycat, wout_ref[...], preferred_element_type=F32)
    x1_ref[...] = x_ref[...] + gm_ref[...] * y

    if ka > 2:
        na_ref[:, 0:(ka - 2) * da] = sa_ref[:, da:(ka - 1) * da]
    na_ref[:, (ka - 2) * da:(ka - 1) * da] = cur
    if kb > 2:
        nb_ref[:, 0:(kb - 2) * da] = sb_ref[:, da:(kb - 1) * da]
    nb_ref[:, (kb - 2) * da:(kb - 1) * da] = glu


def _conv_step(grp, u, sa2, sb2, x, mod3, layer, wa, wb, bb, lng, lnb, wout_bf):
    n = grp.rows
    d = x.shape[1]
    ka, da = wa.shape
    kb = wb.shape[0]
    return pl.pallas_call(
        functools.partial(_conv_step_kernel, ka, kb),
        out_shape=(jax.ShapeDtypeStruct((n, d), F32),
                   jax.ShapeDtypeStruct(sa2.shape, F32),
                   jax.ShapeDtypeStruct(sb2.shape, F32)),
        grid=grp.grid,
        in_specs=[_small(u.shape), _small(sa2.shape), _small(sb2.shape), _small(x.shape),
                  grp.mod_spec(layer, 2, d),
                  _small(wa.shape), _small(wb.shape), _small((1, da)), _small((1, da)), _small((1, da)),
                  _small(wout_bf.shape)],
        out_specs=(_small((n, d)), _small(sa2.shape), _small(sb2.shape)),
        compiler_params=_params(("arbitrary", "arbitrary")),
        name="conv_step",
    )(u, sa2, sb2, x, grp.mod_arg(mod3), wa, wb, bb, lng, lnb, wout_bf)


def _ffn_kernel(x_ref, g_ref, sc_ref, sh_ref, gate_ref, wg_ref, wu_ref, wd_ref, o_ref):
    x = x_ref[...]
    h = _mod_rmsnorm(x, g_ref[...], sc_ref[...], sh_ref[...]).astype(BF16)
    a = jnp.dot(h, wg_ref[...], preferred_element_type=F32)
    b = jnp.dot(h, wu_ref[...], preferred_element_type=F32)
    act = (_silu(a) * b).astype(BF16)
    f = jnp.dot(act, wd_ref[...], preferred_element_type=F32)
    o_ref[...] = x + gate_ref[...] * f


def _ffn(grp, x, mod3, layer, g, wg_bf, wu_bf, wd_bf):
    d = x.shape[1]
    return pl.pallas_call(
        _ffn_kernel,
        out_shape=jax.ShapeDtypeStruct(x.shape, F32),
        grid=grp.grid,
        in_specs=[grp.row_spec(d), _small((1, d)),
                  grp.mod_spec(layer, 4, d), grp.mod_spec(layer, 3, d), grp.mod_spec(layer, 5, d),
                  _resident(wg_bf.shape), _resident(wu_bf.shape), _resident(wd_bf.shape)],
        out_specs=grp.row_spec(d),
        compiler_params=_params(("arbitrary", "arbitrary")),
        name="ffn",
    )(x, g, grp.mod_arg(mod3), grp.mod_arg(mod3), grp.mod_arg(mod3), wg_bf, wu_bf, wd_bf)


def _pool_project(diff, wp_ref):
    n_g, pg, _ = wp_ref.shape
    outs = [jnp.dot(diff[:, gi * pg:(gi + 1) * pg].astype(BF16), wp_ref[gi], preferred_element_type=F32)
            for gi in range(n_g)]
    return jnp.concatenate(outs, axis=-1)


def _route(h3, wrh_ref, wrl_ref, br_ref, tri_ref, base_ref, mi_ref, mf_ref, cnt_ref):
    tt = h3.shape[0]
    h_hi = h3.astype(BF16)
    h_lo = (h3 - h_hi.astype(F32)).astype(BF16)
    logits = (jnp.dot(h_hi, wrh_ref[...], preferred_element_type=F32)
              + jnp.dot(h_lo, wrh_ref[...], preferred_element_type=F32)
              + jnp.dot(h_hi, wrl_ref[...], preferred_element_type=F32)
              + br_ref[...])
    lane = lax.broadcasted_iota(jnp.int32, (tt, LANES), 1).astype(F32)
    m0 = jnp.max(logits, axis=-1, keepdims=True)
    e0 = jnp.min(jnp.where(logits == m0, lane, float(LANES)), axis=-1, keepdims=True)
    rest = jnp.where(lane == e0, NEG_BIG * 2.0, logits)
    m1 = jnp.max(rest, axis=-1, keepdims=True)
    e1 = jnp.min(jnp.where(rest == m1, lane, float(LANES)), axis=-1, keepdims=True)
    dd = jnp.exp(m1 - m0)
    p0 = 1.0 / (1.0 + dd)
    p1 = dd * p0

    oh0 = lane == e0
    oh1 = lane == e1
    c = jnp.where(oh0 | oh1, 1.0, 0.0)
    prefix = jnp.dot(tri_ref[...], c.astype(BF16), preferred_element_type=F32)
    tot = base_ref[...] + prefix
    r0 = jnp.sum(jnp.where(oh0, tot, 0.0), axis=-1, keepdims=True)
    r1 = jnp.sum(jnp.where(oh1, tot, 0.0), axis=-1, keepdims=True)
    new_base = base_ref[...] + jnp.sum(c, axis=0, keepdims=True)
    base_ref[...] = new_base
    cnt_ref[...] = jnp.broadcast_to(new_base, cnt_ref.shape)

    meta = jnp.where(lane == 0.0, e0, jnp.where(lane == 1.0, e1,
                     jnp.where(lane == 2.0, r0, jnp.where(lane == 3.0, r1, 0.0))))
    mi_ref[...] = meta.astype(jnp.int32)
    mf_ref[...] = jnp.where(lane == 0.0, p0, jnp.where(lane == 1.0, p1, 0.0))


def _init_route_scratch(first, tri_ref, base_ref):
    @pl.when(first)
    def _():
        tt = tri_ref.shape[0]
        r = lax.broadcasted_iota(jnp.int32, (tt, tt), 0)
        c = lax.broadcasted_iota(jnp.int32, (tt, tt), 1)
        tri_ref[...] = jnp.where(c < r, 1.0, 0.0).astype(BF16)
        base_ref[...] = jnp.zeros(base_ref.shape, F32)


def _pool_seq_kernel(n_t, ctx, x_ref, xh_ref, gmix_ref, shm_ref, scm_ref, gm_ref, gffn_ref, shf_ref, scf_ref,
                     wp_ref, ps_ref, wrh_ref, wrl_ref, br_ref,
                     x3_ref, h3_ref, mi_ref, mf_ref, nc_ref, cnt_ref, hpad, tri_ref, base_ref):
    b = pl.program_id(0)
    j = pl.program_id(1)
    tt = x_ref.shape[0]
    hb = xh_ref.shape[0]
    n_g, pg, _ = wp_ref.shape
    _init_route_scratch((b == 0) & (j == 0), tri_ref, base_ref)

    keep = (j > 0).astype(F32)
    x = x_ref[...]
    h = _mod_rmsnorm(x, gmix_ref[...], scm_ref[...], shm_ref[...])
    hpad[0:hb, :] = keep * _mod_rmsnorm(xh_ref[...], gmix_ref[...], scm_ref[...], shm_ref[...])
    hpad[hb:hb + tt, :] = h

    pos = lax.broadcasted_iota(jnp.int32, (tt, pg), 0) + j * tt
    groups = []
    for gi, win in enumerate(POOL_WINDOWS):
        sl = slice(gi * pg, (gi + 1) * pg)
        s = h[:, sl]
        for i in range(1, win):
            s = s + hpad[hb - i:hb - i + tt, sl]
        cnt = jnp.minimum(pos + 1, win).astype(F32)
        groups.append(s / cnt - h[:, sl])
    diff = jnp.concatenate(groups, axis=-1)
    y = _pool_project(diff, wp_ref) * ps_ref[...]
    x3 = x + gm_ref[...] * y
    x3_ref[...] = x3
    h3 = _mod_rmsnorm(x3, gffn_ref[...], scf_ref[...], shf_ref[...])
    h3_ref[...] = h3
    _route(h3, wrh_ref, wrl_ref, br_ref, tri_ref, base_ref, mi_ref, mf_ref, cnt_ref)

    @pl.when(j == n_t - 1)
    def _():
        nc_ref[...] = hpad[hb + tt - ctx:hb + tt, :]


def _route_out_shapes(rows, d):
    return (jax.ShapeDtypeStruct((rows, d), F32), jax.ShapeDtypeStruct((rows, d), F32),
            jax.ShapeDtypeStruct((rows, LANES), jnp.int32), jax.ShapeDtypeStruct((rows, LANES), F32))


def _pool_seq(grp, x, mod3, layer, gmix, gffn, wp_bf, ps, wrh, wrl, br, ctx):
    tt = grp.tile
    n_b, n_t = grp.grid
    d = x.shape[1]
    hb = 16
    assert ctx <= hb and max(POOL_WINDOWS) - 1 <= hb and tt % hb == 0
    per = tt // hb
    halo = pl.BlockSpec((hb, d), lambda b, j: (jnp.maximum((b * n_t + j) * per - 1, 0), 0))
    ms = lambda c: grp.mod_spec(layer, c, d)
    return pl.pallas_call(
        functools.partial(_pool_seq_kernel, n_t, ctx),
        out_shape=_route_out_shapes(grp.rows, d) + (
            jax.ShapeDtypeStruct((n_b, ctx, d), F32), jax.ShapeDtypeStruct((SUBLANES, LANES), F32)),
        grid=grp.grid,
        in_specs=[grp.row_spec(d), halo, _small((1, d)), ms(0), ms(1), ms(2), _small((1, d)), ms(3), ms(4),
                  _small(wp_bf.shape), _small((1, d)), _small(wrh.shape), _small(wrl.shape), _small(br.shape)],
        out_specs=(grp.row_spec(d), grp.row_spec(d), grp.row_spec(LANES), grp.row_spec(LANES),
                   pl.BlockSpec((None, ctx, d), lambda b, j: (b, 0, 0)), _small((SUBLANES, LANES))),
        scratch_shapes=[pltpu.VMEM((hb + tt, d), F32), pltpu.VMEM((tt, tt), BF16), pltpu.VMEM((1, LANES), F32)],
        compiler_params=_params(("arbitrary", "arbitrary")),
        name="pool_seq",
    )(x, x, gmix, *([grp.mod_arg(mod3)] * 3), gffn, *([grp.mod_arg(mod3)] * 2), wp_bf, ps, wrh, wrl, br)


def _pool_step_kernel(ctx, x_ref, sc_ref_state, gmix_ref, shm_ref, scm_ref, gm_ref, gffn_ref, shf_ref, scf_ref,
                      wp_ref, ps_ref, wrh_ref, wrl_ref, br_ref,
                      x3_ref, h3_ref, mi_ref, mf_ref, nc_ref, cnt_ref, tri_ref, base_ref):
    d = x_ref.shape[1]
    n_g, pg, _ = wp_ref.shape
    _init_route_scratch(pl.program_id(0) == 0, tri_ref, base_ref)

    x = x_ref[...]
    h = _mod_rmsnorm(x, gmix_ref[...], scm_ref[...], shm_ref[...])
    groups = []
    for gi, win in enumerate(POOL_WINDOWS):
        s = h[:, gi * pg:(gi + 1) * pg]
        for i in range(1, win):
            row = ctx - i
            s = s + sc_ref_state[:, row * d + gi * pg:row * d + (gi + 1) * pg]
        cnt = float(min(PAST_LEN + 1, win))
        groups.append(s / cnt - h[:, gi * pg:(gi + 1) * pg])
    diff = jnp.concatenate(groups, axis=-1)
    y = _pool_project(diff, wp_ref) * ps_ref[...]
    x3 = x + gm_ref[...] * y
    x3_ref[...] = x3
    h3 = _mod_rmsnorm(x3, gffn_ref[...], scf_ref[...], shf_ref[...])
    h3_ref[...] = h3
    _route(h3, wrh_ref, wrl_ref, br_ref, tri_ref, base_ref, mi_ref, mf_ref, cnt_ref)

    if ctx > 1:
        nc_ref[:, 0:(ctx - 1) * d] = sc_ref_state[:, d:ctx * d]
    nc_ref[:, (ctx - 1) * d:ctx * d] = h


def _pool_step(grp, x, sc2, mod3, layer, gmix, gffn, wp_bf, ps, wrh, wrl, br, ctx):
    n, d = x.shape
    ms = lambda c: grp.mod_spec(layer, c, d)
    return pl.pallas_call(
        functools.partial(_pool_step_kernel, ctx),
        out_shape=_route_out_shapes(n, d) + (
            jax.ShapeDtypeStruct(sc2.shape, F32), jax.ShapeDtypeStruct((SUBLANES, LANES), F32)),
        grid=grp.grid,
        in_specs=[_small(x.shape), _small(sc2.shape), _small((1, d)), ms(0), ms(1), ms(2), _small((1, d)), ms(3), ms(4),
                  _small(wp_bf.shape), _small((1, d)), _small(wrh.shape), _small(wrl.shape), _small(br.shape)],
        out_specs=(_small((n, d)), _small((n, d)), _small((n, LANES)), _small((n, LANES)),
                   _small(sc2.shape), _small((SUBLANES, LANES))),
        scratch_shapes=[pltpu.VMEM((n, n), BF16), pltpu.VMEM((1, LANES), F32)],
        compiler_params=_params(("arbitrary", "arbitrary")),
        name="pool_step",
    )(x, sc2, gmix, *([grp.mod_arg(mod3)] * 3), gffn, *([grp.mod_arg(mod3)] * 2), wp_bf, ps, wrh, wrl, br)


def _row_copy(src_hbm, src_row, dst, dst_row, sem):
    return pltpu.make_async_copy(src_hbm.at[pl.ds(src_row, 1), :], dst.at[pl.ds(dst_row, 1), :], sem)


def _dispatch_kernel(pos_ref, h_hbm, xs_hbm, sem):
    td = pos_ref.shape[1] // 2
    row0 = pl.program_id(0) * td

    def copies(t):
        return (_row_copy(h_hbm, row0 + t, xs_hbm, pos_ref[0, t], sem),
                _row_copy(h_hbm, row0 + t, xs_hbm, pos_ref[0, td + t], sem))

    def start(t, carry):
        for cp in copies(t):
            cp.start()
        return carry

    def wait(t, carry):
        for cp in copies(t):
            cp.wait()
        return carry

    lax.fori_loop(0, td, start, 0)
    lax.fori_loop(0, td, wait, 0)


def _dispatch(h3, pos3, n_slots):
    m, d = h3.shape
    n_tiles, _, td2 = pos3.shape
    return pl.pallas_call(
        _dispatch_kernel,
        out_shape=jax.ShapeDtypeStruct((n_slots, d), F32),
        grid=(n_tiles,),
        in_specs=[pl.BlockSpec((None, 1, td2), lambda i: (i, 0, 0), memory_space=pltpu.SMEM),
                  pl.BlockSpec(memory_space=pl.ANY)],
        out_specs=pl.BlockSpec(memory_space=pl.ANY),
        scratch_shapes=[pltpu.SemaphoreType.DMA(())],
        compiler_params=_params(("arbitrary",)),
        name="dispatch",
    )(pos3, h3)


def _moe_kernel(sub, we_ref, wt_ref, wlo_ref, whi_ref, wfirst_ref,
                xs_ref, wg_ref, wu_ref, wd_ref, ys_ref, xb, wgb, wub, wdb):
    w = pl.program_id(0)
    j = pl.program_id(1)
    tm = xs_ref.shape[0]
    lo = wlo_ref[w]
    hi = whi_ref[w]

    @pl.when((j == 0) & (wfirst_ref[w] == 1))
    def _():
        ys_ref[...] = jnp.zeros(ys_ref.shape, F32)

    @pl.when(hi > lo)
    def _():
        @pl.when(j == 0)
        def _():
            rows = lax.broadcasted_iota(jnp.int32, xs_ref.shape, 0)
            xb[...] = jnp.where((rows >= lo) & (rows < hi), xs_ref[...], 0.0).astype(BF16)

        wgb[...] = wg_ref[...].astype(BF16)
        wub[...] = wu_ref[...].astype(BF16)
        wdb[...] = wd_ref[...].astype(BF16)
        for s in range(tm // sub):
            @pl.when((s * sub < hi) & ((s + 1) * sub > lo))
            def _():
                rs = slice(s * sub, (s + 1) * sub)
                xsub = xb[rs, :]
                a = jnp.dot(xsub, wgb[...], preferred_element_type=F32)
                b = jnp.dot(xsub, wub[...], preferred_element_type=F32)
                act = (_silu(a) * b).astype(BF16)
                ys_ref[rs, :] += jnp.dot(act, wdb[...], preferred_element_type=F32)


def _moe(xs, work, wg, wu, wd, tm):
    n_slots, d = xs.shape
    n_e, _, ff = wg.shape
    fc = min(MOE_FF, ff)
    assert ff % fc == 0 and n_slots % tm == 0
    n_fc = ff // fc
    sub = min(MOE_SUB, tm)
    n_work = work[0].shape[0]

    def jj(w, j, wlo, whi):
        return jnp.where(whi[w] > wlo[w], j, n_fc - 1)

    grid_spec = pltpu.PrefetchScalarGridSpec(
        num_scalar_prefetch=5,
        grid=(n_work, n_fc),
        in_specs=[pl.BlockSpec((tm, d), lambda w, j, we, wt, wlo, whi, wf: (wt[w], 0)),
                  pl.BlockSpec((None, d, fc), lambda w, j, we, wt, wlo, whi, wf: (we[w], 0, jj(w, j, wlo, whi))),
                  pl.BlockSpec((None, d, fc), lambda w, j, we, wt, wlo, whi, wf: (we[w], 0, jj(w, j, wlo, whi))),
                  pl.BlockSpec((None, fc, d), lambda w, j, we, wt, wlo, whi, wf: (we[w], jj(w, j, wlo, whi), 0))],
        out_specs=pl.BlockSpec((tm, d), lambda w, j, we, wt, wlo, whi, wf: (wt[w], 0)),
        scratch_shapes=[pltpu.VMEM((tm, d), BF16), pltpu.VMEM((d, fc), BF16),
                        pltpu.VMEM((d, fc), BF16), pltpu.VMEM((fc, d), BF16)])
    return pl.pallas_call(
        functools.partial(_moe_kernel, sub),
        out_shape=jax.ShapeDtypeStruct((n_slots, d), F32),
        grid_spec=grid_spec,
        compiler_params=_params(("arbitrary", "arbitrary")),
        name="moe",
    )(*work, xs, wg, wu, wd)


def _moe_work_items(counts, tm, n_tiles):
    n_e = counts.shape[0]
    n_work = n_tiles + n_e - 1
    uend = jnp.cumsum(counts)
    ustart = uend - counts
    first_tile = ustart // tm
    n_w = jnp.where(counts > 0, (uend - 1) // tm - first_tile + 1, 0)
    wend = jnp.cumsum(n_w)
    wstart = wend - n_w
    total = wend[-1]
    w = jnp.minimum(jnp.arange(n_work, dtype=jnp.int32), total - 1)
    we = jnp.sum((w[:, None] >= wend[None, :]).astype(jnp.int32), axis=1)
    wt = first_tile[we] + (w - wstart[we])
    lo = jnp.clip(ustart[we] - wt * tm, 0, tm)
    hi = jnp.clip(uend[we] - wt * tm, 0, tm)
    live = jnp.arange(n_work, dtype=jnp.int32) < total
    hi = jnp.where(live, hi, lo)
    prev_t = jnp.concatenate([jnp.full((1,), -1, jnp.int32), wt[:-1]])
    wfirst = (live & (wt != prev_t)).astype(jnp.int32)
    as_i32 = lambda a: a.astype(jnp.int32)
    return ustart, (as_i32(we), as_i32(wt), as_i32(lo), as_i32(hi), wfirst)


def _combine_kernel(final, pos_ref, mf_ref, x3_ref, gate_ref, gout_ref, ys_hbm, o_ref, buf, sem):
    tc = x3_ref.shape[0]

    def copies(t):
        return (_row_copy(ys_hbm, pos_ref[0, t], buf.at[0], t, sem),
                _row_copy(ys_hbm, pos_ref[0, tc + t], buf.at[1], t, sem))

    def start(t, carry):
        for cp in copies(t):
            cp.start()
        return carry

    def wait(t, carry):
        for cp in copies(t):
            cp.wait()
        return carry

    lax.fori_loop(0, tc, start, 0)
    lax.fori_loop(0, tc, wait, 0)
    mf = mf_ref[...]
    moe = mf[:, 0:1] * buf[0] + mf[:, 1:2] * buf[1]
    x4 = x3_ref[...] + gate_ref[...] * moe
    o_ref[...] = _rmsnorm(x4, gout_ref[...]) if final else x4


def _combine(grp, ys, pos3, mf, x3, mod3, layer, gout, final):
    m, d = x3.shape
    tc = pos3.shape[2] // 2
    n_b, n_t = grp.grid
    per = grp.tile // tc
    assert grp.tile % tc == 0
    grid = (n_b, n_t * per)
    row = lambda w: pl.BlockSpec((tc, w), lambda b, j: (b * (n_t * per) + j, 0))
    if grp.per_row_mod:
        gate = grp.mod_spec(layer, 5, d)
    else:
        r0 = grp.mod_row0
        gate = pl.BlockSpec((None, None, 1, d), lambda b, j: (layer, r0 + b, 0, 5))
    return pl.pallas_call(
        functools.partial(_combine_kernel, final),
        out_shape=jax.ShapeDtypeStruct((m, d), F32),
        grid=grid,
        in_specs=[pl.BlockSpec((None, 1, 2 * tc), lambda b, j: (b * (n_t * per) + j, 0, 0), memory_space=pltpu.SMEM),
                  row(LANES), row(d), gate, _small((1, d)), pl.BlockSpec(memory_space=pl.ANY)],
        out_specs=row(d),
        scratch_shapes=[pltpu.VMEM((2, tc, d), F32), pltpu.SemaphoreType.DMA(())],
        compiler_params=_params(("arbitrary", "arbitrary")),
        name="combine",
    )(pos3, mf, x3, grp.mod_arg(mod3), gout, ys)


def _moe_layer(grp, x3, h3, mi, mf, cnt, mod3, layer, wg, wu, wd, gout, final):
    m, d = h3.shape
    n_e = wg.shape[0]
    tm = min(MOE_TILE, m)
    td = min(GATHER_TILE, m)
    n_slots = 2 * m
    assert n_slots % tm == 0 and m % td == 0
    counts = cnt[0, :n_e].astype(jnp.int32)
    ustart, work = _moe_work_items(counts, tm, n_slots // tm)

    e = mi[:, 0:2]
    pos = mi[:, 2:4]
    for k in range(n_e):
        pos = pos + jnp.where(e == k, ustart[k], 0)
    pos3 = pos.reshape(m // td, td, 2).transpose(0, 2, 1).reshape(m // td, 1, 2 * td)

    xs = _dispatch(h3, pos3, n_slots)
    ys = _moe(xs, work, wg, wu, wd, tm)
    return _combine(grp, ys, pos3, mf, x3, mod3, layer, gout, final)


def _trunk(grp, x, mod3, st_a, st_b, st_c, p):
    depth = p['w_ada'].shape[0]
    d = x.shape[1]
    new_a, new_b, new_c = [], [], []
    for i in range(depth):
        j = i // 2
        gmix = p['norm_mix'][i].reshape(1, d)
        gffn = p['norm_ffn'][i].reshape(1, d)
        if i % 2 == 0:
            u = _in_proj(grp, x, mod3, i, gmix, p['w_in'][j])
            conv_w = (p['w_conv_a'][j], p['w_conv_b'][j], p['b_conv_b'][j].reshape(1, -1),
                      p['ln_b_g'][j].reshape(1, -1), p['ln_b_b'][j].reshape(1, -1), p['w_out'][j])
            if grp.per_row_mod:
                x, na, nb = _conv_step(grp, u, st_a[j], st_b[j], x, mod3, i, *conv_w)
            else:
                x, na, nb = _conv_seq(grp, u, x, mod3, i, *conv_w)
            new_a.append(na)
            new_b.append(nb)
            x = _ffn(grp, x, mod3, i, gffn, p['w_ffn_gate'][j], p['w_ffn_up'][j], p['w_ffn_down'][j])
        else:
            ctx = p['pool_ctx']
            pool_w = (p['w_pool'][j], p['pool_scale'][j].reshape(1, d), p['wr_hi'][j], p['wr_lo'][j], p['br'][j])
            if grp.per_row_mod:
                x3, h3, mi, mf, nc, cnt = _pool_step(grp, x, st_c[j], mod3, i, gmix, gffn, *pool_w, ctx)
            else:
                x3, h3, mi, mf, nc, cnt = _pool_seq(grp, x, mod3, i, gmix, gffn, *pool_w, ctx)
            new_c.append(nc)
            final = i == depth - 1
            x = _moe_layer(grp, x3, h3, mi, mf, cnt, mod3, i, p['w_exp_gate'][j], p['w_exp_up'][j],
                           p['w_exp_down'][j], p['norm_out'].reshape(1, d), final)
    return x, new_a, new_b, new_c


def kernel(x_prompt, x_sample, state_a, state_b, state_c, c_prompt, c_sample, w_ada, b_ada, norm_mix, norm_ffn, norm_out, w_in, w_conv_a, w_conv_b, b_conv_b, ln_b_g, ln_b_b, w_out, w_ffn_gate, w_ffn_up, w_ffn_down, w_pool, pool_scale, w_router, b_router, w_exp_gate, w_exp_up, w_exp_down):
    n_p, seq, d = x_prompt.shape
    n_s, dec_seq, _ = x_sample.shape
    depth = w_ada.shape[0]
    n_e = w_router.shape[2]
    assert dec_seq == 1 and depth % 2 == 0 and n_s % SUBLANES == 0 and n_e <= LANES
    n_even, n_odd = state_a.shape[0], state_c.shape[0]
    ctx = state_c.shape[2]

    wr = jnp.pad(w_router, ((0, 0), (0, 0), (0, LANES - n_e)))
    wr_hi = wr.astype(BF16)
    wr_lo = (wr - wr_hi.astype(F32)).astype(BF16)
    br = jnp.pad(b_router, ((0, 0), (0, LANES - n_e)), constant_values=NEG_BIG).reshape(n_odd, 1, LANES)

    p = {
        'w_ada': w_ada, 'norm_mix': norm_mix, 'norm_ffn': norm_ffn, 'norm_out': norm_out,
        'w_in': w_in.astype(BF16), 'w_conv_a': w_conv_a, 'w_conv_b': w_conv_b, 'b_conv_b': b_conv_b,
        'ln_b_g': ln_b_g, 'ln_b_b': ln_b_b, 'w_out': w_out.astype(BF16),
        'w_ffn_gate': w_ffn_gate.astype(BF16), 'w_ffn_up': w_ffn_up.astype(BF16),
        'w_ffn_down': w_ffn_down.astype(BF16),
        'w_pool': w_pool.astype(BF16), 'pool_scale': pool_scale, 'wr_hi': wr_hi, 'wr_lo': wr_lo, 'br': br,
        'w_exp_gate': w_exp_gate, 'w_exp_up': w_exp_up, 'w_exp_down': w_exp_down, 'pool_ctx': ctx,
    }

    mod3 = _ada(jnp.concatenate([c_sample, c_prompt], axis=0), w_ada, b_ada)

    g_prompt = _Group(n_p, seq, False, n_s, n_s)
    g_sample = _Group(n_s, 1, True, 0, n_s)

    y_p, pa, pb, pc = _trunk(g_prompt, x_prompt.reshape(n_p * seq, d), mod3, None, None, None, p)
    y_s, sa, sb, sc = _trunk(
        g_sample, x_sample.reshape(n_s, d), mod3,
        state_a.reshape(n_even, n_s, -1), state_b.reshape(n_even, n_s, -1), state_c.reshape(n_odd, n_s, -1), p)

    sa = [a.reshape(n_s, state_a.shape[2], -1) for a in sa]
    sb = [b.reshape(n_s, state_b.shape[2], -1) for b in sb]
    sc = [c.reshape(n_s, ctx, d) for c in sc]
    return (y_p.reshape(n_p, seq, d), y_s.reshape(n_s, 1, d),
            jnp.stack(pa), jnp.stack(sa), jnp.stack(pb), jnp.stack(sb), jnp.stack(pc), jnp.stack(sc))
```

```python
import functools

import jax
import jax.numpy as jnp
from jax import lax
from jax.experimental import pallas as pl
from jax.experimental.pallas import tpu as pltpu

EPS = 1e-6
PAST_LEN = 16384
POOL_WINDOWS = (2, 4, 8, 16)
N_MOD = 6
LANES = 128
SUBLANES = 8
VMEM_LIMIT = 56 * 1024 * 1024
NEG_BIG = -1e30

ROW_TILE = 512
MOE_TILE = 1024
MOE_SUB = 256
MOE_FF = 512
GATHER_TILE = 256

F32 = jnp.float32
BF16 = jnp.bfloat16


def _params(sem):
    return pltpu.CompilerParams(dimension_semantics=sem, vmem_limit_bytes=VMEM_LIMIT)


def _silu(x):
    return x * jax.nn.sigmoid(x)


def _rmsnorm(x, g):
    return x * lax.rsqrt(jnp.mean(x * x, axis=-1, keepdims=True) + EPS) * g


def _mod_rmsnorm(x, g, sc, sh):
    return _rmsnorm(x, g) * (1.0 + sc) + sh


def _resident(shape):
    nd = len(shape)
    return pl.BlockSpec(shape, lambda *_: (0,) * nd, pipeline_mode=pl.Buffered(1))


def _small(shape):
    nd = len(shape)
    return pl.BlockSpec(shape, lambda *_: (0,) * nd)


class _Group:
    def __init__(self, n_seq, seq_len, per_row_mod, mod_row0, n_sample):
        self.n_seq, self.seq_len = n_seq, seq_len
        self.per_row_mod = per_row_mod
        self.mod_row0 = mod_row0
        self.n_sample = n_sample
        if per_row_mod:
            self.tile = n_seq
            self.grid = (1, 1)
        else:
            self.tile = min(ROW_TILE, seq_len)
            assert seq_len % self.tile == 0 and self.tile % 32 == 0
            self.grid = (n_seq, seq_len // self.tile)
        self.rows = n_seq * seq_len

    def row_spec(self, width):
        nt = self.grid[1]
        return pl.BlockSpec((self.tile, width), lambda b, j, *_: (b * nt + j, 0))

    def mod_spec(self, layer, chunk, d):
        if self.per_row_mod:
            return pl.BlockSpec((None, self.n_sample, d), lambda b, j, *_: (layer, 0, chunk))
        r0 = self.mod_row0
        return pl.BlockSpec((None, None, 1, d), lambda b, j, *_: (layer, r0 + b, 0, chunk))

    def mod_arg(self, mod3):
        if self.per_row_mod:
            return mod3
        l, r, w = mod3.shape
        return mod3.reshape(l, r, 1, w)


def _ada_kernel(c_ref, w_ref, b_ref, o_ref):
    cs = _silu(c_ref[...]).astype(BF16)
    o_ref[...] = jnp.dot(cs, w_ref[...].astype(BF16), preferred_element_type=F32) + b_ref[...]


def _ada(c_all, w_ada, b_ada):
    depth, d, w6 = w_ada.shape
    r = c_all.shape[0]
    tn = w6 // 4
    return pl.pallas_call(
        _ada_kernel,
        out_shape=jax.ShapeDtypeStruct((depth, r, w6), F32),
        grid=(depth, w6 // tn),
        in_specs=[pl.BlockSpec((r, d), lambda l, n: (0, 0)),
                  pl.BlockSpec((None, d, tn), lambda l, n: (l, 0, n)),
                  pl.BlockSpec((None, 1, tn), lambda l, n: (l, 0, n))],
        out_specs=pl.BlockSpec((None, r, tn), lambda l, n: (l, 0, n)),
        compiler_params=_params(("arbitrary", "arbitrary")),
        name="ada",
    )(c_all, w_ada, b_ada.reshape(depth, 1, w6))


def _in_proj_kernel(x_ref, g_ref, sc_ref, sh_ref, w_ref, u_ref):
    h = _mod_rmsnorm(x_ref[...], g_ref[...], sc_ref[...], sh_ref[...])
    u_ref[...] = jnp.dot(h.astype(BF16), w_ref[...], preferred_element_type=F32)


def _in_proj(grp, x, mod3, layer, g, w_bf):
    d, n = w_bf.shape
    return pl.pallas_call(
        _in_proj_kernel,
        out_shape=jax.ShapeDtypeStruct((grp.rows, n), F32),
        grid=grp.grid,
        in_specs=[grp.row_spec(d), _small((1, d)),
                  grp.mod_spec(layer, 1, d), grp.mod_spec(layer, 0, d),
                  _resident((d, n))],
        out_specs=grp.row_spec(n),
        compiler_params=_params(("arbitrary", "arbitrary")),
        name="in_proj",
    )(x, g, grp.mod_arg(mod3), grp.mod_arg(mod3), w_bf)


def _layernorm(y, g, b):
    mu = jnp.mean(y, axis=-1, keepdims=True)
    yc = y - mu
    var = jnp.mean(yc * yc, axis=-1, keepdims=True)
    return yc * lax.rsqrt(var + EPS) * g + b


def _conv_seq_kernel(n_t, ka, kb, u_ref, uh_ref, x_ref, gm_ref, wa_ref, wb_ref, bb_ref,
                     lng_ref, lnb_ref, wout_ref, x1_ref, na_ref, nb_ref, apad, gpad):
    j = pl.program_id(1)
    tt = u_ref.shape[0]
    da = wa_ref.shape[1]
    hb = uh_ref.shape[0]
    keep = (j > 0).astype(F32)

    u = u_ref[...]
    a_b, a_c, a_x = u[:, 0:da], u[:, da:2 * da], u[:, 2 * da:3 * da]
    b_v, b_g = u[:, 3 * da:4 * da], u[:, 4 * da:5 * da]
    uh = uh_ref[...]
    apad[0:SUBLANES, :] = keep * (uh[hb - SUBLANES:hb, da:2 * da] * uh[hb - SUBLANES:hb, 2 * da:3 * da])
    apad[SUBLANES:SUBLANES + tt, :] = a_c * a_x
    gpad[0:hb, :] = keep * (uh[:, 3 * da:4 * da] * jax.nn.sigmoid(uh[:, 4 * da:5 * da]))
    gpad[hb:hb + tt, :] = b_v * jax.nn.sigmoid(b_g)

    ya = None
    for k in range(ka):
        term = wa_ref[k:k + 1, :] * apad[SUBLANES - (ka - 1) + k:SUBLANES - (ka - 1) + k + tt, :]
        ya = term if ya is None else ya + term
    ya = a_b * ya
    yb = None
    for k in range(kb):
        term = wb_ref[k:k + 1, :] * gpad[hb - (kb - 1) + k:hb - (kb - 1) + k + tt, :]
        yb = term if yb is None else yb + term
    yb = _silu(_layernorm(yb + bb_ref[...], lng_ref[...], lnb_ref[...]))

    ycat = jnp.concatenate([ya, yb], axis=-1).astype(BF16)
    y = jnp.dot(ycat, wout_ref[...], preferred_element_type=F32)
    x1_ref[...] = x_ref[...] + gm_ref[...] * y

    @pl.when(j == n_t - 1)
    def _():
        na_ref[...] = apad[SUBLANES + tt - (ka - 1):SUBLANES + tt, :]
        nb_ref[...] = gpad[hb + tt - (kb - 1):hb + tt, :]


def _conv_seq(grp, u, x, mod3, layer, wa, wb, bb, lng, lnb, wout_bf):
    tt = grp.tile
    n_b, n_t = grp.grid
    d = x.shape[1]
    ka, da = wa.shape
    kb = wb.shape[0]
    hb = 32
    assert kb - 1 <= hb and ka - 1 <= SUBLANES and tt % hb == 0
    per = tt // hb
    halo = pl.BlockSpec((hb, u.shape[1]), lambda b, j: (jnp.maximum((b * n_t + j) * per - 1, 0), 0))
    return pl.pallas_call(
        functools.partial(_conv_seq_kernel, n_t, ka, kb),
        out_shape=(jax.ShapeDtypeStruct((grp.rows, d), F32),
                   jax.ShapeDtypeStruct((n_b, ka - 1, da), F32),
                   jax.ShapeDtypeStruct((n_b, kb - 1, da), F32)),
        grid=grp.grid,
        in_specs=[grp.row_spec(u.shape[1]), halo, grp.row_spec(d), grp.mod_spec(layer, 2, d),
                  _small(wa.shape), _small(wb.shape), _small((1, da)), _small((1, da)), _small((1, da)),
                  _resident(wout_bf.shape)],
        out_specs=(grp.row_spec(d),
                   pl.BlockSpec((None, ka - 1, da), lambda b, j: (b, 0, 0)),
                   pl.BlockSpec((None, kb - 1, da), lambda b, j: (b, 0, 0))),
        scratch_shapes=[pltpu.VMEM((SUBLANES + tt, da), F32), pltpu.VMEM((hb + tt, da), F32)],
        compiler_params=_params(("arbitrary", "arbitrary")),
        name="conv_seq",
    )(u, u, x, grp.mod_arg(mod3), wa, wb, bb, lng, lnb, wout_bf)


def _conv_step_kernel(ka, kb, u_ref, sa_ref, sb_ref, x_ref, gm_ref, wa_ref, wb_ref, bb_ref,
                      lng_ref, lnb_ref, wout_ref, x1_ref, na_ref, nb_ref):
    da = wa_ref.shape[1]
    u = u_ref[...]
    a_b, a_c, a_x = u[:, 0:da], u[:, da:2 * da], u[:, 2 * da:3 * da]
    b_v, b_g = u[:, 3 * da:4 * da], u[:, 4 * da:5 * da]

    cur = a_c * a_x
    ya = wa_ref[ka - 1:ka, :] * cur
    for k in range(ka - 1):
        ya = ya + wa_ref[k:k + 1, :] * sa_ref[:, k * da:(k + 1) * da]
    ya = a_b * ya
    glu = b_v * jax.nn.sigmoid(b_g)
    yb = wb_ref[kb - 1:kb, :] * glu
    for k in range(kb - 1):
        yb = yb + wb_ref[k:k + 1, :] * sb_ref[:, k * da:(k + 1) * da]
    yb = _silu(_layernorm(yb + bb_ref[...], lng_ref[...], lnb_ref[...]))

    ycat = jnp.concatenate([ya, yb], axis=-1).astype(BF16)
    y = jnp.dot(ycat, wout_ref[...], preferred_element_type=F32)
    x1_ref[...] = x_ref[...] + gm_ref[...] * y

    if ka > 2:
        na_ref[:, 0:(ka - 2) * da] = sa_ref[:, da:(ka - 1) * da]
    na_ref[:, (ka - 2) * da:(ka - 1) * da] = cur
    if kb > 2:
        nb_ref[:, 0:(kb - 2) * da] = sb_ref[:, da:(kb - 1) * da]
    nb_ref[:, (kb - 2) * da:(kb - 1) * da] = glu


def _conv_step(grp, u, sa2, sb2, x, mod3, layer, wa, wb, bb, lng, lnb, wout_bf):
    n = grp.rows
    d = x.shape[1]
    ka, da = wa.shape
    kb = wb.shape[0]
    return pl.pallas_call(
        functools.partial(_conv_step_kernel, ka, kb),
        out_shape=(jax.ShapeDtypeStruct((n, d), F32),
                   jax.ShapeDtypeStruct(sa2.shape, F32),
                   jax.ShapeDtypeStruct(sb2.shape, F32)),
        grid=grp.grid,
        in_specs=[_small(u.shape), _small(sa2.shape), _small(sb2.shape), _small(x.shape),
                  grp.mod_spec(layer, 2, d),
                  _small(wa.shape), _small(wb.shape), _small((1, da)), _small((1, da)), _small((1, da)),
                  _small(wout_bf.shape)],
        out_specs=(_small((n, d)), _small(sa2.shape), _small(sb2.shape)),
        compiler_params=_params(("arbitrary", "arbitrary")),
        name="conv_step",
    )(u, sa2, sb2, x, grp.mod_arg(mod3), wa, wb, bb, lng, lnb, wout_bf)


def _ffn_kernel(x_ref, g_ref, sc_ref, sh_ref, gate_ref, wg_ref, wu_ref, wd_ref, o_ref):
    x = x_ref[...]
    h = _mod_rmsnorm(x, g_ref[...], sc_ref[...], sh_ref[...]).astype(BF16)
    a = jnp.dot(h, wg_ref[...], preferred_element_type=F32)
    b = jnp.dot(h, wu_ref[...], preferred_element_type=F32)
    act = (_silu(a) * b).astype(BF16)
    f = jnp.dot(act, wd_ref[...], preferred_element_type=F32)
    o_ref[...] = x + gate_ref[...] * f


def _ffn(grp, x, mod3, layer, g, wg_bf, wu_bf, wd_bf):
    d = x.shape[1]
    return pl.pallas_call(
        _ffn_kernel,
        out_shape=jax.ShapeDtypeStruct(x.shape, F32),
        grid=grp.grid,
        in_specs=[grp.row_spec(d), _small((1, d)),
                  grp.mod_spec(layer, 4, d), grp.mod_spec(layer, 3, d), grp.mod_spec(layer, 5, d),
                  _resident(wg_bf.shape), _resident(wu_bf.shape), _resident(wd_bf.shape)],
        out_specs=grp.row_spec(d),
        compiler_params=_params(("arbitrary", "arbitrary")),
        name="ffn",
    )(x, g, grp.mod_arg(mod3), grp.mod_arg(mod3), grp.mod_arg(mod3), wg_bf, wu_bf, wd_bf)


def _pool_project(diff, wp_ref):
    n_g, pg, _ = wp_ref.shape
    outs = [jnp.dot(diff[:, gi * pg:(gi + 1) * pg].astype(BF16), wp_ref[gi], preferred_element_type=F32)
            for gi in range(n_g)]
    return jnp.concatenate(outs, axis=-1)


def _route(h3, wrh_ref, wrl_ref, br_ref, tri_ref, base_ref, mi_ref, mf_ref, cnt_ref):
    tt = h3.shape[0]
    h_hi = h3.astype(BF16)
    h_lo = (h3 - h_hi.astype(F32)).astype(BF16)
    logits = (jnp.dot(h_hi, wrh_ref[...], preferred_element_type=F32)
              + jnp.dot(h_lo, wrh_ref[...], preferred_element_type=F32)
              + jnp.dot(h_hi, wrl_ref[...], preferred_element_type=F32)
              + br_ref[...])
    lane = lax.broadcasted_iota(jnp.int32, (tt, LANES), 1).astype(F32)
    m0 = jnp.max(logits, axis=-1, keepdims=True)
    e0 = jnp.min(jnp.where(logits == m0, lane, float(LANES)), axis=-1, keepdims=True)
    rest = jnp.where(lane == e0, NEG_BIG * 2.0, logits)
    m1 = jnp.max(rest, axis=-1, keepdims=True)
    e1 = jnp.min(jnp.where(rest == m1, lane, float(LANES)), axis=-1, keepdims=True)
    dd = jnp.exp(m1 - m0)
    p0 = 1.0 / (1.0 + dd)
    p1 = dd * p0

    oh0 = lane == e0
    oh1 = lane == e1
    c = jnp.where(oh0 | oh1, 1.0, 0.0)
    prefix = jnp.dot(tri_ref[...], c.astype(BF16), preferred_element_type=F32)
    tot = base_ref[...] + prefix
    r0 = jnp.sum(jnp.where(oh0, tot, 0.0), axis=-1, keepdims=True)
    r1 = jnp.sum(jnp.where(oh1, tot, 0.0), axis=-1, keepdims=True)
    new_base = base_ref[...] + jnp.sum(c, axis=0, keepdims=True)
    base_ref[...] = new_base
    cnt_ref[...] = jnp.broadcast_to(new_base, cnt_ref.shape)

    meta = jnp.where(lane == 0.0, e0, jnp.where(lane == 1.0, e1,
                     jnp.where(lane == 2.0, r0, jnp.where(lane == 3.0, r1, 0.0))))
    mi_ref[...] = meta.astype(jnp.int32)
    mf_ref[...] = jnp.where(lane == 0.0, p0, jnp.where(lane == 1.0, p1, 0.0))


def _init_route_scratch(first, tri_ref, base_ref):
    @pl.when(first)
    def _():
        tt = tri_ref.shape[0]
        r = lax.broadcasted_iota(jnp.int32, (tt, tt), 0)
        c = lax.broadcasted_iota(jnp.int32, (tt, tt), 1)
        tri_ref[...] = jnp.where(c < r, 1.0, 0.0).astype(BF16)
        base_ref[...] = jnp.zeros(base_ref.shape, F32)


def _pool_seq_kernel(n_t, ctx, x_ref, xh_ref, gmix_ref, shm_ref, scm_ref, gm_ref, gffn_ref, shf_ref, scf_ref,
                     wp_ref, ps_ref, wrh_ref, wrl_ref, br_ref,
                     x3_ref, h3_ref, mi_ref, mf_ref, nc_ref, cnt_ref, hpad, tri_ref, base_ref):
    b = pl.program_id(0)
    j = pl.program_id(1)
    tt = x_ref.shape[0]
    hb = xh_ref.shape[0]
    n_g, pg, _ = wp_ref.shape
    _init_route_scratch((b == 0) & (j == 0), tri_ref, base_ref)

    keep = (j > 0).astype(F32)
    x = x_ref[...]
    h = _mod_rmsnorm(x, gmix_ref[...], scm_ref[...], shm_ref[...])
    hpad[0:hb, :] = keep * _mod_rmsnorm(xh_ref[...], gmix_ref[...], scm_ref[...], shm_ref[...])
    hpad[hb:hb + tt, :] = h

    pos = lax.broadcasted_iota(jnp.int32, (tt, pg), 0) + j * tt
    groups = []
    for gi, win in enumerate(POOL_WINDOWS):
        sl = slice(gi * pg, (gi + 1) * pg)
        s = h[:, sl]
        for i in range(1, win):
            s = s + hpad[hb - i:hb - i + tt, sl]
        cnt = jnp.minimum(pos + 1, win).astype(F32)
        groups.append(s / cnt - h[:, sl])
    diff = jnp.concatenate(groups, axis=-1)
    y = _pool_project(diff, wp_ref) * ps_ref[...]
    x3 = x + gm_ref[...] * y
    x3_ref[...] = x3
    h3 = _mod_rmsnorm(x3, gffn_ref[...], scf_ref[...], shf_ref[...])
    h3_ref[...] = h3
    _route(h3, wrh_ref, wrl_ref, br_ref, tri_ref, base_ref, mi_ref, mf_ref, cnt_ref)

    @pl.when(j == n_t - 1)
    def _():
        nc_ref[...] = hpad[hb + tt - ctx:hb + tt, :]


def _route_out_shapes(rows, d):
    return (jax.ShapeDtypeStruct((rows, d), F32), jax.ShapeDtypeStruct((rows, d), F32),
            jax.ShapeDtypeStruct((rows, LANES), jnp.int32), jax.ShapeDtypeStruct((rows, LANES), F32))


def _pool_seq(grp, x, mod3, layer, gmix, gffn, wp_bf, ps, wrh, wrl, br, ctx):
    tt = grp.tile
    n_b, n_t = grp.grid
    d = x.shape[1]
    hb = 16
    assert ctx <= hb and max(POOL_WINDOWS) - 1 <= hb and tt % hb == 0
    per = tt // hb
    halo = pl.BlockSpec((hb, d), lambda b, j: (jnp.maximum((b * n_t + j) * per - 1, 0), 0))
    ms = lambda c: grp.mod_spec(layer, c, d)
    return pl.pallas_call(
        functools.partial(_pool_seq_kernel, n_t, ctx),
        out_shape=_route_out_shapes(grp.rows, d) + (
            jax.ShapeDtypeStruct((n_b, ctx, d), F32), jax.ShapeDtypeStruct((SUBLANES, LANES), F32)),
        grid=grp.grid,
        in_specs=[grp.row_spec(d), halo, _small((1, d)), ms(0), ms(1), ms(2), _small((1, d)), ms(3), ms(4),
                  _small(wp_bf.shape), _small((1, d)), _small(wrh.shape), _small(wrl.shape), _small(br.shape)],
        out_specs=(grp.row_spec(d), grp.row_spec(d), grp.row_spec(LANES), grp.row_spec(LANES),
                   pl.BlockSpec((None, ctx, d), lambda b, j: (b, 0, 0)), _small((SUBLANES, LANES))),
        scratch_shapes=[pltpu.VMEM((hb + tt, d), F32), pltpu.VMEM((tt, tt), BF16), pltpu.VMEM((1, LANES), F32)],
        compiler_params=_params(("arbitrary", "arbitrary")),
        name="pool_seq",
    )(x, x, gmix, *([grp.mod_arg(mod3)] * 3), gffn, *([grp.mod_arg(mod3)] * 2), wp_bf, ps, wrh, wrl, br)


def _pool_step_kernel(ctx, x_ref, sc_ref_state, gmix_ref, shm_ref, scm_ref, gm_ref, gffn_ref, shf_ref, scf_ref,
                      wp_ref, ps_ref, wrh_ref, wrl_ref, br_ref,
                      x3_ref, h3_ref, mi_ref, mf_ref, nc_ref, cnt_ref, tri_ref, base_ref):
    d = x_ref.shape[1]
    n_g, pg, _ = wp_ref.shape
    _init_route_scratch(pl.program_id(0) == 0, tri_ref, base_ref)

    x = x_ref[...]
    h = _mod_rmsnorm(x, gmix_ref[...], scm_ref[...], shm_ref[...])
    groups = []
    for gi, win in enumerate(POOL_WINDOWS):
        s = h[:, gi * pg:(gi + 1) * pg]
        for i in range(1, win):
            row = ctx - i
            s = s + sc_ref_state[:, row * d + gi * pg:row * d + (gi + 1) * pg]
        cnt = float(min(PAST_LEN + 1, win))
        groups.append(s / cnt - h[:, gi * pg:(gi + 1) * pg])
    diff = jnp.concatenate(groups, axis=-1)
    y = _pool_project(diff, wp_ref) * ps_ref[...]
    x3 = x + gm_ref[...] * y
    x3_ref[...] = x3
    h3 = _mod_rmsnorm(x3, gffn_ref[...], scf_ref[...], shf_ref[...])
    h3_ref[...] = h3
    _route(h3, wrh_ref, wrl_ref, br_ref, tri_ref, base_ref, mi_ref, mf_ref, cnt_ref)

    if ctx > 1:
        nc_ref[:, 0:(ctx - 1) * d] = sc_ref_state[:, d:ctx * d]
    nc_ref[:, (ctx - 1) * d:ctx * d] = h


def _pool_step(grp, x, sc2, mod3, layer, gmix, gffn, wp_bf, ps, wrh, wrl, br, ctx):
    n, d = x.shape
    ms = lambda c: grp.mod_spec(layer, c, d)
    return pl.pallas_call(
        functools.partial(_pool_step_kernel, ctx),
        out_shape=_route_out_shapes(n, d) + (
            jax.ShapeDtypeStruct(sc2.shape, F32), jax.ShapeDtypeStruct((SUBLANES, LANES), F32)),
        grid=grp.grid,
        in_specs=[_small(x.shape), _small(sc2.shape), _small((1, d)), ms(0), ms(1), ms(2), _small((1, d)), ms(3), ms(4),
                  _small(wp_bf.shape), _small((1, d)), _small(wrh.shape), _small(wrl.shape), _small(br.shape)],
        out_specs=(_small((n, d)), _small((n, d)), _small((n, LANES)), _small((n, LANES)),
                   _small(sc2.shape), _small((SUBLANES, LANES))),
        scratch_shapes=[pltpu.VMEM((n, n), BF16), pltpu.VMEM((1, LANES), F32)],
        compiler_params=_params(("arbitrary", "arbitrary")),
        name="pool_step",
    )(x, sc2, gmix, *([grp.mod_arg(mod3)] * 3), gffn, *([grp.mod_arg(mod3)] * 2), wp_bf, ps, wrh, wrl, br)


def _row_copy(src_hbm, src_row, dst, dst_row, sem):
    return pltpu.make_async_copy(src_hbm.at[pl.ds(src_row, 1), :], dst.at[pl.ds(dst_row, 1), :], sem)


def _dispatch_kernel(pos_ref, h_ref, xs_hbm, sem):
    td = pos_ref.shape[1] // 2

    def copies(t):
        return (_row_copy(h_ref, t, xs_hbm, pos_ref[0, t], sem),
                _row_copy(h_ref, t, xs_hbm, pos_ref[0, td + t], sem))

    def start(t, carry):
        for k, cp in enumerate(copies(t)):
            cp.start(priority=k)
        return carry

    def wait(t, carry):
        for cp in copies(t):
            cp.wait()
        return carry

    lax.fori_loop(0, td, start, 0)
    lax.fori_loop(0, td, wait, 0)


def _dispatch(h3, pos3, n_slots):
    m, d = h3.shape
    n_tiles, _, td2 = pos3.shape
    return pl.pallas_call(
        _dispatch_kernel,
        out_shape=jax.ShapeDtypeStruct((n_slots, d), F32),
        grid=(n_tiles,),
        in_specs=[pl.BlockSpec((None, 1, td2), lambda i: (i, 0, 0), memory_space=pltpu.SMEM),
                  pl.BlockSpec((td2 // 2, d), lambda i: (i, 0))],
        out_specs=pl.BlockSpec(memory_space=pl.ANY),
        scratch_shapes=[pltpu.SemaphoreType.DMA(())],
        compiler_params=_params(("arbitrary",)),
        name="dispatch",
    )(pos3, h3)


def _moe_kernel(sub, we_ref, wt_ref, wlo_ref, whi_ref, wfirst_ref,
                xs_ref, wg_ref, wu_ref, wd_ref, ys_ref, xb, wgb, wub, wdb):
    w = pl.program_id(0)
    j = pl.program_id(1)
    tm = xs_ref.shape[0]
    lo = wlo_ref[w]
    hi = whi_ref[w]

    @pl.when((j == 0) & (wfirst_ref[w] == 1))
    def _():
        ys_ref[...] = jnp.zeros(ys_ref.shape, F32)

    @pl.when(hi > lo)
    def _():
        @pl.when(j == 0)
        def _():
            rows = lax.broadcasted_iota(jnp.int32, xs_ref.shape, 0)
            xb[...] = jnp.where((rows >= lo) & (rows < hi), xs_ref[...], 0.0).astype(BF16)

        def swiglu(x, wg, wu, wd):
            a = jnp.dot(x, wg, preferred_element_type=F32)
            b = jnp.dot(x, wu, preferred_element_type=F32)
            return jnp.dot((_silu(a) * b).astype(BF16), wd, preferred_element_type=F32)

        whole = (lo == 0) & (hi == tm)

        @pl.when(whole)
        def _():
            ys_ref[...] += swiglu(xb[...], wg_ref[...].astype(BF16), wu_ref[...].astype(BF16),
                                  wd_ref[...].astype(BF16))

        @pl.when(jnp.logical_not(whole))
        def _():
            wgb[...] = wg_ref[...].astype(BF16)
            wub[...] = wu_ref[...].astype(BF16)
            wdb[...] = wd_ref[...].astype(BF16)
            for s in range(tm // sub):
                @pl.when((s * sub < hi) & ((s + 1) * sub > lo))
                def _():
                    rs = slice(s * sub, (s + 1) * sub)
                    ys_ref[rs, :] += swiglu(xb[rs, :], wgb[...], wub[...], wdb[...])


def _moe(xs, work, wg, wu, wd, tm):
    n_slots, d = xs.shape
    n_e, _, ff = wg.shape
    fc = min(MOE_FF, ff)
    assert ff % fc == 0 and n_slots % tm == 0
    n_fc = ff // fc
    sub = min(MOE_SUB, tm)
    n_work = work[0].shape[0]

    def jj(w, j, wlo, whi):
        return jnp.where(whi[w] > wlo[w], j, n_fc - 1)

    grid_spec = pltpu.PrefetchScalarGridSpec(
        num_scalar_prefetch=5,
        grid=(n_work, n_fc),
        in_specs=[pl.BlockSpec((tm, d), lambda w, j, we, wt, wlo, whi, wf: (wt[w], 0)),
                  pl.BlockSpec((None, d, fc), lambda w, j, we, wt, wlo, whi, wf: (we[w], 0, jj(w, j, wlo, whi))),
                  pl.BlockSpec((None, d, fc), lambda w, j, we, wt, wlo, whi, wf: (we[w], 0, jj(w, j, wlo, whi))),
                  pl.BlockSpec((None, fc, d), lambda w, j, we, wt, wlo, whi, wf: (we[w], jj(w, j, wlo, whi), 0))],
        out_specs=pl.BlockSpec((tm, d), lambda w, j, we, wt, wlo, whi, wf: (wt[w], 0)),
        scratch_shapes=[pltpu.VMEM((tm, d), BF16), pltpu.VMEM((d, fc), BF16),
                        pltpu.VMEM((d, fc), BF16), pltpu.VMEM((fc, d), BF16)])
    return pl.pallas_call(
        functools.partial(_moe_kernel, sub),
        out_shape=jax.ShapeDtypeStruct((n_slots, d), F32),
        grid_spec=grid_spec,
        compiler_params=_params(("arbitrary", "arbitrary")),
        name="moe",
    )(*work, xs, wg, wu, wd)


def _moe_work_items(counts, tm, n_tiles):
    n_e = counts.shape[0]
    n_work = n_tiles + n_e - 1
    uend = jnp.cumsum(counts)
    ustart = uend - counts
    first_tile = ustart // tm
    n_w = jnp.where(counts > 0, (uend - 1) // tm - first_tile + 1, 0)
    wend = jnp.cumsum(n_w)
    wstart = wend - n_w
    total = wend[-1]
    w = jnp.minimum(jnp.arange(n_work, dtype=jnp.int32), total - 1)
    we = jnp.sum((w[:, None] >= wend[None, :]).astype(jnp.int32), axis=1)
    wt = first_tile[we] + (w - wstart[we])
    lo = jnp.clip(ustart[we] - wt * tm, 0, tm)
    hi = jnp.clip(uend[we] - wt * tm, 0, tm)
    live = jnp.arange(n_work, dtype=jnp.int32) < total
    hi = jnp.where(live, hi, lo)
    prev_t = jnp.concatenate([jnp.full((1,), -1, jnp.int32), wt[:-1]])
    wfirst = (live & (wt != prev_t)).astype(jnp.int32)
    as_i32 = lambda a: a.astype(jnp.int32)
    return ustart, (as_i32(we), as_i32(wt), as_i32(lo), as_i32(hi), wfirst)


def _combine_kernel(final, pos_ref, mf_ref, x3_ref, gate_ref, gout_ref, ys_hbm, o_ref, buf, sem):
    tc = x3_ref.shape[0]

    def copies(t):
        return (_row_copy(ys_hbm, pos_ref[0, t], buf.at[0], t, sem),
                _row_copy(ys_hbm, pos_ref[0, tc + t], buf.at[1], t, sem))

    def start(t, carry):
        for k, cp in enumerate(copies(t)):
            cp.start(priority=k)
        return carry

    def wait(t, carry):
        for cp in copies(t):
            cp.wait()
        return carry

    lax.fori_loop(0, tc, start, 0)
    lax.fori_loop(0, tc, wait, 0)
    mf = mf_ref[...]
    moe = mf[:, 0:1] * buf[0] + mf[:, 1:2] * buf[1]
    x4 = x3_ref[...] + gate_ref[...] * moe
    o_ref[...] = _rmsnorm(x4, gout_ref[...]) if final else x4


def _combine(grp, ys, pos3, mf, x3, mod3, layer, gout, final):
    m, d = x3.shape
    tc = pos3.shape[2] // 2
    n_b, n_t = grp.grid
    per = grp.tile // tc
    assert grp.tile % tc == 0
    grid = (n_b, n_t * per)
    row = lambda w: pl.BlockSpec((tc, w), lambda b, j: (b * (n_t * per) + j, 0))
    if grp.per_row_mod:
        gate = grp.mod_spec(layer, 5, d)
    else:
        r0 = grp.mod_row0
        gate = pl.BlockSpec((None, None, 1, d), lambda b, j: (layer, r0 + b, 0, 5))
    return pl.pallas_call(
        functools.partial(_combine_kernel, final),
        out_shape=jax.ShapeDtypeStruct((m, d), F32),
        grid=grid,
        in_specs=[pl.BlockSpec((None, 1, 2 * tc), lambda b, j: (b * (n_t * per) + j, 0, 0), memory_space=pltpu.SMEM),
                  row(LANES), row(d), gate, _small((1, d)), pl.BlockSpec(memory_space=pl.ANY)],
        out_specs=row(d),
        scratch_shapes=[pltpu.VMEM((2, tc, d), F32), pltpu.SemaphoreType.DMA(())],
        compiler_params=_params(("arbitrary", "arbitrary")),
        name="combine",
    )(pos3, mf, x3, grp.mod_arg(mod3), gout, ys)


def _moe_layer(grp, x3, h3, mi, mf, cnt, mod3, layer, wg, wu, wd, gout, final):
    m, d = h3.shape
    n_e = wg.shape[0]
    tm = min(MOE_TILE, m)
    td = min(GATHER_TILE, m)
    n_slots = 2 * m
    assert n_slots % tm == 0 and m % td == 0
    counts = cnt[0, :n_e].astype(jnp.int32)
    ustart, work = _moe_work_items(counts, tm, n_slots // tm)

    e = mi[:, 0:2]
    pos = mi[:, 2:4]
    for k in range(n_e):
        pos = pos + jnp.where(e == k, ustart[k], 0)
    pos3 = pos.reshape(m // td, td, 2).transpose(0, 2, 1).reshape(m // td, 1, 2 * td)

    xs = _dispatch(h3, pos3, n_slots)
    ys = _moe(xs, work, wg, wu, wd, tm)
    return _combine(grp, ys, pos3, mf, x3, mod3, layer, gout, final)


def _trunk(grp, x, mod3, st_a, st_b, st_c, p):
    depth = p['w_ada'].shape[0]
    d = x.shape[1]
    new_a, new_b, new_c = [], [], []
    for i in range(depth):
        j = i // 2
        gmix = p['norm_mix'][i].reshape(1, d)
        gffn = p['norm_ffn'][i].reshape(1, d)
        if i % 2 == 0:
            u = _in_proj(grp, x, mod3, i, gmix, p['w_in'][j])
            conv_w = (p['w_conv_a'][j], p['w_conv_b'][j], p['b_conv_b'][j].reshape(1, -1),
                      p['ln_b_g'][j].reshape(1, -1), p['ln_b_b'][j].reshape(1, -1), p['w_out'][j])
            if grp.per_row_mod:
                x, na, nb = _conv_step(grp, u, st_a[j], st_b[j], x, mod3, i, *conv_w)
            else:
                x, na, nb = _conv_seq(grp, u, x, mod3, i, *conv_w)
            new_a.append(na)
            new_b.append(nb)
            x = _ffn(grp, x, mod3, i, gffn, p['w_ffn_gate'][j], p['w_ffn_up'][j], p['w_ffn_down'][j])
        else:
            ctx = p['pool_ctx']
            pool_w = (p['w_pool'][j], p['pool_scale'][j].reshape(1, d), p['wr_hi'][j], p['wr_lo'][j], p['br'][j])
            if grp.per_row_mod:
                x3, h3, mi, mf, nc, cnt = _pool_step(grp, x, st_c[j], mod3, i, gmix, gffn, *pool_w, ctx)
            else:
                x3, h3, mi, mf, nc, cnt = _pool_seq(grp, x, mod3, i, gmix, gffn, *pool_w, ctx)
            new_c.append(nc)
            final = i == depth - 1
            x = _moe_layer(grp, x3, h3, mi, mf, cnt, mod3, i, p['w_exp_gate'][j], p['w_exp_up'][j],
                           p['w_exp_down'][j], p['norm_out'].reshape(1, d), final)
    return x, new_a, new_b, new_c


def kernel(x_prompt, x_sample, state_a, state_b, state_c, c_prompt, c_sample, w_ada, b_ada, norm_mix, norm_ffn, norm_out, w_in, w_conv_a, w_conv_b, b_conv_b, ln_b_g, ln_b_b, w_out, w_ffn_gate, w_ffn_up, w_ffn_down, w_pool, pool_scale, w_router, b_router, w_exp_gate, w_exp_up, w_exp_down):
    n_p, seq, d = x_prompt.shape
    n_s, dec_seq, _ = x_sample.shape
    depth = w_ada.shape[0]
    n_e = w_router.shape[2]
    assert dec_seq == 1 and depth % 2 == 0 and n_s % SUBLANES == 0 and n_e <= LANES
    n_even, n_odd = state_a.shape[0], state_c.shape[0]
    ctx = state_c.shape[2]

    wr = jnp.pad(w_router, ((0, 0), (0, 0), (0, LANES - n_e)))
    wr_hi = wr.astype(BF16)
    wr_lo = (wr - wr_hi.astype(F32)).astype(BF16)
    br = jnp.pad(b_router, ((0, 0), (0, LANES - n_e)), constant_values=NEG_BIG).reshape(n_odd, 1, LANES)

    p = {
        'w_ada': w_ada, 'norm_mix': norm_mix, 'norm_ffn': norm_ffn, 'norm_out': norm_out,
        'w_in': w_in.astype(BF16), 'w_conv_a': w_conv_a, 'w_conv_b': w_conv_b, 'b_conv_b': b_conv_b,
        'ln_b_g': ln_b_g, 'ln_b_b': ln_b_b, 'w_out': w_out.astype(BF16),
        'w_ffn_gate': w_ffn_gate.astype(BF16), 'w_ffn_up': w_ffn_up.astype(BF16),
        'w_ffn_down': w_ffn_down.astype(BF16),
        'w_pool': w_pool.astype(BF16), 'pool_scale': pool_scale, 'wr_hi': wr_hi, 'wr_lo': wr_lo, 'br': br,
        'w_exp_gate': w_exp_gate, 'w_exp_up': w_exp_up, 'w_exp_down': w_exp_down, 'pool_ctx': ctx,
    }

    mod3 = _ada(jnp.concatenate([c_sample, c_prompt], axis=0), w_ada, b_ada)

    g_prompt = _Group(n_p, seq, False, n_s, n_s)
    g_sample = _Group(n_s, 1, True, 0, n_s)

    y_p, pa, pb, pc = _trunk(g_prompt, x_prompt.reshape(n_p * seq, d), mod3, None, None, None, p)
    y_s, sa, sb, sc = _trunk(
        g_sample, x_sample.reshape(n_s, d), mod3,
        state_a.reshape(n_even, n_s, -1), state_b.reshape(n_even, n_s, -1), state_c.reshape(n_odd, n_s, -1), p)

    sa = [a.reshape(n_s, state_a.shape[2], -1) for a in sa]
    sb = [b.reshape(n_s, state_b.shape[2], -1) for b in sb]
    sc = [c.reshape(n_s, ctx, d) for c in sc]
    return (y_p.reshape(n_p, seq, d), y_s.reshape(n_s, 1, d),
            jnp.stack(pa), jnp.stack(sa), jnp.stack(pb), jnp.stack(sb), jnp.stack(pc), jnp.stack(sc))
```

```python
import functools

import jax
import jax.numpy as jnp
from jax import lax
from jax.experimental import pallas as pl
from jax.experimental.pallas import tpu as pltpu

EPS = 1e-6
PAST_LEN = 16384
POOL_WINDOWS = (2, 4, 8, 16)
N_MOD = 6
LANES = 128
SUBLANES = 8
VMEM_LIMIT = 56 * 1024 * 1024
NEG_BIG = -1e30

ROW_TILE = 512
MOE_TILE = 1024
MOE_SUB = 256
MOE_FF = 512

F32 = jnp.float32
BF16 = jnp.bfloat16


def _params(sem):
    return pltpu.CompilerParams(dimension_semantics=sem, vmem_limit_bytes=VMEM_LIMIT)


def _silu(x):
    return x * jax.nn.sigmoid(x)


def _rmsnorm(x, g):
    return x * lax.rsqrt(jnp.mean(x * x, axis=-1, keepdims=True) + EPS) * g


def _mod_rmsnorm(x, g, sc, sh):
    return _rmsnorm(x, g) * (1.0 + sc) + sh


def _resident(shape):
    nd = len(shape)
    return pl.BlockSpec(shape, lambda *_: (0,) * nd, pipeline_mode=pl.Buffered(1))


def _small(shape):
    nd = len(shape)
    return pl.BlockSpec(shape, lambda *_: (0,) * nd)


class _Group:
    def __init__(self, n_seq, seq_len, per_row_mod, mod_row0, n_sample):
        self.n_seq, self.seq_len = n_seq, seq_len
        self.per_row_mod = per_row_mod
        self.mod_row0 = mod_row0
        self.n_sample = n_sample
        if per_row_mod:
            self.tile = n_seq
            self.grid = (1, 1)
        else:
            self.tile = min(ROW_TILE, seq_len)
            assert seq_len % self.tile == 0 and self.tile % 32 == 0
            self.grid = (n_seq, seq_len // self.tile)
        self.rows = n_seq * seq_len

    def row_spec(self, width):
        nt = self.grid[1]
        return pl.BlockSpec((self.tile, width), lambda b, j, *_: (b * nt + j, 0))

    def mod_spec(self, layer, chunk, d):
        if self.per_row_mod:
            return pl.BlockSpec((None, self.n_sample, d), lambda b, j, *_: (layer, 0, chunk))
        r0 = self.mod_row0
        return pl.BlockSpec((None, None, 1, d), lambda b, j, *_: (layer, r0 + b, 0, chunk))

    def mod_arg(self, mod3):
        if self.per_row_mod:
            return mod3
        l, r, w = mod3.shape
        return mod3.reshape(l, r, 1, w)


def _ada_kernel(c_ref, w_ref, b_ref, o_ref):
    cs = _silu(c_ref[...]).astype(BF16)
    o_ref[...] = jnp.dot(cs, w_ref[...].astype(BF16), preferred_element_type=F32) + b_ref[...]


def _ada(c_all, w_ada, b_ada):
    depth, d, w6 = w_ada.shape
    r = c_all.shape[0]
    tn = w6 // 4
    return pl.pallas_call(
        _ada_kernel,
        out_shape=jax.ShapeDtypeStruct((depth, r, w6), F32),
        grid=(depth, w6 // tn),
        in_specs=[pl.BlockSpec((r, d), lambda l, n: (0, 0)),
                  pl.BlockSpec((None, d, tn), lambda l, n: (l, 0, n)),
                  pl.BlockSpec((None, 1, tn), lambda l, n: (l, 0, n))],
        out_specs=pl.BlockSpec((None, r, tn), lambda l, n: (l, 0, n)),
        compiler_params=_params(("arbitrary", "arbitrary")),
        name="ada",
    )(c_all, w_ada, b_ada.reshape(depth, 1, w6))


def _in_proj_kernel(x_ref, g_ref, sc_ref, sh_ref, w_ref, u_ref):
    h = _mod_rmsnorm(x_ref[...], g_ref[...], sc_ref[...], sh_ref[...])
    u_ref[...] = jnp.dot(h.astype(BF16), w_ref[...], preferred_element_type=F32)


def _in_proj(grp, x, mod3, layer, g, w_bf):
    d, n = w_bf.shape
    return pl.pallas_call(
        _in_proj_kernel,
        out_shape=jax.ShapeDtypeStruct((grp.rows, n), F32),
        grid=grp.grid,
        in_specs=[grp.row_spec(d), _small((1, d)),
                  grp.mod_spec(layer, 1, d), grp.mod_spec(layer, 0, d),
                  _resident((d, n))],
        out_specs=grp.row_spec(n),
        compiler_params=_params(("arbitrary", "arbitrary")),
        name="in_proj",
    )(x, g, grp.mod_arg(mod3), grp.mod_arg(mod3), w_bf)


def _layernorm(y, g, b):
    mu = jnp.mean(y, axis=-1, keepdims=True)
    yc = y - mu
    var = jnp.mean(yc * yc, axis=-1, keepdims=True)
    return yc * lax.rsqrt(var + EPS) * g + b


def _conv_seq_kernel(n_t, ka, kb, u_ref, uh_ref, x_ref, gm_ref, wa_ref, wb_ref, bb_ref,
                     lng_ref, lnb_ref, wout_ref, x1_ref, na_ref, nb_ref, apad, gpad):
    j = pl.program_id(1)
    tt = u_ref.shape[0]
    da = wa_ref.shape[1]
    hb = uh_ref.shape[0]
    keep = (j > 0).astype(F32)

    u = u_ref[...]
    a_b, a_c, a_x = u[:, 0:da], u[:, da:2 * da], u[:, 2 * da:3 * da]
    b_v, b_g = u[:, 3 * da:4 * da], u[:, 4 * da:5 * da]
    uh = uh_ref[...]
    apad[0:SUBLANES, :] = keep * (uh[hb - SUBLANES:hb, da:2 * da] * uh[hb - SUBLANES:hb, 2 * da:3 * da])
    apad[SUBLANES:SUBLANES + tt, :] = a_c * a_x
    gpad[0:hb, :] = keep * (uh[:, 3 * da:4 * da] * jax.nn.sigmoid(uh[:, 4 * da:5 * da]))
    gpad[hb:hb + tt, :] = b_v * jax.nn.sigmoid(b_g)

    ya = None
    for k in range(ka):
        term = wa_ref[k:k + 1, :] * apad[SUBLANES - (ka - 1) + k:SUBLANES - (ka - 1) + k + tt, :]
        ya = term if ya is None else ya + term
    ya = a_b * ya
    yb = None
    for k in range(kb):
        term = wb_ref[k:k + 1, :] * gpad[hb - (kb - 1) + k:hb - (kb - 1) + k + tt, :]
        yb = term if yb is None else yb + term
    yb = _silu(_layernorm(yb + bb_ref[...], lng_ref[...], lnb_ref[...]))

    ycat = jnp.concatenate([ya, yb], axis=-1).astype(BF16)
    y = jnp.dot(ycat, wout_ref[...], preferred_element_type=F32)
    x1_ref[...] = x_ref[...] + gm_ref[...] * y

    @pl.when(j == n_t - 1)
    def _():
        na_ref[...] = apad[SUBLANES + tt - (ka - 1):SUBLANES + tt, :]
        nb_ref[...] = gpad[hb + tt - (kb - 1):hb + tt, :]


def _conv_seq(grp, u, x, mod3, layer, wa, wb, bb, lng, lnb, wout_bf):
    tt = grp.tile
    n_b, n_t = grp.grid
    d = x.shape[1]
    ka, da = wa.shape
    kb = wb.shape[0]
    hb = 32
    assert kb - 1 <= hb and ka - 1 <= SUBLANES and tt % hb == 0
    per = tt // hb
    halo = pl.BlockSpec((hb, u.shape[1]), lambda b, j: (jnp.maximum((b * n_t + j) * per - 1, 0), 0))
    return pl.pallas_call(
        functools.partial(_conv_seq_kernel, n_t, ka, kb),
        out_shape=(jax.ShapeDtypeStruct((grp.rows, d), F32),
                   jax.ShapeDtypeStruct((n_b, ka - 1, da), F32),
                   jax.ShapeDtypeStruct((n_b, kb - 1, da), F32)),
        grid=grp.grid,
        in_specs=[grp.row_spec(u.shape[1]), halo, grp.row_spec(d), grp.mod_spec(layer, 2, d),
                  _small(wa.shape), _small(wb.shape), _small((1, da)), _small((1, da)), _small((1, da)),
                  _resident(wout_bf.shape)],
        out_specs=(grp.row_spec(d),
                   pl.BlockSpec((None, ka - 1, da), lambda b, j: (b, 0, 0)),
                   pl.BlockSpec((None, kb - 1, da), lambda b, j: (b, 0, 0))),
        scratch_shapes=[pltpu.VMEM((SUBLANES + tt, da), F32), pltpu.VMEM((hb + tt, da), F32)],
        compiler_params=_params(("arbitrary", "arbitrary")),
        name="conv_seq",
    )(u, u, x, grp.mod_arg(mod3), wa, wb, bb, lng, lnb, wout_bf)


def _conv_step_kernel(ka, kb, u_ref, sa_ref, sb_ref, x_ref, gm_ref, wa_ref, wb_ref, bb_ref,
                      lng_ref, lnb_ref, wout_ref, x1_ref, na_ref, nb_ref):
    da = wa_ref.shape[1]
    u = u_ref[...]
    a_b, a_c, a_x = u[:, 0:da], u[:, da:2 * da], u[:, 2 * da:3 * da]
    b_v, b_g = u[:, 3 * da:4 * da], u[:, 4 * da:5 * da]

    cur = a_c * a_x
    ya = wa_ref[ka - 1:ka, :] * cur
    for k in range(ka - 1):
        ya = ya + wa_ref[k:k + 1, :] * sa_ref[:, k * da:(k + 1) * da]
    ya = a_b * ya
    glu = b_v * jax.nn.sigmoid(b_g)
    yb = wb_ref[kb - 1:kb, :] * glu
    for k in range(kb - 1):
        yb = yb + wb_ref[k:k + 1, :] * sb_ref[:, k * da:(k + 1) * da]
    yb = _silu(_layernorm(yb + bb_ref[...], lng_ref[...], lnb_ref[...]))

    ycat = jnp.concatenate([ya, yb], axis=-1).astype(BF16)
    y = jnp.dot(ycat, wout_ref[...], preferred_element_type=F32)
    x1_ref[...] = x_ref[...] + gm_ref[...] * y

    if ka > 2:
        na_ref[:, 0:(ka - 2) * da] = sa_ref[:, da:(ka - 1) * da]
    na_ref[:, (ka - 2) * da:(ka - 1) * da] = cur
    if kb > 2:
        nb_ref[:, 0:(kb - 2) * da] = sb_ref[:, da:(kb - 1) * da]
    nb_ref[:, (kb - 2) * da:(kb - 1) * da] = glu


def _conv_step(grp, u, sa2, sb2, x, mod3, layer, wa, wb, bb, lng, lnb, wout_bf):
    n = grp.rows
    d = x.shape[1]
    ka, da = wa.shape
    kb = wb.shape[0]
    return pl.pallas_call(
        functools.partial(_conv_step_kernel, ka, kb),
        out_shape=(jax.ShapeDtypeStruct((n, d), F32),
                   jax.ShapeDtypeStruct(sa2.shape, F32),
                   jax.ShapeDtypeStruct(sb2.shape, F32)),
        grid=grp.grid,
        in_specs=[_small(u.shape), _small(sa2.shape), _small(sb2.shape), _small(x.shape),
                  grp.mod_spec(layer, 2, d),
                  _small(wa.shape), _small(wb.shape), _small((1, da)), _small((1, da)), _small((1, da)),
                  _small(wout_bf.shape)],
        out_specs=(_small((n, d)), _small(sa2.shape), _small(sb2.shape)),
        compiler_params=_params(("arbitrary", "arbitrary")),
        name="conv_step",
    )(u, sa2, sb2, x, grp.mod_arg(mod3), wa, wb, bb, lng, lnb, wout_bf)


def _ffn_kernel(x_ref, g_ref, sc_ref, sh_ref, gate_ref, wg_ref, wu_ref, wd_ref, o_ref):
    x = x_ref[...]
    h = _mod_rmsnorm(x, g_ref[...], sc_ref[...], sh_ref[...]).astype(BF16)
    a = jnp.dot(h, wg_ref[...], preferred_element_type=F32)
    b = jnp.dot(h, wu_ref[...], preferred_element_type=F32)
    act = (_silu(a) * b).astype(BF16)
    f = jnp.dot(act, wd_ref[...], preferred_element_type=F32)
    o_ref[...] = x + gate_ref[...] * f


def _ffn(grp, x, mod3, layer, g, wg_bf, wu_bf, wd_bf):
    d = x.shape[1]
    return pl.pallas_call(
        _ffn_kernel,
        out_shape=jax.ShapeDtypeStruct(x.shape, F32),
        grid=grp.grid,
        in_specs=[grp.row_spec(d), _small((1, d)),
                  grp.mod_spec(layer, 4, d), grp.mod_spec(layer, 3, d), grp.mod_spec(layer, 5, d),
                  _resident(wg_bf.shape), _resident(wu_bf.shape), _resident(wd_bf.shape)],
        out_specs=grp.row_spec(d),
        compiler_params=_params(("arbitrary", "arbitrary")),
        name="ffn",
    )(x, g, grp.mod_arg(mod3), grp.mod_arg(mod3), grp.mod_arg(mod3), wg_bf, wu_bf, wd_bf)


def _pool_project(diff, wp_ref):
    n_g, pg, _ = wp_ref.shape
    outs = [jnp.dot(diff[:, gi * pg:(gi + 1) * pg].astype(BF16), wp_ref[gi], preferred_element_type=F32)
            for gi in range(n_g)]
    return jnp.concatenate(outs, axis=-1)


def _route(h3, wrh_ref, wrl_ref, br_ref, tri_ref, base_ref, mi_ref, mf_ref, cnt_ref):
    tt = h3.shape[0]
    h_hi = h3.astype(BF16)
    h_lo = (h3 - h_hi.astype(F32)).astype(BF16)
    logits = (jnp.dot(h_hi, wrh_ref[...], preferred_element_type=F32)
              + jnp.dot(h_lo, wrh_ref[...], preferred_element_type=F32)
              + jnp.dot(h_hi, wrl_ref[...], preferred_element_type=F32)
              + br_ref[...])
    lane = lax.broadcasted_iota(jnp.int32, (tt, LANES), 1).astype(F32)
    m0 = jnp.max(logits, axis=-1, keepdims=True)
    e0 = jnp.min(jnp.where(logits == m0, lane, float(LANES)), axis=-1, keepdims=True)
    rest = jnp.where(lane == e0, NEG_BIG * 2.0, logits)
    m1 = jnp.max(rest, axis=-1, keepdims=True)
    e1 = jnp.min(jnp.where(rest == m1, lane, float(LANES)), axis=-1, keepdims=True)
    dd = jnp.exp(m1 - m0)
    p0 = 1.0 / (1.0 + dd)
    p1 = dd * p0

    oh0 = lane == e0
    oh1 = lane == e1
    c = jnp.where(oh0 | oh1, 1.0, 0.0)
    prefix = jnp.dot(tri_ref[...], c.astype(BF16), preferred_element_type=F32)
    tot = base_ref[...] + prefix
    r0 = jnp.sum(jnp.where(oh0, tot, 0.0), axis=-1, keepdims=True)
    r1 = jnp.sum(jnp.where(oh1, tot, 0.0), axis=-1, keepdims=True)
    new_base = base_ref[...] + jnp.sum(c, axis=0, keepdims=True)
    base_ref[...] = new_base
    cnt_ref[...] = jnp.broadcast_to(new_base, cnt_ref.shape)

    meta = jnp.where(lane == 0.0, e0, jnp.where(lane == 1.0, e1,
                     jnp.where(lane == 2.0, r0, jnp.where(lane == 3.0, r1, 0.0))))
    mi_ref[...] = meta.astype(jnp.int32)
    mf_ref[...] = jnp.where(lane == 0.0, p0, jnp.where(lane == 1.0, p1, 0.0))


def _init_route_scratch(first, tri_ref, base_ref):
    @pl.when(first)
    def _():
        tt = tri_ref.shape[0]
        r = lax.broadcasted_iota(jnp.int32, (tt, tt), 0)
        c = lax.broadcasted_iota(jnp.int32, (tt, tt), 1)
        tri_ref[...] = jnp.where(c < r, 1.0, 0.0).astype(BF16)
        base_ref[...] = jnp.zeros(base_ref.shape, F32)


def _pool_seq_kernel(n_t, ctx, x_ref, xh_ref, gmix_ref, shm_ref, scm_ref, gm_ref, gffn_ref, shf_ref, scf_ref,
                     wp_ref, ps_ref, wrh_ref, wrl_ref, br_ref,
                     x3_ref, h3_ref, mi_ref, mf_ref, nc_ref, cnt_ref, hpad, tri_ref, base_ref):
    b = pl.program_id(0)
    j = pl.program_id(1)
    tt = x_ref.shape[0]
    hb = xh_ref.shape[0]
    n_g, pg, _ = wp_ref.shape
    _init_route_scratch((b == 0) & (j == 0), tri_ref, base_ref)

    keep = (j > 0).astype(F32)
    x = x_ref[...]
    h = _mod_rmsnorm(x, gmix_ref[...], scm_ref[...], shm_ref[...])
    hpad[0:hb, :] = keep * _mod_rmsnorm(xh_ref[...], gmix_ref[...], scm_ref[...], shm_ref[...])
    hpad[hb:hb + tt, :] = h

    pos = lax.broadcasted_iota(jnp.int32, (tt, pg), 0) + j * tt
    groups = []
    for gi, win in enumerate(POOL_WINDOWS):
        sl = slice(gi * pg, (gi + 1) * pg)
        s = h[:, sl]
        for i in range(1, win):
            s = s + hpad[hb - i:hb - i + tt, sl]
        cnt = jnp.minimum(pos + 1, win).astype(F32)
        groups.append(s / cnt - h[:, sl])
    diff = jnp.concatenate(groups, axis=-1)
    y = _pool_project(diff, wp_ref) * ps_ref[...]
    x3 = x + gm_ref[...] * y
    x3_ref[...] = x3
    h3 = _mod_rmsnorm(x3, gffn_ref[...], scf_ref[...], shf_ref[...])
    h3_ref[...] = h3
    _route(h3, wrh_ref, wrl_ref, br_ref, tri_ref, base_ref, mi_ref, mf_ref, cnt_ref)

    @pl.when(j == n_t - 1)
    def _():
        nc_ref[...] = hpad[hb + tt - ctx:hb + tt, :]


def _route_out_shapes(rows, d):
    return (jax.ShapeDtypeStruct((rows, d), F32), jax.ShapeDtypeStruct((rows, d), F32),
            jax.ShapeDtypeStruct((rows, LANES), jnp.int32), jax.ShapeDtypeStruct((rows, LANES), F32))


def _pool_seq(grp, x, mod3, layer, gmix, gffn, wp_bf, ps, wrh, wrl, br, ctx):
    tt = grp.tile
    n_b, n_t = grp.grid
    d = x.shape[1]
    hb = 16
    assert ctx <= hb and max(POOL_WINDOWS) - 1 <= hb and tt % hb == 0
    per = tt // hb
    halo = pl.BlockSpec((hb, d), lambda b, j: (jnp.maximum((b * n_t + j) * per - 1, 0), 0))
    ms = lambda c: grp.mod_spec(layer, c, d)
    return pl.pallas_call(
        functools.partial(_pool_seq_kernel, n_t, ctx),
        out_shape=_route_out_shapes(grp.rows, d) + (
            jax.ShapeDtypeStruct((n_b, ctx, d), F32), jax.ShapeDtypeStruct((SUBLANES, LANES), F32)),
        grid=grp.grid,
        in_specs=[grp.row_spec(d), halo, _small((1, d)), ms(0), ms(1), ms(2), _small((1, d)), ms(3), ms(4),
                  _small(wp_bf.shape), _small((1, d)), _small(wrh.shape), _small(wrl.shape), _small(br.shape)],
        out_specs=(grp.row_spec(d), grp.row_spec(d), grp.row_spec(LANES), grp.row_spec(LANES),
                   pl.BlockSpec((None, ctx, d), lambda b, j: (b, 0, 0)), _small((SUBLANES, LANES))),
        scratch_shapes=[pltpu.VMEM((hb + tt, d), F32), pltpu.VMEM((tt, tt), BF16), pltpu.VMEM((1, LANES), F32)],
        compiler_params=_params(("arbitrary", "arbitrary")),
        name="pool_seq",
    )(x, x, gmix, *([grp.mod_arg(mod3)] * 3), gffn, *([grp.mod_arg(mod3)] * 2), wp_bf, ps, wrh, wrl, br)


def _pool_step_kernel(ctx, x_ref, sc_ref_state, gmix_ref, shm_ref, scm_ref, gm_ref, gffn_ref, shf_ref, scf_ref,
                      wp_ref, ps_ref, wrh_ref, wrl_ref, br_ref,
                      x3_ref, h3_ref, mi_ref, mf_ref, nc_ref, cnt_ref, tri_ref, base_ref):
    d = x_ref.shape[1]
    n_g, pg, _ = wp_ref.shape
    _init_route_scratch(pl.program_id(0) == 0, tri_ref, base_ref)

    x = x_ref[...]
    h = _mod_rmsnorm(x, gmix_ref[...], scm_ref[...], shm_ref[...])
    groups = []
    for gi, win in enumerate(POOL_WINDOWS):
        s = h[:, gi * pg:(gi + 1) * pg]
        for i in range(1, win):
            row = ctx - i
            s = s + sc_ref_state[:, row * d + gi * pg:row * d + (gi + 1) * pg]
        cnt = float(min(PAST_LEN + 1, win))
        groups.append(s / cnt - h[:, gi * pg:(gi + 1) * pg])
    diff = jnp.concatenate(groups, axis=-1)
    y = _pool_project(diff, wp_ref) * ps_ref[...]
    x3 = x + gm_ref[...] * y
    x3_ref[...] = x3
    h3 = _mod_rmsnorm(x3, gffn_ref[...], scf_ref[...], shf_ref[...])
    h3_ref[...] = h3
    _route(h3, wrh_ref, wrl_ref, br_ref, tri_ref, base_ref, mi_ref, mf_ref, cnt_ref)

    if ctx > 1:
        nc_ref[:, 0:(ctx - 1) * d] = sc_ref_state[:, d:ctx * d]
    nc_ref[:, (ctx - 1) * d:ctx * d] = h


def _pool_step(grp, x, sc2, mod3, layer, gmix, gffn, wp_bf, ps, wrh, wrl, br, ctx):
    n, d = x.shape
    ms = lambda c: grp.mod_spec(layer, c, d)
    return pl.pallas_call(
        functools.partial(_pool_step_kernel, ctx),
        out_shape=_route_out_shapes(n, d) + (
            jax.ShapeDtypeStruct(sc2.shape, F32), jax.ShapeDtypeStruct((SUBLANES, LANES), F32)),
        grid=grp.grid,
        in_specs=[_small(x.shape), _small(sc2.shape), _small((1, d)), ms(0), ms(1), ms(2), _small((1, d)), ms(3), ms(4),
                  _small(wp_bf.shape), _small((1, d)), _small(wrh.shape), _small(wrl.shape), _small(br.shape)],
        out_specs=(_small((n, d)), _small((n, d)), _small((n, LANES)), _small((n, LANES)),
                   _small(sc2.shape), _small((SUBLANES, LANES))),
        scratch_shapes=[pltpu.VMEM((n, n), BF16), pltpu.VMEM((1, LANES), F32)],
        compiler_params=_params(("arbitrary", "arbitrary")),
        name="pool_step",
    )(x, sc2, gmix, *([grp.mod_arg(mod3)] * 3), gffn, *([grp.mod_arg(mod3)] * 2), wp_bf, ps, wrh, wrl, br)


def _invert_kernel(pos_ref, code_ref):
    def body(i, carry):
        code_ref[pos_ref[i]] = i
        return carry

    lax.fori_loop(0, pos_ref.shape[0], body, 0, unroll=8)


def _invert(pos_flat):
    smem = pl.BlockSpec(memory_space=pltpu.SMEM)
    return pl.pallas_call(
        _invert_kernel,
        out_shape=jax.ShapeDtypeStruct(pos_flat.shape, jnp.int32),
        in_specs=[smem], out_specs=smem,
        name="invert",
    )(pos_flat)


FLAG_FIRST, FLAG_LAST, FLAG_FINAL = 1, 2, 4


def _moe_kernel(sub, n_fc, we_ref, wt_ref, wlo_ref, whi_ref, wflag_ref,
                code_prev_ref, code_ref, code_next_ref, h_hbm, wg_ref, wu_ref, wd_ref, yk_hbm,
                xs_buf, acc, xb, wgb, wub, wdb, gsem, ssem):
    w = pl.program_id(0)
    j = pl.program_id(1)
    tm = xb.shape[0]
    t = wt_ref[w]
    slot = lax.rem(t, 2)
    other = 1 - slot
    lo = wlo_ref[w]
    hi = whi_ref[w]
    flag = wflag_ref[w]
    first = (flag & FLAG_FIRST) != 0
    last = (flag & FLAG_LAST) != 0
    final = (flag & FLAG_FINAL) != 0
    share = (tm // n_fc) // SUBLANES * SUBLANES
    rest = tm - share * n_fc

    def gather_row(codes, s, r):
        tok = lax.shift_right_logical(codes[0, r], 1)
        return pltpu.make_async_copy(h_hbm.at[pl.ds(tok, 1), :], xs_buf.at[s, pl.ds(r, 1), :], gsem.at[s])

    def scatter_row(codes, s, r):
        code = codes[0, r]
        dst = yk_hbm.at[code & 1, pl.ds(lax.shift_right_logical(code, 1), 1), :]
        return pltpu.make_async_copy(acc.at[s, pl.ds(r, 1), :], dst, ssem.at[s])

    def wait_gather(s):
        pltpu.make_async_copy(h_hbm.at[pl.ds(0, tm), :], xs_buf.at[s], gsem.at[s]).wait()

    def wait_scatter(s):
        pltpu.make_async_copy(acc.at[s], yk_hbm.at[0, pl.ds(0, tm), :], ssem.at[s]).wait()

    def rows_loop(r0, n, fn):
        def body(i, carry):
            fn(r0 + i)
            return carry

        lax.fori_loop(0, n, body, 0, unroll=SUBLANES)

    def neighbour_share():
        base = pl.multiple_of(j * share, SUBLANES)
        for u in range(share):
            gather_row(code_next_ref, other, base + u).start()
            scatter_row(code_prev_ref, other, base + u).start()

    @pl.when((w == 0) & (j == 0))
    def _():
        rows_loop(0, tm, lambda r: gather_row(code_ref, 0, r).start())
        acc[1] = jnp.zeros((tm, acc.shape[2]), F32)

    @pl.when(first & (j == 0))
    def _():
        wait_gather(slot)
        acc[slot] = jnp.zeros((tm, acc.shape[2]), F32)
        if rest:
            rows_loop(share * n_fc, rest, lambda r: gather_row(code_next_ref, other, r).start())
            rows_loop(share * n_fc, rest, lambda r: scatter_row(code_prev_ref, other, r).start())

    def swiglu(x, wg, wu, wd):
        a = jnp.dot(x, wg, preferred_element_type=F32)
        b = jnp.dot(x, wu, preferred_element_type=F32)
        return jnp.dot((_silu(a) * b).astype(BF16), wd, preferred_element_type=F32)

    whole = (lo == 0) & (hi == tm)

    @pl.when((hi > lo) & (j == 0))
    def _():
        rows = lax.broadcasted_iota(jnp.int32, xb.shape, 0)
        xb[...] = jnp.where((rows >= lo) & (rows < hi), xs_buf[slot], 0.0).astype(BF16)

    @pl.when(whole)
    def _():
        neighbour_share()
        acc[slot] += swiglu(xb[...], wg_ref[...].astype(BF16), wu_ref[...].astype(BF16),
                            wd_ref[...].astype(BF16))

    @pl.when((hi > lo) & jnp.logical_not(whole))
    def _():
        @pl.when(first)
        def _():
            neighbour_share()

        wgb[...] = wg_ref[...].astype(BF16)
        wub[...] = wu_ref[...].astype(BF16)
        wdb[...] = wd_ref[...].astype(BF16)
        for s in range(tm // sub):
            @pl.when((s * sub < hi) & ((s + 1) * sub > lo))
            def _():
                rs = pl.ds(s * sub, sub)
                acc[slot, rs, :] += swiglu(xb[rs, :], wgb[...], wub[...], wdb[...])

    @pl.when(last & (j == n_fc - 1))
    def _():
        wait_scatter(other)

        @pl.when(final)
        def _():
            rows_loop(0, tm, lambda r: scatter_row(code_ref, slot, r).start())
            wait_scatter(slot)
            wait_gather(other)


def _moe(h3, codes, work, wg, wu, wd, tm):
    m, d = h3.shape
    n_e, _, ff = wg.shape
    fc = min(MOE_FF, ff)
    assert ff % fc == 0
    n_fc = ff // fc
    sub = min(MOE_SUB, tm)
    n_work = work[0].shape[0]
    n_tiles = codes.shape[0]

    def jj(w, j, wlo, whi):
        return jnp.where(whi[w] > wlo[w], j, n_fc - 1)

    grid_spec = pltpu.PrefetchScalarGridSpec(
        num_scalar_prefetch=5,
        grid=(n_work, n_fc),
        in_specs=[pl.BlockSpec((None, 1, tm), lambda w, j, we, wt, wlo, whi, wf: (jnp.maximum(wt[w] - 1, 0), 0, 0),
                               memory_space=pltpu.SMEM),
                  pl.BlockSpec((None, 1, tm), lambda w, j, we, wt, wlo, whi, wf: (wt[w], 0, 0),
                               memory_space=pltpu.SMEM),
                  pl.BlockSpec((None, 1, tm),
                               lambda w, j, we, wt, wlo, whi, wf: (jnp.minimum(wt[w] + 1, n_tiles - 1), 0, 0),
                               memory_space=pltpu.SMEM),
                  pl.BlockSpec(memory_space=pl.ANY),
                  pl.BlockSpec((None, d, fc), lambda w, j, we, wt, wlo, whi, wf: (we[w], 0, jj(w, j, wlo, whi))),
                  pl.BlockSpec((None, d, fc), lambda w, j, we, wt, wlo, whi, wf: (we[w], 0, jj(w, j, wlo, whi))),
                  pl.BlockSpec((None, fc, d), lambda w, j, we, wt, wlo, whi, wf: (we[w], jj(w, j, wlo, whi), 0))],
        out_specs=pl.BlockSpec(memory_space=pl.ANY),
        scratch_shapes=[pltpu.VMEM((2, tm, d), F32), pltpu.VMEM((2, tm, d), F32), pltpu.VMEM((tm, d), BF16),
                        pltpu.VMEM((d, fc), BF16), pltpu.VMEM((d, fc), BF16), pltpu.VMEM((fc, d), BF16),
                        pltpu.SemaphoreType.DMA((2,)), pltpu.SemaphoreType.DMA((2,))])
    return pl.pallas_call(
        functools.partial(_moe_kernel, sub, n_fc),
        out_shape=jax.ShapeDtypeStruct((2, m, d), F32),
        grid_spec=grid_spec,
        compiler_params=_params(("arbitrary", "arbitrary")),
        name="moe",
    )(*work, codes, codes, codes, h3, wg, wu, wd)


def _moe_work_items(counts, tm, n_tiles):
    n_e = counts.shape[0]
    n_work = n_tiles + n_e - 1
    uend = jnp.cumsum(counts)
    ustart = uend - counts
    first_tile = ustart // tm
    n_w = jnp.where(counts > 0, (uend - 1) // tm - first_tile + 1, 0)
    wend = jnp.cumsum(n_w)
    wstart = wend - n_w
    total = wend[-1]
    idx = jnp.arange(n_work, dtype=jnp.int32)
    w = jnp.minimum(idx, total - 1)
    we = jnp.sum((w[:, None] >= wend[None, :]).astype(jnp.int32), axis=1)
    wt = first_tile[we] + (w - wstart[we])
    lo = jnp.clip(ustart[we] - wt * tm, 0, tm)
    hi = jnp.clip(uend[we] - wt * tm, 0, tm)
    live = idx < total
    hi = jnp.where(live, hi, lo)
    prev_t = jnp.concatenate([jnp.full((1,), -1, jnp.int32), wt[:-1]])
    next_t = jnp.concatenate([wt[1:], jnp.full((1,), -1, jnp.int32)])
    final = idx == total - 1
    flags = (jnp.where(live & (wt != prev_t), FLAG_FIRST, 0)
             + jnp.where(live & ((wt != next_t) | final), FLAG_LAST, 0)
             + jnp.where(final, FLAG_FINAL, 0))
    as_i32 = lambda a: a.astype(jnp.int32)
    return ustart, (as_i32(we), as_i32(wt), as_i32(lo), as_i32(hi), as_i32(flags))


def _combine_kernel(final, y0_ref, y1_ref, mf_ref, x3_ref, gate_ref, gout_ref, o_ref):
    mf = mf_ref[...]
    moe = mf[:, 0:1] * y0_ref[...] + mf[:, 1:2] * y1_ref[...]
    x4 = x3_ref[...] + gate_ref[...] * moe
    o_ref[...] = _rmsnorm(x4, gout_ref[...]) if final else x4


def _combine(grp, yk, mf, x3, mod3, layer, gout, final):
    m, d = x3.shape
    nt = grp.grid[1]
    choice = lambda k: pl.BlockSpec((None, grp.tile, d), lambda b, j: (k, b * nt + j, 0))
    return pl.pallas_call(
        functools.partial(_combine_kernel, final),
        out_shape=jax.ShapeDtypeStruct((m, d), F32),
        grid=grp.grid,
        in_specs=[choice(0), choice(1), grp.row_spec(LANES), grp.row_spec(d), grp.mod_spec(layer, 5, d),
                  _small((1, d))],
        out_specs=grp.row_spec(d),
        compiler_params=_params(("arbitrary", "arbitrary")),
        name="combine",
    )(yk, yk, mf, x3, grp.mod_arg(mod3), gout)


def _moe_layer(grp, x3, h3, mi, mf, cnt, mod3, layer, wg, wu, wd, gout, final):
    m, d = h3.shape
    n_e = wg.shape[0]
    tm = min(MOE_TILE, m)
    n_slots = 2 * m
    assert n_slots % tm == 0
    n_tiles = n_slots // tm
    counts = cnt[0, :n_e].astype(jnp.int32)
    ustart, work = _moe_work_items(counts, tm, n_tiles)

    e = mi[:, 0:2]
    pos = mi[:, 2:4]
    for k in range(n_e):
        pos = pos + jnp.where(e == k, ustart[k], 0)
    codes = _invert(pos.reshape(n_slots)).reshape(n_tiles, 1, tm)

    yk = _moe(h3, codes, work, wg, wu, wd, tm)
    return _combine(grp, yk, mf, x3, mod3, layer, gout, final)


def _trunk(grp, x, mod3, st_a, st_b, st_c, p):
    depth = p['w_ada'].shape[0]
    d = x.shape[1]
    new_a, new_b, new_c = [], [], []
    for i in range(depth):
        j = i // 2
        gmix = p['norm_mix'][i].reshape(1, d)
        gffn = p['norm_ffn'][i].reshape(1, d)
        if i % 2 == 0:
            u = _in_proj(grp, x, mod3, i, gmix, p['w_in'][j])
            conv_w = (p['w_conv_a'][j], p['w_conv_b'][j], p['b_conv_b'][j].reshape(1, -1),
                      p['ln_b_g'][j].reshape(1, -1), p['ln_b_b'][j].reshape(1, -1), p['w_out'][j])
            if grp.per_row_mod:
                x, na, nb = _conv_step(grp, u, st_a[j], st_b[j], x, mod3, i, *conv_w)
            else:
                x, na, nb = _conv_seq(grp, u, x, mod3, i, *conv_w)
            new_a.append(na)
            new_b.append(nb)
            x = _ffn(grp, x, mod3, i, gffn, p['w_ffn_gate'][j], p['w_ffn_up'][j], p['w_ffn_down'][j])
        else:
            ctx = p['pool_ctx']
            pool_w = (p['w_pool'][j], p['pool_scale'][j].reshape(1, d), p['wr_hi'][j], p['wr_lo'][j], p['br'][j])
            if grp.per_row_mod:
                x3, h3, mi, mf, nc, cnt = _pool_step(grp, x, st_c[j], mod3, i, gmix, gffn, *pool_w, ctx)
            else:
                x3, h3, mi, mf, nc, cnt = _pool_seq(grp, x, mod3, i, gmix, gffn, *pool_w, ctx)
            new_c.append(nc)
            final = i == depth - 1
            x = _moe_layer(grp, x3, h3, mi, mf, cnt, mod3, i, p['w_exp_gate'][j], p['w_exp_up'][j],
                           p['w_exp_down'][j], p['norm_out'].reshape(1, d), final)
    return x, new_a, new_b, new_c


def kernel(x_prompt, x_sample, state_a, state_b, state_c, c_prompt, c_sample, w_ada, b_ada, norm_mix, norm_ffn, norm_out, w_in, w_conv_a, w_conv_b, b_conv_b, ln_b_g, ln_b_b, w_out, w_ffn_gate, w_ffn_up, w_ffn_down, w_pool, pool_scale, w_router, b_router, w_exp_gate, w_exp_up, w_exp_down):
    n_p, seq, d = x_prompt.shape
    n_s, dec_seq, _ = x_sample.shape
    depth = w_ada.shape[0]
    n_e = w_router.shape[2]
    assert dec_seq == 1 and depth % 2 == 0 and n_s % SUBLANES == 0 and n_e <= LANES
    n_even, n_odd = state_a.shape[0], state_c.shape[0]
    ctx = state_c.shape[2]

    wr = jnp.pad(w_router, ((0, 0), (0, 0), (0, LANES - n_e)))
    wr_hi = wr.astype(BF16)
    wr_lo = (wr - wr_hi.astype(F32)).astype(BF16)
    br = jnp.pad(b_router, ((0, 0), (0, LANES - n_e)), constant_values=NEG_BIG).reshape(n_odd, 1, LANES)

    p = {
        'w_ada': w_ada, 'norm_mix': norm_mix, 'norm_ffn': norm_ffn, 'norm_out': norm_out,
        'w_in': w_in.astype(BF16), 'w_conv_a': w_conv_a, 'w_conv_b': w_conv_b, 'b_conv_b': b_conv_b,
        'ln_b_g': ln_b_g, 'ln_b_b': ln_b_b, 'w_out': w_out.astype(BF16),
        'w_ffn_gate': w_ffn_gate.astype(BF16), 'w_ffn_up': w_ffn_up.astype(BF16),
        'w_ffn_down': w_ffn_down.astype(BF16),
        'w_pool': w_pool.astype(BF16), 'pool_scale': pool_scale, 'wr_hi': wr_hi, 'wr_lo': wr_lo, 'br': br,
        'w_exp_gate': w_exp_gate, 'w_exp_up': w_exp_up, 'w_exp_down': w_exp_down, 'pool_ctx': ctx,
    }

    mod3 = _ada(jnp.concatenate([c_sample, c_prompt], axis=0), w_ada, b_ada)

    g_prompt = _Group(n_p, seq, False, n_s, n_s)
    g_sample = _Group(n_s, 1, True, 0, n_s)

    y_p, pa, pb, pc = _trunk(g_prompt, x_prompt.reshape(n_p * seq, d), mod3, None, None, None, p)
    y_s, sa, sb, sc = _trunk(
        g_sample, x_sample.reshape(n_s, d), mod3,
        state_a.reshape(n_even, n_s, -1), state_b.reshape(n_even, n_s, -1), state_c.reshape(n_odd, n_s, -1), p)

    sa = [a.reshape(n_s, state_a.shape[2], -1) for a in sa]
    sb = [b.reshape(n_s, state_b.shape[2], -1) for b in sb]
    sc = [c.reshape(n_s, ctx, d) for c in sc]
    return (y_p.reshape(n_p, seq, d), y_s.reshape(n_s, 1, d),
            jnp.stack(pa), jnp.stack(sa), jnp.stack(pb), jnp.stack(sb), jnp.stack(pc), jnp.stack(sc))
```

```python
import functools

import jax
import jax.numpy as jnp
from jax import lax
from jax.experimental import pallas as pl
from jax.experimental.pallas import tpu as pltpu

EPS = 1e-6
PAST_LEN = 16384
POOL_WINDOWS = (2, 4, 8, 16)
N_MOD = 6
LANES = 128
SUBLANES = 8
VMEM_LIMIT = 56 * 1024 * 1024
NEG_BIG = -1e30

ROW_TILE = 512
MOE_TILE = 1024
MOE_SUB = 256
MOE_FF = 512

F32 = jnp.float32
BF16 = jnp.bfloat16


def _params(sem):
    return pltpu.CompilerParams(dimension_semantics=sem, vmem_limit_bytes=VMEM_LIMIT)


def _silu(x):
    return x * jax.nn.sigmoid(x)


def _rmsnorm(x, g):
    return x * lax.rsqrt(jnp.mean(x * x, axis=-1, keepdims=True) + EPS) * g


def _mod_rmsnorm(x, g, sc, sh):
    return _rmsnorm(x, g) * (1.0 + sc) + sh


def _resident(shape):
    nd = len(shape)
    return pl.BlockSpec(shape, lambda *_: (0,) * nd, pipeline_mode=pl.Buffered(1))


def _small(shape):
    nd = len(shape)
    return pl.BlockSpec(shape, lambda *_: (0,) * nd)


class _Group:
    def __init__(self, n_seq, seq_len, per_row_mod, mod_row0, n_sample):
        self.n_seq, self.seq_len = n_seq, seq_len
        self.per_row_mod = per_row_mod
        self.mod_row0 = mod_row0
        self.n_sample = n_sample
        if per_row_mod:
            self.tile = n_seq
            self.grid = (1, 1)
        else:
            self.tile = min(ROW_TILE, seq_len)
            assert seq_len % self.tile == 0 and self.tile % 32 == 0
            self.grid = (n_seq, seq_len // self.tile)
        self.rows = n_seq * seq_len

    def row_spec(self, width):
        nt = self.grid[1]
        return pl.BlockSpec((self.tile, width), lambda b, j, *_: (b * nt + j, 0))

    def mod_spec(self, layer, chunk, d):
        if self.per_row_mod:
            return pl.BlockSpec((None, self.n_sample, d), lambda b, j, *_: (layer, 0, chunk))
        r0 = self.mod_row0
        return pl.BlockSpec((None, None, 1, d), lambda b, j, *_: (layer, r0 + b, 0, chunk))

    def mod_arg(self, mod3):
        if self.per_row_mod:
            return mod3
        l, r, w = mod3.shape
        return mod3.reshape(l, r, 1, w)


def _ada_kernel(c_ref, w_ref, b_ref, o_ref):
    cs = _silu(c_ref[...]).astype(BF16)
    o_ref[...] = jnp.dot(cs, w_ref[...].astype(BF16), preferred_element_type=F32) + b_ref[...]


def _ada(c_all, w_ada, b_ada):
    depth, d, w6 = w_ada.shape
    r = c_all.shape[0]
    tn = w6 // 4
    return pl.pallas_call(
        _ada_kernel,
        out_shape=jax.ShapeDtypeStruct((depth, r, w6), F32),
        grid=(depth, w6 // tn),
        in_specs=[pl.BlockSpec((r, d), lambda l, n: (0, 0)),
                  pl.BlockSpec((None, d, tn), lambda l, n: (l, 0, n)),
                  pl.BlockSpec((None, 1, tn), lambda l, n: (l, 0, n))],
        out_specs=pl.BlockSpec((None, r, tn), lambda l, n: (l, 0, n)),
        compiler_params=_params(("arbitrary", "arbitrary")),
        name="ada",
    )(c_all, w_ada, b_ada.reshape(depth, 1, w6))


def _in_proj_kernel(x_ref, g_ref, sc_ref, sh_ref, w_ref, u_ref):
    h = _mod_rmsnorm(x_ref[...], g_ref[...], sc_ref[...], sh_ref[...])
    u_ref[...] = jnp.dot(h.astype(BF16), w_ref[...], preferred_element_type=F32)


def _in_proj(grp, x, mod3, layer, g, w_bf):
    d, n = w_bf.shape
    return pl.pallas_call(
        _in_proj_kernel,
        out_shape=jax.ShapeDtypeStruct((grp.rows, n), F32),
        grid=grp.grid,
        in_specs=[grp.row_spec(d), _small((1, d)),
                  grp.mod_spec(layer, 1, d), grp.mod_spec(layer, 0, d),
                  _resident((d, n))],
        out_specs=grp.row_spec(n),
        compiler_params=_params(("arbitrary", "arbitrary")),
        name="in_proj",
    )(x, g, grp.mod_arg(mod3), grp.mod_arg(mod3), w_bf)


def _layernorm(y, g, b):
    mu = jnp.mean(y, axis=-1, keepdims=True)
    yc = y - mu
    var = jnp.mean(yc * yc, axis=-1, keepdims=True)
    return yc * lax.rsqrt(var + EPS) * g + b


def _causal_taps(pad_ref, w_ref, k_taps, halo, tt):
    off0 = halo - (k_taps - 1)
    y = None
    for c in range(SUBLANES):
        n = tt + (SUBLANES if c else 0)
        q = None
        for k in range(k_taps):
            if (off0 + k) % SUBLANES == c:
                a = off0 + k - c
                term = w_ref[k:k + 1, :] * pad_ref[a:a + n, :]
                q = term if q is None else q + term
        if q is not None:
            part = q[c:c + tt, :]
            y = part if y is None else y + part
    return y


def _conv_seq_kernel(n_t, ka, kb, u_ref, uh_ref, x_ref, gm_ref, wa_ref, wb_ref, bb_ref,
                     lng_ref, lnb_ref, wout_ref, x1_ref, na_ref, nb_ref, apad, gpad):
    j = pl.program_id(1)
    tt = u_ref.shape[0]
    da = wa_ref.shape[1]
    hb = uh_ref.shape[0]
    keep = (j > 0).astype(F32)

    u = u_ref[...]
    a_b, a_c, a_x = u[:, 0:da], u[:, da:2 * da], u[:, 2 * da:3 * da]
    b_v, b_g = u[:, 3 * da:4 * da], u[:, 4 * da:5 * da]
    uh = uh_ref[...]
    apad[0:SUBLANES, :] = keep * (uh[hb - SUBLANES:hb, da:2 * da] * uh[hb - SUBLANES:hb, 2 * da:3 * da])
    apad[SUBLANES:SUBLANES + tt, :] = a_c * a_x
    gpad[0:hb, :] = keep * (uh[:, 3 * da:4 * da] * jax.nn.sigmoid(uh[:, 4 * da:5 * da]))
    gpad[hb:hb + tt, :] = b_v * jax.nn.sigmoid(b_g)

    zero_tail = jnp.zeros((SUBLANES, da), F32)
    apad[SUBLANES + tt:SUBLANES + tt + SUBLANES, :] = zero_tail
    gpad[hb + tt:hb + tt + SUBLANES, :] = zero_tail

    ya = a_b * _causal_taps(apad, wa_ref, ka, SUBLANES, tt)
    yb = _causal_taps(gpad, wb_ref, kb, hb, tt)
    yb = _silu(_layernorm(yb + bb_ref[...], lng_ref[...], lnb_ref[...]))

    ycat = jnp.concatenate([ya, yb], axis=-1).astype(BF16)
    y = jnp.dot(ycat, wout_ref[...], preferred_element_type=F32)
    x1_ref[...] = x_ref[...] + gm_ref[...] * y

    @pl.when(j == n_t - 1)
    def _():
        na_ref[...] = apad[SUBLANES + tt - (ka - 1):SUBLANES + tt, :]
        nb_ref[...] = gpad[hb + tt - (kb - 1):hb + tt, :]


def _conv_seq(grp, u, x, mod3, layer, wa, wb, bb, lng, lnb, wout_bf):
    tt = grp.tile
    n_b, n_t = grp.grid
    d = x.shape[1]
    ka, da = wa.shape
    kb = wb.shape[0]
    hb = 32
    assert kb - 1 <= hb and ka - 1 <= SUBLANES and tt % hb == 0
    per = tt // hb
    halo = pl.BlockSpec((hb, u.shape[1]), lambda b, j: (jnp.maximum((b * n_t + j) * per - 1, 0), 0))
    return pl.pallas_call(
        functools.partial(_conv_seq_kernel, n_t, ka, kb),
        out_shape=(jax.ShapeDtypeStruct((grp.rows, d), F32),
                   jax.ShapeDtypeStruct((n_b, ka - 1, da), F32),
                   jax.ShapeDtypeStruct((n_b, kb - 1, da), F32)),
        grid=grp.grid,
        in_specs=[grp.row_spec(u.shape[1]), halo, grp.row_spec(d), grp.mod_spec(layer, 2, d),
                  _small(wa.shape), _small(wb.shape), _small((1, da)), _small((1, da)), _small((1, da)),
                  _resident(wout_bf.shape)],
        out_specs=(grp.row_spec(d),
                   pl.BlockSpec((None, ka - 1, da), lambda b, j: (b, 0, 0)),
                   pl.BlockSpec((None, kb - 1, da), lambda b, j: (b, 0, 0))),
        scratch_shapes=[pltpu.VMEM((2 * SUBLANES + tt, da), F32), pltpu.VMEM((hb + tt + SUBLANES, da), F32)],
        compiler_params=_params(("arbitrary", "arbitrary")),
        name="conv_seq",
    )(u, u, x, grp.mod_arg(mod3), wa, wb, bb, lng, lnb, wout_bf)


def _conv_step_kernel(ka, kb, u_ref, sa_ref, sb_ref, x_ref, gm_ref, wa_ref, wb_ref, bb_ref,
                      lng_ref, lnb_ref, wout_ref, x1_ref, na_ref, nb_ref):
    da = wa_ref.shape[1]
    u = u_ref[...]
    a_b, a_c, a_x = u[:, 0:da], u[:, da:2 * da], u[:, 2 * da:3 * da]
    b_v, b_g = u[:, 3 * da:4 * da], u[:, 4 * da:5 * da]

    cur = a_c * a_x
    ya = wa_ref[ka - 1:ka, :] * cur
    for k in range(ka - 1):
        ya = ya + wa_ref[k:k + 1, :] * sa_ref[:, k * da:(k + 1) * da]
    ya = a_b * ya
    glu = b_v * jax.nn.sigmoid(b_g)
    yb = wb_ref[kb - 1:kb, :] * glu
    for k in range(kb - 1):
        yb = yb + wb_ref[k:k + 1, :] * sb_ref[:, k * da:(k + 1) * da]
    yb = _silu(_layernorm(yb + bb_ref[...], lng_ref[...], lnb_ref[...]))

    ycat = jnp.concatenate([ya, yb], axis=-1).astype(BF16)
    y = jnp.dot(ycat, wout_ref[...], preferred_element_type=F32)
    x1_ref[...] = x_ref[...] + gm_ref[...] * y

    if ka > 2:
        na_ref[:, 0:(ka - 2) * da] = sa_ref[:, da:(ka - 1) * da]
    na_ref[:, (ka - 2) * da:(ka - 1) * da] = cur
    if kb > 2:
        nb_ref[:, 0:(kb - 2) * da] = sb_ref[:, da:(kb - 1) * da]
    nb_ref[:, (kb - 2) * da:(kb - 1) * da] = glu


def _conv_step(grp, u, sa2, sb2, x, mod3, layer, wa, wb, bb, lng, lnb, wout_bf):
    n = grp.rows
    d = x.shape[1]
    ka, da = wa.shape
    kb = wb.shape[0]
    return pl.pallas_call(
        functools.partial(_conv_step_kernel, ka, kb),
        out_shape=(jax.ShapeDtypeStruct((n, d), F32),
                   jax.ShapeDtypeStruct(sa2.shape, F32),
                   jax.ShapeDtypeStruct(sb2.shape, F32)),
        grid=grp.grid,
        in_specs=[_small(u.shape), _small(sa2.shape), _small(sb2.shape), _small(x.shape),
                  grp.mod_spec(layer, 2, d),
                  _small(wa.shape), _small(wb.shape), _small((1, da)), _small((1, da)), _small((1, da)),
                  _small(wout_bf.shape)],
        out_specs=(_small((n, d)), _small(sa2.shape), _small(sb2.shape)),
        compiler_params=_params(("arbitrary", "arbitrary")),
        name="conv_step",
    )(u, sa2, sb2, x, grp.mod_arg(mod3), wa, wb, bb, lng, lnb, wout_bf)


def _ffn_kernel(x_ref, g_ref, sc_ref, sh_ref, gate_ref, wg_ref, wu_ref, wd_ref, o_ref):
    x = x_ref[...]
    h = _mod_rmsnorm(x, g_ref[...], sc_ref[...], sh_ref[...]).astype(BF16)
    a = jnp.dot(h, wg_ref[...], preferred_element_type=F32)
    b = jnp.dot(h, wu_ref[...], preferred_element_type=F32)
    act = (_silu(a) * b).astype(BF16)
    f = jnp.dot(act, wd_ref[...], preferred_element_type=F32)
    o_ref[...] = x + gate_ref[...] * f


def _ffn(grp, x, mod3, layer, g, wg_bf, wu_bf, wd_bf):
    d = x.shape[1]
    return pl.pallas_call(
        _ffn_kernel,
        out_shape=jax.ShapeDtypeStruct(x.shape, F32),
        grid=grp.grid,
        in_specs=[grp.row_spec(d), _small((1, d)),
                  grp.mod_spec(layer, 4, d), grp.mod_spec(layer, 3, d), grp.mod_spec(layer, 5, d),
                  _resident(wg_bf.shape), _resident(wu_bf.shape), _resident(wd_bf.shape)],
        out_specs=grp.row_spec(d),
        compiler_params=_params(("arbitrary", "arbitrary")),
        name="ffn",
    )(x, g, grp.mod_arg(mod3), grp.mod_arg(mod3), grp.mod_arg(mod3), wg_bf, wu_bf, wd_bf)


def _window_sum(v, win):
    assert win & (win - 1) == 0
    span = 1
    while span < win:
        v = v + pltpu.roll(v, span, 0)
        span *= 2
    return v


def _pool_project(diff, wp_ref):
    n_g, pg, _ = wp_ref.shape
    outs = [jnp.dot(diff[:, gi * pg:(gi + 1) * pg].astype(BF16), wp_ref[gi], preferred_element_type=F32)
            for gi in range(n_g)]
    return jnp.concatenate(outs, axis=-1)


def _route(h3, wrh_ref, wrl_ref, br_ref, tri_ref, base_ref, mi_ref, mf_ref, cnt_ref):
    tt = h3.shape[0]
    h_hi = h3.astype(BF16)
    h_lo = (h3 - h_hi.astype(F32)).astype(BF16)
    logits = (jnp.dot(h_hi, wrh_ref[...], preferred_element_type=F32)
              + jnp.dot(h_lo, wrh_ref[...], preferred_element_type=F32)
              + jnp.dot(h_hi, wrl_ref[...], preferred_element_type=F32)
              + br_ref[...])
    lane = lax.broadcasted_iota(jnp.int32, (tt, LANES), 1).astype(F32)
    m0 = jnp.max(logits, axis=-1, keepdims=True)
    e0 = jnp.min(jnp.where(logits == m0, lane, float(LANES)), axis=-1, keepdims=True)
    rest = jnp.where(lane == e0, NEG_BIG * 2.0, logits)
    m1 = jnp.max(rest, axis=-1, keepdims=True)
    e1 = jnp.min(jnp.where(rest == m1, lane, float(LANES)), axis=-1, keepdims=True)
    dd = jnp.exp(m1 - m0)
    p0 = 1.0 / (1.0 + dd)
    p1 = dd * p0

    oh0 = lane == e0
    oh1 = lane == e1
    c = jnp.where(oh0 | oh1, 1.0, 0.0)
    prefix = jnp.dot(tri_ref[...], c.astype(BF16), preferred_element_type=F32)
    tot = base_ref[...] + prefix
    r0 = jnp.sum(jnp.where(oh0, tot, 0.0), axis=-1, keepdims=True)
    r1 = jnp.sum(jnp.where(oh1, tot, 0.0), axis=-1, keepdims=True)
    new_base = base_ref[...] + jnp.sum(c, axis=0, keepdims=True)
    base_ref[...] = new_base
    cnt_ref[...] = jnp.broadcast_to(new_base, cnt_ref.shape)

    meta = jnp.where(lane == 0.0, e0, jnp.where(lane == 1.0, e1,
                     jnp.where(lane == 2.0, r0, jnp.where(lane == 3.0, r1, 0.0))))
    mi_ref[...] = meta.T[0:SUBLANES, :].astype(jnp.int32)
    mf_ref[...] = jnp.where(lane == 0.0, p0, jnp.where(lane == 1.0, p1, 0.0))


def _init_route_scratch(first, tri_ref, base_ref):
    @pl.when(first)
    def _():
        tt = tri_ref.shape[0]
        r = lax.broadcasted_iota(jnp.int32, (tt, tt), 0)
        c = lax.broadcasted_iota(jnp.int32, (tt, tt), 1)
        tri_ref[...] = jnp.where(c < r, 1.0, 0.0).astype(BF16)
        base_ref[...] = jnp.zeros(base_ref.shape, F32)


def _pool_seq_kernel(n_t, ctx, x_ref, xh_ref, gmix_ref, shm_ref, scm_ref, gm_ref, gffn_ref, shf_ref, scf_ref,
                     wp_ref, ps_ref, wrh_ref, wrl_ref, br_ref,
                     x3_ref, h3_ref, mi_ref, mf_ref, nc_ref, cnt_ref, hpad, tri_ref, base_ref):
    b = pl.program_id(0)
    j = pl.program_id(1)
    tt = x_ref.shape[0]
    hb = xh_ref.shape[0]
    n_g, pg, _ = wp_ref.shape
    _init_route_scratch((b == 0) & (j == 0), tri_ref, base_ref)

    keep = (j > 0).astype(F32)
    x = x_ref[...]
    h = _mod_rmsnorm(x, gmix_ref[...], scm_ref[...], shm_ref[...])
    hpad[0:hb, :] = keep * _mod_rmsnorm(xh_ref[...], gmix_ref[...], scm_ref[...], shm_ref[...])
    hpad[hb:hb + tt, :] = h

    pos = lax.broadcasted_iota(jnp.int32, (tt, pg), 0) + j * tt
    groups = []
    for gi, win in enumerate(POOL_WINDOWS):
        sl = slice(gi * pg, (gi + 1) * pg)
        s = _window_sum(hpad[:, sl], win)[hb:hb + tt, :]
        cnt = jnp.minimum(pos + 1, win).astype(F32)
        groups.append(s / cnt - h[:, sl])
    diff = jnp.concatenate(groups, axis=-1)
    y = _pool_project(diff, wp_ref) * ps_ref[...]
    x3 = x + gm_ref[...] * y
    x3_ref[...] = x3
    h3 = _mod_rmsnorm(x3, gffn_ref[...], scf_ref[...], shf_ref[...])
    h3_ref[...] = h3
    _route(h3, wrh_ref, wrl_ref, br_ref, tri_ref, base_ref, mi_ref, mf_ref, cnt_ref)

    @pl.when(j == n_t - 1)
    def _():
        nc_ref[...] = hpad[hb + tt - ctx:hb + tt, :]


def _route_out_shapes(rows, d):
    return (jax.ShapeDtypeStruct((rows, d), F32), jax.ShapeDtypeStruct((rows, d), F32),
            jax.ShapeDtypeStruct((SUBLANES, rows), jnp.int32), jax.ShapeDtypeStruct((rows, LANES), F32))


def _pool_seq(grp, x, mod3, layer, gmix, gffn, wp_bf, ps, wrh, wrl, br, ctx):
    tt = grp.tile
    n_b, n_t = grp.grid
    d = x.shape[1]
    hb = 16
    assert ctx <= hb and max(POOL_WINDOWS) - 1 <= hb and tt % hb == 0
    per = tt // hb
    halo = pl.BlockSpec((hb, d), lambda b, j: (jnp.maximum((b * n_t + j) * per - 1, 0), 0))
    ms = lambda c: grp.mod_spec(layer, c, d)
    return pl.pallas_call(
        functools.partial(_pool_seq_kernel, n_t, ctx),
        out_shape=_route_out_shapes(grp.rows, d) + (
            jax.ShapeDtypeStruct((n_b, ctx, d), F32), jax.ShapeDtypeStruct((SUBLANES, LANES), F32)),
        grid=grp.grid,
        in_specs=[grp.row_spec(d), halo, _small((1, d)), ms(0), ms(1), ms(2), _small((1, d)), ms(3), ms(4),
                  _small(wp_bf.shape), _small((1, d)), _small(wrh.shape), _small(wrl.shape), _small(br.shape)],
        out_specs=(grp.row_spec(d), grp.row_spec(d), pl.BlockSpec((SUBLANES, tt), lambda b, j: (0, b * n_t + j)),
                   grp.row_spec(LANES), pl.BlockSpec((None, ctx, d), lambda b, j: (b, 0, 0)), _small((SUBLANES, LANES))),
        scratch_shapes=[pltpu.VMEM((hb + tt, d), F32), pltpu.VMEM((tt, tt), BF16), pltpu.VMEM((1, LANES), F32)],
        compiler_params=_params(("arbitrary", "arbitrary")),
        name="pool_seq",
    )(x, x, gmix, *([grp.mod_arg(mod3)] * 3), gffn, *([grp.mod_arg(mod3)] * 2), wp_bf, ps, wrh, wrl, br)


def _pool_step_kernel(ctx, x_ref, sc_ref_state, gmix_ref, shm_ref, scm_ref, gm_ref, gffn_ref, shf_ref, scf_ref,
                      wp_ref, ps_ref, wrh_ref, wrl_ref, br_ref,
                      x3_ref, h3_ref, mi_ref, mf_ref, nc_ref, cnt_ref, tri_ref, base_ref):
    d = x_ref.shape[1]
    n_g, pg, _ = wp_ref.shape
    _init_route_scratch(pl.program_id(0) == 0, tri_ref, base_ref)

    x = x_ref[...]
    h = _mod_rmsnorm(x, gmix_ref[...], scm_ref[...], shm_ref[...])
    groups = []
    for gi, win in enumerate(POOL_WINDOWS):
        s = h[:, gi * pg:(gi + 1) * pg]
        for i in range(1, win):
            row = ctx - i
            s = s + sc_ref_state[:, row * d + gi * pg:row * d + (gi + 1) * pg]
        cnt = float(min(PAST_LEN + 1, win))
        groups.append(s / cnt - h[:, gi * pg:(gi + 1) * pg])
    diff = jnp.concatenate(groups, axis=-1)
    y = _pool_project(diff, wp_ref) * ps_ref[...]
    x3 = x + gm_ref[...] * y
    x3_ref[...] = x3
    h3 = _mod_rmsnorm(x3, gffn_ref[...], scf_ref[...], shf_ref[...])
    h3_ref[...] = h3
    _route(h3, wrh_ref, wrl_ref, br_ref, tri_ref, base_ref, mi_ref, mf_ref, cnt_ref)

    if ctx > 1:
        nc_ref[:, 0:(ctx - 1) * d] = sc_ref_state[:, d:ctx * d]
    nc_ref[:, (ctx - 1) * d:ctx * d] = h


def _pool_step(grp, x, sc2, mod3, layer, gmix, gffn, wp_bf, ps, wrh, wrl, br, ctx):
    n, d = x.shape
    ms = lambda c: grp.mod_spec(layer, c, d)
    return pl.pallas_call(
        functools.partial(_pool_step_kernel, ctx),
        out_shape=_route_out_shapes(n, d) + (
            jax.ShapeDtypeStruct(sc2.shape, F32), jax.ShapeDtypeStruct((SUBLANES, LANES), F32)),
        grid=grp.grid,
        in_specs=[_small(x.shape), _small(sc2.shape), _small((1, d)), ms(0), ms(1), ms(2), _small((1, d)), ms(3), ms(4),
                  _small(wp_bf.shape), _small((1, d)), _small(wrh.shape), _small(wrl.shape), _small(br.shape)],
        out_specs=(_small((n, d)), _small((n, d)), _small((SUBLANES, n)), _small((n, LANES)),
                   _small(sc2.shape), _small((SUBLANES, LANES))),
        scratch_shapes=[pltpu.VMEM((n, n), BF16), pltpu.VMEM((1, LANES), F32)],
        compiler_params=_params(("arbitrary", "arbitrary")),
        name="pool_step",
    )(x, sc2, gmix, *([grp.mod_arg(mod3)] * 3), gffn, *([grp.mod_arg(mod3)] * 2), wp_bf, ps, wrh, wrl, br)


def _invert_kernel(pos_ref, code_ref):
    m = pos_ref.shape[0] // 2

    def body(i, carry):
        code_ref[pos_ref[i]] = 2 * i
        code_ref[pos_ref[m + i]] = 2 * i + 1
        return carry

    lax.fori_loop(0, m, body, 0, unroll=8)


def _invert(pos_flat):
    smem = pl.BlockSpec(memory_space=pltpu.SMEM)
    return pl.pallas_call(
        _invert_kernel,
        out_shape=jax.ShapeDtypeStruct(pos_flat.shape, jnp.int32),
        in_specs=[smem], out_specs=smem,
        name="invert",
    )(pos_flat)


FLAG_FIRST, FLAG_LAST, FLAG_FINAL = 1, 2, 4


def _moe_kernel(sub, n_fc, we_ref, wt_ref, wlo_ref, whi_ref, wflag_ref,
                code_prev_ref, code_ref, code_next_ref, h_hbm, wg_ref, wu_ref, wd_ref, yk_hbm,
                xs_buf, acc, xb, wgb, wub, wdb, gsem, ssem):
    w = pl.program_id(0)
    j = pl.program_id(1)
    tm = xb.shape[0]
    t = wt_ref[w]
    slot = lax.rem(t, 2)
    other = 1 - slot
    lo = wlo_ref[w]
    hi = whi_ref[w]
    flag = wflag_ref[w]
    first = (flag & FLAG_FIRST) != 0
    last = (flag & FLAG_LAST) != 0
    final = (flag & FLAG_FINAL) != 0
    share = (tm // n_fc) // SUBLANES * SUBLANES
    rest = tm - share * n_fc

    def gather_row(codes, s, r):
        tok = lax.shift_right_logical(codes[0, r], 1)
        return pltpu.make_async_copy(h_hbm.at[pl.ds(tok, 1), :], xs_buf.at[s, pl.ds(r, 1), :], gsem.at[s])

    def scatter_row(codes, s, r):
        code = codes[0, r]
        dst = yk_hbm.at[code & 1, pl.ds(lax.shift_right_logical(code, 1), 1), :]
        return pltpu.make_async_copy(acc.at[s, pl.ds(r, 1), :], dst, ssem.at[s])

    def wait_gather(s):
        pltpu.make_async_copy(h_hbm.at[pl.ds(0, tm), :], xs_buf.at[s], gsem.at[s]).wait()

    def wait_scatter(s):
        pltpu.make_async_copy(acc.at[s], yk_hbm.at[0, pl.ds(0, tm), :], ssem.at[s]).wait()

    def rows_loop(r0, n, fn):
        def body(i, carry):
            fn(r0 + i)
            return carry

        lax.fori_loop(0, n, body, 0, unroll=SUBLANES)

    def neighbour_share():
        base = pl.multiple_of(j * share, SUBLANES)
        for u in range(share):
            gather_row(code_next_ref, other, base + u).start()
            scatter_row(code_prev_ref, other, base + u).start()

    @pl.when((w == 0) & (j == 0))
    def _():
        rows_loop(0, tm, lambda r: gather_row(code_ref, 0, r).start())
        acc[1] = jnp.zeros((tm, acc.shape[2]), F32)

    @pl.when(first & (j == 0))
    def _():
        wait_gather(slot)
        acc[slot] = jnp.zeros((tm, acc.shape[2]), F32)
        if rest:
            rows_loop(share * n_fc, rest, lambda r: gather_row(code_next_ref, other, r).start())
            rows_loop(share * n_fc, rest, lambda r: scatter_row(code_prev_ref, other, r).start())

    def swiglu(x, wg, wu, wd):
        a = jnp.dot(x, wg, preferred_element_type=F32)
        b = jnp.dot(x, wu, preferred_element_type=F32)
        return jnp.dot((_silu(a) * b).astype(BF16), wd, preferred_element_type=F32)

    whole = (lo == 0) & (hi == tm)

    @pl.when((hi > lo) & (j == 0))
    def _():
        rows = lax.broadcasted_iota(jnp.int32, xb.shape, 0)
        xb[...] = jnp.where((rows >= lo) & (rows < hi), xs_buf[slot], 0.0).astype(BF16)

    @pl.when(whole)
    def _():
        neighbour_share()
        acc[slot] += swiglu(xb[...], wg_ref[...].astype(BF16), wu_ref[...].astype(BF16),
                            wd_ref[...].astype(BF16))

    @pl.when((hi > lo) & jnp.logical_not(whole))
    def _():
        @pl.when(first)
        def _():
            neighbour_share()

        wgb[...] = wg_ref[...].astype(BF16)
        wub[...] = wu_ref[...].astype(BF16)
        wdb[...] = wd_ref[...].astype(BF16)
        for s in range(tm // sub):
            @pl.when((s * sub < hi) & ((s + 1) * sub > lo))
            def _():
                rs = pl.ds(s * sub, sub)
                acc[slot, rs, :] += swiglu(xb[rs, :], wgb[...], wub[...], wdb[...])

    @pl.when(last & (j == n_fc - 1))
    def _():
        wait_scatter(other)

        @pl.when(final)
        def _():
            rows_loop(0, tm, lambda r: scatter_row(code_ref, slot, r).start())
            wait_scatter(slot)
            wait_gather(other)


def _moe(h3, codes, work, wg, wu, wd, tm):
    m, d = h3.shape
    n_e, _, ff = wg.shape
    fc = min(MOE_FF, ff)
    assert ff % fc == 0
    n_fc = ff // fc
    sub = min(MOE_SUB, tm)
    n_work = work[0].shape[0]
    n_tiles = codes.shape[0]

    def jj(w, j, wlo, whi):
        return jnp.where(whi[w] > wlo[w], j, n_fc - 1)

    grid_spec = pltpu.PrefetchScalarGridSpec(
        num_scalar_prefetch=5,
        grid=(n_work, n_fc),
        in_specs=[pl.BlockSpec((None, 1, tm), lambda w, j, we, wt, wlo, whi, wf: (jnp.maximum(wt[w] - 1, 0), 0, 0),
                               memory_space=pltpu.SMEM),
                  pl.BlockSpec((None, 1, tm), lambda w, j, we, wt, wlo, whi, wf: (wt[w], 0, 0),
                               memory_space=pltpu.SMEM),
                  pl.BlockSpec((None, 1, tm),
                               lambda w, j, we, wt, wlo, whi, wf: (jnp.minimum(wt[w] + 1, n_tiles - 1), 0, 0),
                               memory_space=pltpu.SMEM),
                  pl.BlockSpec(memory_space=pl.ANY),
                  pl.BlockSpec((None, d, fc), lambda w, j, we, wt, wlo, whi, wf: (we[w], 0, jj(w, j, wlo, whi))),
                  pl.BlockSpec((None, d, fc), lambda w, j, we, wt, wlo, whi, wf: (we[w], 0, jj(w, j, wlo, whi))),
                  pl.BlockSpec((None, fc, d), lambda w, j, we, wt, wlo, whi, wf: (we[w], jj(w, j, wlo, whi), 0))],
        out_specs=pl.BlockSpec(memory_space=pl.ANY),
        scratch_shapes=[pltpu.VMEM((2, tm, d), F32), pltpu.VMEM((2, tm, d), F32), pltpu.VMEM((tm, d), BF16),
                        pltpu.VMEM((d, fc), BF16), pltpu.VMEM((d, fc), BF16), pltpu.VMEM((fc, d), BF16),
                        pltpu.SemaphoreType.DMA((2,)), pltpu.SemaphoreType.DMA((2,))])
    return pl.pallas_call(
        functools.partial(_moe_kernel, sub, n_fc),
        out_shape=jax.ShapeDtypeStruct((2, m, d), F32),
        grid_spec=grid_spec,
        compiler_params=_params(("arbitrary", "arbitrary")),
        name="moe",
    )(*work, codes, codes, codes, h3, wg, wu, wd)


def _moe_work_items(counts, tm, n_tiles):
    n_e = counts.shape[0]
    n_work = n_tiles + n_e - 1
    uend = jnp.cumsum(counts)
    ustart = uend - counts
    first_tile = ustart // tm
    n_w = jnp.where(counts > 0, (uend - 1) // tm - first_tile + 1, 0)
    wend = jnp.cumsum(n_w)
    wstart = wend - n_w
    total = wend[-1]
    idx = jnp.arange(n_work, dtype=jnp.int32)
    w = jnp.minimum(idx, total - 1)
    we = jnp.sum((w[:, None] >= wend[None, :]).astype(jnp.int32), axis=1)
    wt = first_tile[we] + (w - wstart[we])
    lo = jnp.clip(ustart[we] - wt * tm, 0, tm)
    hi = jnp.clip(uend[we] - wt * tm, 0, tm)
    live = idx < total
    hi = jnp.where(live, hi, lo)
    prev_t = jnp.concatenate([jnp.full((1,), -1, jnp.int32), wt[:-1]])
    next_t = jnp.concatenate([wt[1:], jnp.full((1,), -1, jnp.int32)])
    final = idx == total - 1
    flags = (jnp.where(live & (wt != prev_t), FLAG_FIRST, 0)
             + jnp.where(live & ((wt != next_t) | final), FLAG_LAST, 0)
             + jnp.where(final, FLAG_FINAL, 0))
    as_i32 = lambda a: a.astype(jnp.int32)
    return ustart, (as_i32(we), as_i32(wt), as_i32(lo), as_i32(hi), as_i32(flags))


def _combine_kernel(final, y0_ref, y1_ref, mf_ref, x3_ref, gate_ref, gout_ref, o_ref):
    mf = mf_ref[...]
    moe = mf[:, 0:1] * y0_ref[...] + mf[:, 1:2] * y1_ref[...]
    x4 = x3_ref[...] + gate_ref[...] * moe
    o_ref[...] = _rmsnorm(x4, gout_ref[...]) if final else x4


def _combine(grp, yk, mf, x3, mod3, layer, gout, final):
    m, d = x3.shape
    nt = grp.grid[1]
    choice = lambda k: pl.BlockSpec((None, grp.tile, d), lambda b, j: (k, b * nt + j, 0))
    return pl.pallas_call(
        functools.partial(_combine_kernel, final),
        out_shape=jax.ShapeDtypeStruct((m, d), F32),
        grid=grp.grid,
        in_specs=[choice(0), choice(1), grp.row_spec(LANES), grp.row_spec(d), grp.mod_spec(layer, 5, d),
                  _small((1, d))],
        out_specs=grp.row_spec(d),
        compiler_params=_params(("arbitrary", "arbitrary")),
        name="combine",
    )(yk, yk, mf, x3, grp.mod_arg(mod3), gout)


def _moe_layer(grp, x3, h3, mi, mf, cnt, mod3, layer, wg, wu, wd, gout, final):
    m, d = h3.shape
    n_e = wg.shape[0]
    tm = min(MOE_TILE, m)
    n_slots = 2 * m
    assert n_slots % tm == 0
    n_tiles = n_slots // tm
    counts = cnt[0, :n_e].astype(jnp.int32)
    ustart, work = _moe_work_items(counts, tm, n_tiles)

    e = mi[0:2]
    pos = mi[2:4]
    for k in range(n_e):
        pos = pos + jnp.where(e == k, ustart[k], 0)
    codes = _invert(pos.reshape(n_slots)).reshape(n_tiles, 1, tm)

    yk = _moe(h3, codes, work, wg, wu, wd, tm)
    return _combine(grp, yk, mf, x3, mod3, layer, gout, final)


def _trunk(grp, x, mod3, st_a, st_b, st_c, p):
    depth = p['w_ada'].shape[0]
    d = x.shape[1]
    new_a, new_b, new_c = [], [], []
    for i in range(depth):
        j = i // 2
        gmix = p['norm_mix'][i].reshape(1, d)
        gffn = p['norm_ffn'][i].reshape(1, d)
        if i % 2 == 0:
            u = _in_proj(grp, x, mod3, i, gmix, p['w_in'][j])
            conv_w = (p['w_conv_a'][j], p['w_conv_b'][j], p['b_conv_b'][j].reshape(1, -1),
                      p['ln_b_g'][j].reshape(1, -1), p['ln_b_b'][j].reshape(1, -1), p['w_out'][j])
            if grp.per_row_mod:
                x, na, nb = _conv_step(grp, u, st_a[j], st_b[j], x, mod3, i, *conv_w)
            else:
                x, na, nb = _conv_seq(grp, u, x, mod3, i, *conv_w)
            new_a.append(na)
            new_b.append(nb)
            x = _ffn(grp, x, mod3, i, gffn, p['w_ffn_gate'][j], p['w_ffn_up'][j], p['w_ffn_down'][j])
        else:
            ctx = p['pool_ctx']
            pool_w = (p['w_pool'][j], p['pool_scale'][j].reshape(1, d), p['wr_hi'][j], p['wr_lo'][j], p['br'][j])
            if grp.per_row_mod:
                x3, h3, mi, mf, nc, cnt = _pool_step(grp, x, st_c[j], mod3, i, gmix, gffn, *pool_w, ctx)
            else:
                x3, h3, mi, mf, nc, cnt = _pool_seq(grp, x, mod3, i, gmix, gffn, *pool_w, ctx)
            new_c.append(nc)
            final = i == depth - 1
            x = _moe_layer(grp, x3, h3, mi, mf, cnt, mod3, i, p['w_exp_gate'][j], p['w_exp_up'][j],
                           p['w_exp_down'][j], p['norm_out'].reshape(1, d), final)
    return x, new_a, new_b, new_c


def kernel(x_prompt, x_sample, state_a, state_b, state_c, c_prompt, c_sample, w_ada, b_ada, norm_mix, norm_ffn, norm_out, w_in, w_conv_a, w_conv_b, b_conv_b, ln_b_g, ln_b_b, w_out, w_ffn_gate, w_ffn_up, w_ffn_down, w_pool, pool_scale, w_router, b_router, w_exp_gate, w_exp_up, w_exp_down):
    n_p, seq, d = x_prompt.shape
    n_s, dec_seq, _ = x_sample.shape
    depth = w_ada.shape[0]
    n_e = w_router.shape[2]
    assert dec_seq == 1 and depth % 2 == 0 and n_s % SUBLANES == 0 and n_e <= LANES
    n_even, n_odd = state_a.shape[0], state_c.shape[0]
    ctx = state_c.shape[2]

    wr = jnp.pad(w_router, ((0, 0), (0, 0), (0, LANES - n_e)))
    wr_hi = wr.astype(BF16)
    wr_lo = (wr - wr_hi.astype(F32)).astype(BF16)
    br = jnp.pad(b_router, ((0, 0), (0, LANES - n_e)), constant_values=NEG_BIG).reshape(n_odd, 1, LANES)

    p = {
        'w_ada': w_ada, 'norm_mix': norm_mix, 'norm_ffn': norm_ffn, 'norm_out': norm_out,
        'w_in': w_in.astype(BF16), 'w_conv_a': w_conv_a, 'w_conv_b': w_conv_b, 'b_conv_b': b_conv_b,
        'ln_b_g': ln_b_g, 'ln_b_b': ln_b_b, 'w_out': w_out.astype(BF16),
        'w_ffn_gate': w_ffn_gate.astype(BF16), 'w_ffn_up': w_ffn_up.astype(BF16),
        'w_ffn_down': w_ffn_down.astype(BF16),
        'w_pool': w_pool.astype(BF16), 'pool_scale': pool_scale, 'wr_hi': wr_hi, 'wr_lo': wr_lo, 'br': br,
        'w_exp_gate': w_exp_gate, 'w_exp_up': w_exp_up, 'w_exp_down': w_exp_down, 'pool_ctx': ctx,
    }

    mod3 = _ada(jnp.concatenate([c_sample, c_prompt], axis=0), w_ada, b_ada)

    g_prompt = _Group(n_p, seq, False, n_s, n_s)
    g_sample = _Group(n_s, 1, True, 0, n_s)

    y_p, pa, pb, pc = _trunk(g_prompt, x_prompt.reshape(n_p * seq, d), mod3, None, None, None, p)
    y_s, sa, sb, sc = _trunk(
        g_sample, x_sample.reshape(n_s, d), mod3,
        state_a.reshape(n_even, n_s, -1), state_b.reshape(n_even, n_s, -1), state_c.reshape(n_odd, n_s, -1), p)

    sa = [a.reshape(n_s, state_a.shape[2], -1) for a in sa]
    sb = [b.reshape(n_s, state_b.shape[2], -1) for b in sb]
    sc = [c.reshape(n_s, ctx, d) for c in sc]
    return (y_p.reshape(n_p, seq, d), y_s.reshape(n_s, 1, d),
            jnp.stack(pa), jnp.stack(sa), jnp.stack(pb), jnp.stack(sb), jnp.stack(pc), jnp.stack(sc))
```

```python
import functools

import jax
import jax.numpy as jnp
from jax import lax
from jax.experimental import pallas as pl
from jax.experimental.pallas import tpu as pltpu

EPS = 1e-6
PAST_LEN = 16384
POOL_WINDOWS = (2, 4, 8, 16)
N_MOD = 6
LANES = 128
SUBLANES = 8
VMEM_LIMIT = 56 * 1024 * 1024
NEG_BIG = -1e30

ROW_TILE = 512
MOE_TILE = 1024
MOE_SUB = 256
MOE_FF = 512

F32 = jnp.float32
BF16 = jnp.bfloat16


def _params(sem):
    return pltpu.CompilerParams(dimension_semantics=sem, vmem_limit_bytes=VMEM_LIMIT)


def _silu(x):
    return x * jax.nn.sigmoid(x)


def _rmsnorm(x, g):
    return x * lax.rsqrt(jnp.mean(x * x, axis=-1, keepdims=True) + EPS) * g


def _mod_rmsnorm(x, g, sc, sh):
    return _rmsnorm(x, g) * (1.0 + sc) + sh


def _resident(shape):
    nd = len(shape)
    return pl.BlockSpec(shape, lambda *_: (0,) * nd, pipeline_mode=pl.Buffered(1))


def _small(shape):
    nd = len(shape)
    return pl.BlockSpec(shape, lambda *_: (0,) * nd)


class _Group:
    def __init__(self, n_seq, seq_len, per_row_mod, mod_row0, n_sample):
        self.n_seq, self.seq_len = n_seq, seq_len
        self.per_row_mod = per_row_mod
        self.mod_row0 = mod_row0
        self.n_sample = n_sample
        if per_row_mod:
            self.tile = n_seq
            self.grid = (1, 1)
        else:
            self.tile = min(ROW_TILE, seq_len)
            assert seq_len % self.tile == 0 and self.tile % 32 == 0
            self.grid = (n_seq, seq_len // self.tile)
        self.rows = n_seq * seq_len

    def row_spec(self, width):
        nt = self.grid[1]
        return pl.BlockSpec((self.tile, width), lambda b, j, *_: (b * nt + j, 0))

    def mod_spec(self, layer, chunk, d):
        if self.per_row_mod:
            return pl.BlockSpec((None, self.n_sample, d), lambda b, j, *_: (layer, 0, chunk))
        r0 = self.mod_row0
        return pl.BlockSpec((None, None, 1, d), lambda b, j, *_: (layer, r0 + b, 0, chunk))

    def mod_arg(self, mod3):
        if self.per_row_mod:
            return mod3
        l, r, w = mod3.shape
        return mod3.reshape(l, r, 1, w)


def _ada_kernel(c_ref, w_ref, b_ref, o_ref):
    cs = _silu(c_ref[...]).astype(BF16)
    o_ref[...] = jnp.dot(cs, w_ref[...].astype(BF16), preferred_element_type=F32) + b_ref[...]


def _ada(c_all, w_ada, b_ada):
    depth, d, w6 = w_ada.shape
    r = c_all.shape[0]
    tn = w6 // 4
    return pl.pallas_call(
        _ada_kernel,
        out_shape=jax.ShapeDtypeStruct((depth, r, w6), F32),
        grid=(depth, w6 // tn),
        in_specs=[pl.BlockSpec((r, d), lambda l, n: (0, 0)),
                  pl.BlockSpec((None, d, tn), lambda l, n: (l, 0, n)),
                  pl.BlockSpec((None, 1, tn), lambda l, n: (l, 0, n))],
        out_specs=pl.BlockSpec((None, r, tn), lambda l, n: (l, 0, n)),
        compiler_params=_params(("arbitrary", "arbitrary")),
        name="ada",
    )(c_all, w_ada, b_ada.reshape(depth, 1, w6))


def _in_proj_kernel(x_ref, g_ref, sc_ref, sh_ref, w_ref, u_ref):
    h = _mod_rmsnorm(x_ref[...], g_ref[...], sc_ref[...], sh_ref[...])
    u_ref[...] = jnp.dot(h.astype(BF16), w_ref[...], preferred_element_type=F32)


def _in_proj(grp, x, mod3, layer, g, w_bf):
    d, n = w_bf.shape
    return pl.pallas_call(
        _in_proj_kernel,
        out_shape=jax.ShapeDtypeStruct((grp.rows, n), F32),
        grid=grp.grid,
        in_specs=[grp.row_spec(d), _small((1, d)),
                  grp.mod_spec(layer, 1, d), grp.mod_spec(layer, 0, d),
                  _resident((d, n))],
        out_specs=grp.row_spec(n),
        compiler_params=_params(("arbitrary", "arbitrary")),
        name="in_proj",
    )(x, g, grp.mod_arg(mod3), grp.mod_arg(mod3), w_bf)


def _layernorm(y, g, b):
    mu = jnp.mean(y, axis=-1, keepdims=True)
    yc = y - mu
    var = jnp.mean(yc * yc, axis=-1, keepdims=True)
    return yc * lax.rsqrt(var + EPS) * g + b


def _causal_taps(pad_ref, w_ref, k_taps, halo, tt):
    off0 = halo - (k_taps - 1)
    y = None
    for c in range(SUBLANES):
        n = tt + (SUBLANES if c else 0)
        q = None
        for k in range(k_taps):
            if (off0 + k) % SUBLANES == c:
                a = off0 + k - c
                term = w_ref[k:k + 1, :] * pad_ref[a:a + n, :]
                q = term if q is None else q + term
        if q is not None:
            part = q[c:c + tt, :]
            y = part if y is None else y + part
    return y


def _conv_seq_kernel(n_t, ka, kb, u_ref, uh_ref, x_ref, gm_ref, wa_ref, wb_ref, bb_ref,
                     lng_ref, lnb_ref, wout_ref, x1_ref, na_ref, nb_ref, apad, gpad):
    j = pl.program_id(1)
    tt = u_ref.shape[0]
    da = wa_ref.shape[1]
    hb = uh_ref.shape[0]
    keep = (j > 0).astype(F32)

    u = u_ref[...]
    a_b, a_c, a_x = u[:, 0:da], u[:, da:2 * da], u[:, 2 * da:3 * da]
    b_v, b_g = u[:, 3 * da:4 * da], u[:, 4 * da:5 * da]
    uh = uh_ref[...]
    apad[0:SUBLANES, :] = keep * (uh[hb - SUBLANES:hb, da:2 * da] * uh[hb - SUBLANES:hb, 2 * da:3 * da])
    apad[SUBLANES:SUBLANES + tt, :] = a_c * a_x
    gpad[0:hb, :] = keep * (uh[:, 3 * da:4 * da] * jax.nn.sigmoid(uh[:, 4 * da:5 * da]))
    gpad[hb:hb + tt, :] = b_v * jax.nn.sigmoid(b_g)

    zero_tail = jnp.zeros((SUBLANES, da), F32)
    apad[SUBLANES + tt:SUBLANES + tt + SUBLANES, :] = zero_tail
    gpad[hb + tt:hb + tt + SUBLANES, :] = zero_tail

    ya = a_b * _causal_taps(apad, wa_ref, ka, SUBLANES, tt)
    yb = _causal_taps(gpad, wb_ref, kb, hb, tt)
    yb = _silu(_layernorm(yb + bb_ref[...], lng_ref[...], lnb_ref[...]))

    ycat = jnp.concatenate([ya, yb], axis=-1).astype(BF16)
    y = jnp.dot(ycat, wout_ref[...], preferred_element_type=F32)
    x1_ref[...] = x_ref[...] + gm_ref[...] * y

    @pl.when(j == n_t - 1)
    def _():
        na_ref[...] = apad[SUBLANES + tt - (ka - 1):SUBLANES + tt, :]
        nb_ref[...] = gpad[hb + tt - (kb - 1):hb + tt, :]


def _conv_seq(grp, u, x, mod3, layer, wa, wb, bb, lng, lnb, wout_bf):
    tt = grp.tile
    n_b, n_t = grp.grid
    d = x.shape[1]
    ka, da = wa.shape
    kb = wb.shape[0]
    hb = 32
    assert kb - 1 <= hb and ka - 1 <= SUBLANES and tt % hb == 0
    per = tt // hb
    halo = pl.BlockSpec((hb, u.shape[1]), lambda b, j: (jnp.maximum((b * n_t + j) * per - 1, 0), 0))
    return pl.pallas_call(
        functools.partial(_conv_seq_kernel, n_t, ka, kb),
        out_shape=(jax.ShapeDtypeStruct((grp.rows, d), F32),
                   jax.ShapeDtypeStruct((n_b, ka - 1, da), F32),
                   jax.ShapeDtypeStruct((n_b, kb - 1, da), F32)),
        grid=grp.grid,
        in_specs=[grp.row_spec(u.shape[1]), halo, grp.row_spec(d), grp.mod_spec(layer, 2, d),
                  _small(wa.shape), _small(wb.shape), _small((1, da)), _small((1, da)), _small((1, da)),
                  _resident(wout_bf.shape)],
        out_specs=(grp.row_spec(d),
                   pl.BlockSpec((None, ka - 1, da), lambda b, j: (b, 0, 0)),
                   pl.BlockSpec((None, kb - 1, da), lambda b, j: (b, 0, 0))),
        scratch_shapes=[pltpu.VMEM((2 * SUBLANES + tt, da), F32), pltpu.VMEM((hb + tt + SUBLANES, da), F32)],
        compiler_params=_params(("arbitrary", "arbitrary")),
        name="conv_seq",
    )(u, u, x, grp.mod_arg(mod3), wa, wb, bb, lng, lnb, wout_bf)


def _conv_step_kernel(ka, kb, u_ref, sa_ref, sb_ref, x_ref, gm_ref, wa_ref, wb_ref, bb_ref,
                      lng_ref, lnb_ref, wout_ref, x1_ref, na_ref, nb_ref):
    da = wa_ref.shape[1]
    u = u_ref[...]
    a_b, a_c, a_x = u[:, 0:da], u[:, da:2 * da], u[:, 2 * da:3 * da]
    b_v, b_g = u[:, 3 * da:4 * da], u[:, 4 * da:5 * da]

    cur = a_c * a_x
    ya = wa_ref[ka - 1:ka, :] * cur
    for k in range(ka - 1):
        ya = ya + wa_ref[k:k + 1, :] * sa_ref[:, k * da:(k + 1) * da]
    ya = a_b * ya
    glu = b_v * jax.nn.sigmoid(b_g)
    yb = wb_ref[kb - 1:kb, :] * glu
    for k in range(kb - 1):
        yb = yb + wb_ref[k:k + 1, :] * sb_ref[:, k * da:(k + 1) * da]
    yb = _silu(_layernorm(yb + bb_ref[...], lng_ref[...], lnb_ref[...]))

    ycat = jnp.concatenate([ya, yb], axis=-1).astype(BF16)
    y = jnp.dot(ycat, wout_ref[...], preferred_element_type=F32)
    x1_ref[...] = x_ref[...] + gm_ref[...] * y

    if ka > 2:
        na_ref[:, 0:(ka - 2) * da] = sa_ref[:, da:(ka - 1) * da]
    na_ref[:, (ka - 2) * da:(ka - 1) * da] = cur
    if kb > 2:
        nb_ref[:, 0:(kb - 2) * da] = sb_ref[:, da:(kb - 1) * da]
    nb_ref[:, (kb - 2) * da:(kb - 1) * da] = glu


def _conv_step(grp, u, sa2, sb2, x, mod3, layer, wa, wb, bb, lng, lnb, wout_bf):
    n = grp.rows
    d = x.shape[1]
    ka, da = wa.shape
    kb = wb.shape[0]
    return pl.pallas_call(
        functools.partial(_conv_step_kernel, ka, kb),
        out_shape=(jax.ShapeDtypeStruct((n, d), F32),
                   jax.ShapeDtypeStruct(sa2.shape, F32),
                   jax.ShapeDtypeStruct(sb2.shape, F32)),
        grid=grp.grid,
        in_specs=[_small(u.shape), _small(sa2.shape), _small(sb2.shape), _small(x.shape),
                  grp.mod_spec(layer, 2, d),
                  _small(wa.shape), _small(wb.shape), _small((1, da)), _small((1, da)), _small((1, da)),
                  _small(wout_bf.shape)],
        out_specs=(_small((n, d)), _small(sa2.shape), _small(sb2.shape)),
        compiler_params=_params(("arbitrary", "arbitrary")),
        name="conv_step",
    )(u, sa2, sb2, x, grp.mod_arg(mod3), wa, wb, bb, lng, lnb, wout_bf)


def _ffn_kernel(x_ref, g_ref, sc_ref, sh_ref, gate_ref, wg_ref, wu_ref, wd_ref, o_ref):
    x = x_ref[...]
    h = _mod_rmsnorm(x, g_ref[...], sc_ref[...], sh_ref[...]).astype(BF16)
    a = jnp.dot(h, wg_ref[...], preferred_element_type=F32)
    b = jnp.dot(h, wu_ref[...], preferred_element_type=F32)
    act = (_silu(a) * b).astype(BF16)
    f = jnp.dot(act, wd_ref[...], preferred_element_type=F32)
    o_ref[...] = x + gate_ref[...] * f


def _ffn(grp, x, mod3, layer, g, wg_bf, wu_bf, wd_bf):
    d = x.shape[1]
    return pl.pallas_call(
        _ffn_kernel,
        out_shape=jax.ShapeDtypeStruct(x.shape, F32),
        grid=grp.grid,
        in_specs=[grp.row_spec(d), _small((1, d)),
                  grp.mod_spec(layer, 4, d), grp.mod_spec(layer, 3, d), grp.mod_spec(layer, 5, d),
                  _resident(wg_bf.shape), _resident(wu_bf.shape), _resident(wd_bf.shape)],
        out_specs=grp.row_spec(d),
        compiler_params=_params(("arbitrary", "arbitrary")),
        name="ffn",
    )(x, g, grp.mod_arg(mod3), grp.mod_arg(mod3), grp.mod_arg(mod3), wg_bf, wu_bf, wd_bf)


def _window_sum(v, win):
    assert win & (win - 1) == 0
    span = 1
    while span < win:
        v = v + pltpu.roll(v, span, 0)
        span *= 2
    return v


def _pool_project(diff, wp_ref):
    n_g, pg, _ = wp_ref.shape
    outs = [jnp.dot(diff[:, gi * pg:(gi + 1) * pg].astype(BF16), wp_ref[gi], preferred_element_type=F32)
            for gi in range(n_g)]
    return jnp.concatenate(outs, axis=-1)


def _route(h3, wrh_ref, wrl_ref, br_ref, tri_ref, base_ref, mi_ref, mf_ref, cnt_ref):
    tt = h3.shape[0]
    h_hi = h3.astype(BF16)
    h_lo = (h3 - h_hi.astype(F32)).astype(BF16)
    logits = (jnp.dot(h_hi, wrh_ref[...], preferred_element_type=F32)
              + jnp.dot(h_lo, wrh_ref[...], preferred_element_type=F32)
              + jnp.dot(h_hi, wrl_ref[...], preferred_element_type=F32)
              + br_ref[...])
    lane = lax.broadcasted_iota(jnp.int32, (tt, LANES), 1).astype(F32)
    m0 = jnp.max(logits, axis=-1, keepdims=True)
    e0 = jnp.min(jnp.where(logits == m0, lane, float(LANES)), axis=-1, keepdims=True)
    rest = jnp.where(lane == e0, NEG_BIG * 2.0, logits)
    m1 = jnp.max(rest, axis=-1, keepdims=True)
    e1 = jnp.min(jnp.where(rest == m1, lane, float(LANES)), axis=-1, keepdims=True)
    dd = jnp.exp(m1 - m0)
    p0 = 1.0 / (1.0 + dd)
    p1 = dd * p0

    oh0 = lane == e0
    oh1 = lane == e1
    c = jnp.where(oh0 | oh1, 1.0, 0.0)
    prefix = jnp.dot(tri_ref[...], c.astype(BF16), preferred_element_type=F32)
    tot = base_ref[...] + prefix
    r0 = jnp.sum(jnp.where(oh0, tot, 0.0), axis=-1, keepdims=True)
    r1 = jnp.sum(jnp.where(oh1, tot, 0.0), axis=-1, keepdims=True)
    new_base = base_ref[...] + jnp.sum(c, axis=0, keepdims=True)
    base_ref[...] = new_base
    cnt_ref[...] = jnp.broadcast_to(new_base, cnt_ref.shape)

    meta = jnp.where(lane == 0.0, e0, jnp.where(lane == 1.0, e1,
                     jnp.where(lane == 2.0, r0, jnp.where(lane == 3.0, r1, 0.0))))
    mi_ref[...] = meta.T[0:SUBLANES, :].astype(jnp.int32)
    mf_ref[...] = jnp.where(lane == 0.0, p0, jnp.where(lane == 1.0, p1, 0.0))


def _init_route_scratch(first, tri_ref, base_ref, base0=None):
    @pl.when(first)
    def _():
        tt = tri_ref.shape[0]
        r = lax.broadcasted_iota(jnp.int32, (tt, tt), 0)
        c = lax.broadcasted_iota(jnp.int32, (tt, tt), 1)
        tri_ref[...] = jnp.where(c < r, 1.0, 0.0).astype(BF16)
        base_ref[...] = jnp.zeros(base_ref.shape, F32) if base0 is None else base0


def _pool_seq_kernel(n_t, ctx, x_ref, xh_ref, gmix_ref, shm_ref, scm_ref, gm_ref, gffn_ref, shf_ref, scf_ref,
                     wp_ref, ps_ref, wrh_ref, wrl_ref, br_ref,
                     x3_ref, h3_ref, mi_ref, mf_ref, nc_ref, cnt_ref, hpad, tri_ref, base_ref):
    b = pl.program_id(0)
    j = pl.program_id(1)
    tt = x_ref.shape[0]
    hb = xh_ref.shape[0]
    n_g, pg, _ = wp_ref.shape
    _init_route_scratch((b == 0) & (j == 0), tri_ref, base_ref)

    keep = (j > 0).astype(F32)
    x = x_ref[...]
    h = _mod_rmsnorm(x, gmix_ref[...], scm_ref[...], shm_ref[...])
    hpad[0:hb, :] = keep * _mod_rmsnorm(xh_ref[...], gmix_ref[...], scm_ref[...], shm_ref[...])
    hpad[hb:hb + tt, :] = h

    pos = lax.broadcasted_iota(jnp.int32, (tt, pg), 0) + j * tt
    groups = []
    for gi, win in enumerate(POOL_WINDOWS):
        sl = slice(gi * pg, (gi + 1) * pg)
        s = _window_sum(hpad[:, sl], win)[hb:hb + tt, :]
        cnt = jnp.minimum(pos + 1, win).astype(F32)
        groups.append(s / cnt - h[:, sl])
    diff = jnp.concatenate(groups, axis=-1)
    y = _pool_project(diff, wp_ref) * ps_ref[...]
    x3 = x + gm_ref[...] * y
    x3_ref[...] = x3
    h3 = _mod_rmsnorm(x3, gffn_ref[...], scf_ref[...], shf_ref[...])
    h3_ref[...] = h3
    _route(h3, wrh_ref, wrl_ref, br_ref, tri_ref, base_ref, mi_ref, mf_ref, cnt_ref)

    @pl.when(j == n_t - 1)
    def _():
        nc_ref[...] = hpad[hb + tt - ctx:hb + tt, :]


def _route_out_shapes(rows, d):
    return (jax.ShapeDtypeStruct((rows, d), F32), jax.ShapeDtypeStruct((rows, d), F32),
            jax.ShapeDtypeStruct((SUBLANES, rows), jnp.int32), jax.ShapeDtypeStruct((rows, LANES), F32))


def _pool_seq(grp, x, mod3, layer, gmix, gffn, wp_bf, ps, wrh, wrl, br, ctx):
    tt = grp.tile
    n_b, n_t = grp.grid
    d = x.shape[1]
    hb = 16
    assert ctx <= hb and max(POOL_WINDOWS) - 1 <= hb and tt % hb == 0
    per = tt // hb
    halo = pl.BlockSpec((hb, d), lambda b, j: (jnp.maximum((b * n_t + j) * per - 1, 0), 0))
    ms = lambda c: grp.mod_spec(layer, c, d)
    return pl.pallas_call(
        functools.partial(_pool_seq_kernel, n_t, ctx),
        out_shape=_route_out_shapes(grp.rows, d) + (
            jax.ShapeDtypeStruct((n_b, ctx, d), F32), jax.ShapeDtypeStruct((SUBLANES, LANES), F32)),
        grid=grp.grid,
        in_specs=[grp.row_spec(d), halo, _small((1, d)), ms(0), ms(1), ms(2), _small((1, d)), ms(3), ms(4),
                  _small(wp_bf.shape), _small((1, d)), _small(wrh.shape), _small(wrl.shape), _small(br.shape)],
        out_specs=(grp.row_spec(d), grp.row_spec(d), pl.BlockSpec((SUBLANES, tt), lambda b, j: (0, b * n_t + j)),
                   grp.row_spec(LANES), pl.BlockSpec((None, ctx, d), lambda b, j: (b, 0, 0)), _small((SUBLANES, LANES))),
        scratch_shapes=[pltpu.VMEM((hb + tt, d), F32), pltpu.VMEM((tt, tt), BF16), pltpu.VMEM((1, LANES), F32)],
        compiler_params=_params(("arbitrary", "arbitrary")),
        name="pool_seq",
    )(x, x, gmix, *([grp.mod_arg(mod3)] * 3), gffn, *([grp.mod_arg(mod3)] * 2), wp_bf, ps, wrh, wrl, br)


def _pool_step_kernel(ctx, x_ref, sc_ref_state, cnt0_ref, gmix_ref, shm_ref, scm_ref, gm_ref, gffn_ref, shf_ref,
                      scf_ref, wp_ref, ps_ref, wrh_ref, wrl_ref, br_ref,
                      x3_ref, h3_ref, mi_ref, mf_ref, nc_ref, cnt_ref, tri_ref, base_ref):
    d = x_ref.shape[1]
    n_g, pg, _ = wp_ref.shape
    _init_route_scratch(pl.program_id(0) == 0, tri_ref, base_ref, cnt0_ref[0:1, :])

    x = x_ref[...]
    h = _mod_rmsnorm(x, gmix_ref[...], scm_ref[...], shm_ref[...])
    groups = []
    for gi, win in enumerate(POOL_WINDOWS):
        s = h[:, gi * pg:(gi + 1) * pg]
        for i in range(1, win):
            row = ctx - i
            s = s + sc_ref_state[:, row * d + gi * pg:row * d + (gi + 1) * pg]
        cnt = float(min(PAST_LEN + 1, win))
        groups.append(s / cnt - h[:, gi * pg:(gi + 1) * pg])
    diff = jnp.concatenate(groups, axis=-1)
    y = _pool_project(diff, wp_ref) * ps_ref[...]
    x3 = x + gm_ref[...] * y
    x3_ref[...] = x3
    h3 = _mod_rmsnorm(x3, gffn_ref[...], scf_ref[...], shf_ref[...])
    h3_ref[...] = h3
    _route(h3, wrh_ref, wrl_ref, br_ref, tri_ref, base_ref, mi_ref, mf_ref, cnt_ref)

    if ctx > 1:
        nc_ref[:, 0:(ctx - 1) * d] = sc_ref_state[:, d:ctx * d]
    nc_ref[:, (ctx - 1) * d:ctx * d] = h


def _pool_step(grp, x, sc2, cnt0, mod3, layer, gmix, gffn, wp_bf, ps, wrh, wrl, br, ctx):
    n, d = x.shape
    ms = lambda c: grp.mod_spec(layer, c, d)
    return pl.pallas_call(
        functools.partial(_pool_step_kernel, ctx),
        out_shape=_route_out_shapes(n, d) + (
            jax.ShapeDtypeStruct(sc2.shape, F32), jax.ShapeDtypeStruct((SUBLANES, LANES), F32)),
        grid=grp.grid,
        in_specs=[_small(x.shape), _small(sc2.shape), _small(cnt0.shape), _small((1, d)), ms(0), ms(1), ms(2),
                  _small((1, d)), ms(3), ms(4),
                  _small(wp_bf.shape), _small((1, d)), _small(wrh.shape), _small(wrl.shape), _small(br.shape)],
        out_specs=(_small((n, d)), _small((n, d)), _small((SUBLANES, n)), _small((n, LANES)),
                   _small(sc2.shape), _small((SUBLANES, LANES))),
        scratch_shapes=[pltpu.VMEM((n, n), BF16), pltpu.VMEM((1, LANES), F32)],
        compiler_params=_params(("arbitrary", "arbitrary")),
        name="pool_step",
    )(x, sc2, cnt0, gmix, *([grp.mod_arg(mod3)] * 3), gffn, *([grp.mod_arg(mod3)] * 2), wp_bf, ps, wrh, wrl, br)


def _invert_kernel(pos_ref, code_ref):
    n_real = pos_ref.shape[0]
    m = n_real // 2

    def body(i, carry):
        code_ref[pos_ref[i]] = 2 * i
        code_ref[pos_ref[m + i]] = 2 * i + 1
        return carry

    lax.fori_loop(0, m, body, 0, unroll=8)

    def pad(s, carry):
        code_ref[s] = s
        return carry

    lax.fori_loop(n_real, code_ref.shape[0], pad, 0)


def _invert(pos_flat, n_slots):
    smem = pl.BlockSpec(memory_space=pltpu.SMEM)
    return pl.pallas_call(
        _invert_kernel,
        out_shape=jax.ShapeDtypeStruct((n_slots,), jnp.int32),
        in_specs=[smem], out_specs=smem,
        name="invert",
    )(pos_flat)


FLAG_FIRST, FLAG_LAST, FLAG_FINAL = 1, 2, 4


def _moe_kernel(sub, n_fc, we_ref, wt_ref, wlo_ref, whi_ref, wflag_ref, wfix_lo_ref, wfix_hi_ref,
                code_prev_ref, code_ref, code_next_ref, h_hbm, h2_hbm, wg_ref, wu_ref, wd_ref, yk_hbm,
                xs_buf, acc, xb, wgb, wub, wdb, gsem, ssem, fsem):
    w = pl.program_id(0)
    j = pl.program_id(1)
    tm = xb.shape[0]
    t = wt_ref[w]
    slot = lax.rem(t, 2)
    other = 1 - slot
    lo = wlo_ref[w]
    hi = whi_ref[w]
    flag = wflag_ref[w]
    first = (flag & FLAG_FIRST) != 0
    last = (flag & FLAG_LAST) != 0
    final = (flag & FLAG_FINAL) != 0
    share = (tm // n_fc) // SUBLANES * SUBLANES
    rest = tm - share * n_fc

    m1 = h_hbm.shape[0]

    def gather_row(codes, s, r):
        tok = jnp.minimum(lax.shift_right_logical(codes[0, r], 1), m1 - 1)
        return pltpu.make_async_copy(h_hbm.at[pl.ds(tok, 1), :], xs_buf.at[s, pl.ds(r, 1), :], gsem.at[s])

    def refetch_second_group(r, carry):
        tok = lax.shift_right_logical(code_ref[0, r], 1) - m1
        cp = pltpu.make_async_copy(h2_hbm.at[pl.ds(tok, 1), :], xs_buf.at[slot, pl.ds(r, 1), :], fsem)
        cp.start()
        cp.wait()
        return carry

    def scatter_row(codes, s, r):
        code = codes[0, r]
        dst = yk_hbm.at[code & 1, pl.ds(lax.shift_right_logical(code, 1), 1), :]
        return pltpu.make_async_copy(acc.at[s, pl.ds(r, 1), :], dst, ssem.at[s])

    def wait_gather(s):
        pltpu.make_async_copy(h_hbm.at[pl.ds(0, tm), :], xs_buf.at[s], gsem.at[s]).wait()

    def wait_scatter(s):
        pltpu.make_async_copy(acc.at[s], yk_hbm.at[0, pl.ds(0, tm), :], ssem.at[s]).wait()

    def rows_loop(r0, n, fn):
        def body(i, carry):
            fn(r0 + i)
            return carry

        lax.fori_loop(0, n, body, 0, unroll=SUBLANES)

    def neighbour_share():
        base = pl.multiple_of(j * share, SUBLANES)
        for u in range(share):
            gather_row(code_next_ref, other, base + u).start()
            scatter_row(code_prev_ref, other, base + u).start()

    @pl.when((w == 0) & (j == 0))
    def _():
        rows_loop(0, tm, lambda r: gather_row(code_ref, 0, r).start())
        acc[1] = jnp.zeros((tm, acc.shape[2]), F32)

    @pl.when(first & (j == 0))
    def _():
        wait_gather(slot)
        acc[slot] = jnp.zeros((tm, acc.shape[2]), F32)
        if rest:
            rows_loop(share * n_fc, rest, lambda r: gather_row(code_next_ref, other, r).start())
            rows_loop(share * n_fc, rest, lambda r: scatter_row(code_prev_ref, other, r).start())

    def swiglu(x, wg, wu, wd):
        a = jnp.dot(x, wg, preferred_element_type=F32)
        b = jnp.dot(x, wu, preferred_element_type=F32)
        return jnp.dot((_silu(a) * b).astype(BF16), wd, preferred_element_type=F32)

    whole = (lo == 0) & (hi == tm)

    @pl.when((hi > lo) & (j == 0))
    def _():
        lax.fori_loop(wfix_lo_ref[w], wfix_hi_ref[w], refetch_second_group, 0)
        rows = lax.broadcasted_iota(jnp.int32, xb.shape, 0)
        xb[...] = jnp.where((rows >= lo) & (rows < hi), xs_buf[slot], 0.0).astype(BF16)

    @pl.when(whole)
    def _():
        neighbour_share()
        acc[slot] += swiglu(xb[...], wg_ref[...].astype(BF16), wu_ref[...].astype(BF16),
                            wd_ref[...].astype(BF16))

    @pl.when((hi > lo) & jnp.logical_not(whole))
    def _():
        @pl.when(first)
        def _():
            neighbour_share()

        wgb[...] = wg_ref[...].astype(BF16)
        wub[...] = wu_ref[...].astype(BF16)
        wdb[...] = wd_ref[...].astype(BF16)
        for s in range(tm // sub):
            @pl.when((s * sub < hi) & ((s + 1) * sub > lo))
            def _():
                rs = pl.ds(s * sub, sub)
                acc[slot, rs, :] += swiglu(xb[rs, :], wgb[...], wub[...], wdb[...])

    @pl.when(last & (j == n_fc - 1))
    def _():
        wait_scatter(other)

        @pl.when(final)
        def _():
            rows_loop(0, tm, lambda r: scatter_row(code_ref, slot, r).start())
            wait_scatter(slot)
            wait_gather(other)


def _moe(h3, h3b, codes, work, wg, wu, wd, tm):
    d = h3.shape[1]
    n_e, _, ff = wg.shape
    fc = min(MOE_FF, ff)
    assert ff % fc == 0
    n_fc = ff // fc
    sub = min(MOE_SUB, tm)
    n_work = work[0].shape[0]
    n_tiles = codes.shape[0]

    def jj(w, j, wlo, whi):
        return jnp.where(whi[w] > wlo[w], j, n_fc - 1)

    grid_spec = pltpu.PrefetchScalarGridSpec(
        num_scalar_prefetch=7,
        grid=(n_work, n_fc),
        in_specs=[pl.BlockSpec((None, 1, tm), lambda w, j, we, wt, wlo, whi, wf, fl, fh: (jnp.maximum(wt[w] - 1, 0), 0, 0),
                               memory_space=pltpu.SMEM),
                  pl.BlockSpec((None, 1, tm), lambda w, j, we, wt, wlo, whi, wf, fl, fh: (wt[w], 0, 0),
                               memory_space=pltpu.SMEM),
                  pl.BlockSpec((None, 1, tm),
                               lambda w, j, we, wt, wlo, whi, wf, fl, fh: (jnp.minimum(wt[w] + 1, n_tiles - 1), 0, 0),
                               memory_space=pltpu.SMEM),
                  pl.BlockSpec(memory_space=pl.ANY), pl.BlockSpec(memory_space=pl.ANY),
                  pl.BlockSpec((None, d, fc), lambda w, j, we, wt, wlo, whi, wf, fl, fh: (we[w], 0, jj(w, j, wlo, whi))),
                  pl.BlockSpec((None, d, fc), lambda w, j, we, wt, wlo, whi, wf, fl, fh: (we[w], 0, jj(w, j, wlo, whi))),
                  pl.BlockSpec((None, fc, d), lambda w, j, we, wt, wlo, whi, wf, fl, fh: (we[w], jj(w, j, wlo, whi), 0))],
        out_specs=pl.BlockSpec(memory_space=pl.ANY),
        scratch_shapes=[pltpu.VMEM((2, tm, d), F32), pltpu.VMEM((2, tm, d), F32), pltpu.VMEM((tm, d), BF16),
                        pltpu.VMEM((d, fc), BF16), pltpu.VMEM((d, fc), BF16), pltpu.VMEM((fc, d), BF16),
                        pltpu.SemaphoreType.DMA((2,)), pltpu.SemaphoreType.DMA((2,)), pltpu.SemaphoreType.DMA(())])
    return pl.pallas_call(
        functools.partial(_moe_kernel, sub, n_fc),
        out_shape=jax.ShapeDtypeStruct((2, n_tiles * tm // 2, d), F32),
        grid_spec=grid_spec,
        compiler_params=_params(("arbitrary", "arbitrary")),
        name="moe",
    )(*work, codes, codes, codes, h3, h3b, wg, wu, wd)


def _moe_work_items(counts, counts2, tm, n_tiles):
    n_e = counts.shape[0]
    n_work = n_tiles + n_e - 1
    uend = jnp.cumsum(counts)
    ustart = uend - counts
    first_tile = ustart // tm
    n_w = jnp.where(counts > 0, (uend - 1) // tm - first_tile + 1, 0)
    wend = jnp.cumsum(n_w)
    wstart = wend - n_w
    total = wend[-1]
    idx = jnp.arange(n_work, dtype=jnp.int32)
    w = jnp.minimum(idx, total - 1)
    we = jnp.sum((w[:, None] >= wend[None, :]).astype(jnp.int32), axis=1)
    wt = first_tile[we] + (w - wstart[we])
    lo = jnp.clip(ustart[we] - wt * tm, 0, tm)
    hi = jnp.clip(uend[we] - wt * tm, 0, tm)
    live = idx < total
    hi = jnp.where(live, hi, lo)
    prev_t = jnp.concatenate([jnp.full((1,), -1, jnp.int32), wt[:-1]])
    next_t = jnp.concatenate([wt[1:], jnp.full((1,), -1, jnp.int32)])
    final = idx == total - 1
    flags = (jnp.where(live & (wt != prev_t), FLAG_FIRST, 0)
             + jnp.where(live & ((wt != next_t) | final), FLAG_LAST, 0)
             + jnp.where(final, FLAG_FINAL, 0))
    fix_lo = jnp.clip(uend[we] - counts2[we] - wt * tm, lo, hi)
    as_i32 = lambda a: a.astype(jnp.int32)
    return ustart, (as_i32(we), as_i32(wt), as_i32(lo), as_i32(hi), as_i32(flags), as_i32(fix_lo), as_i32(hi))


def _combine_kernel(final, y0_ref, y1_ref, mf_ref, x3_ref, gate_ref, gout_ref, o_ref):
    mf = mf_ref[...]
    moe = mf[:, 0:1] * y0_ref[...] + mf[:, 1:2] * y1_ref[...]
    x4 = x3_ref[...] + gate_ref[...] * moe
    o_ref[...] = _rmsnorm(x4, gout_ref[...]) if final else x4


def _combine(grp, yk, row0, mf, x3, mod3, layer, gout, final):
    m, d = x3.shape
    nt = grp.grid[1]
    assert row0 % grp.tile == 0
    t0 = row0 // grp.tile
    choice = lambda k: pl.BlockSpec((None, grp.tile, d), lambda b, j: (k, t0 + b * nt + j, 0))
    return pl.pallas_call(
        functools.partial(_combine_kernel, final),
        out_shape=jax.ShapeDtypeStruct((m, d), F32),
        grid=grp.grid,
        in_specs=[choice(0), choice(1), grp.row_spec(LANES), grp.row_spec(d), grp.mod_spec(layer, 5, d),
                  _small((1, d))],
        out_specs=grp.row_spec(d),
        compiler_params=_params(("arbitrary", "arbitrary")),
        name="combine",
    )(yk, yk, mf, x3, grp.mod_arg(mod3), gout)


def _moe_layer(groups, x3s, h3s, mis, mfs, cnt1, cnt, mod3, layer, wg, wu, wd, gout, final):
    m1, d = h3s[0].shape
    m = m1 + h3s[1].shape[0]
    n_e = wg.shape[0]
    tm = min(MOE_TILE, m1)
    n_tiles = pl.cdiv(2 * m, tm)
    counts = cnt[0, :n_e].astype(jnp.int32)
    counts2 = counts - cnt1[0, :n_e].astype(jnp.int32)
    ustart, work = _moe_work_items(counts, counts2, tm, n_tiles)

    mi = jnp.concatenate(mis, axis=1)
    e = mi[0:2]
    pos = mi[2:4]
    for k in range(n_e):
        pos = pos + jnp.where(e == k, ustart[k], 0)
    codes = _invert(pos.reshape(2 * m), n_tiles * tm).reshape(n_tiles, 1, tm)

    yk = _moe(h3s[0], h3s[1], codes, work, wg, wu, wd, tm)
    return [_combine(g, yk, r0, mf, x3, mod3, layer, gout, final)
            for g, r0, mf, x3 in zip(groups, (0, m1), mfs, x3s)]


def _layers(g_p, g_s, x_p, x_s, mod3, st_a, st_b, st_c, p):
    depth = p['w_ada'].shape[0]
    d = x_p.shape[1]
    new = {k: [] for k in ('pa', 'sa', 'pb', 'sb', 'pc', 'sc')}
    for i in range(depth):
        j = i // 2
        gmix = p['norm_mix'][i].reshape(1, d)
        gffn = p['norm_ffn'][i].reshape(1, d)
        if i % 2 == 0:
            conv_w = (p['w_conv_a'][j], p['w_conv_b'][j], p['b_conv_b'][j].reshape(1, -1),
                      p['ln_b_g'][j].reshape(1, -1), p['ln_b_b'][j].reshape(1, -1), p['w_out'][j])
            ffn_w = (p['w_ffn_gate'][j], p['w_ffn_up'][j], p['w_ffn_down'][j])
            u_p = _in_proj(g_p, x_p, mod3, i, gmix, p['w_in'][j])
            x_p, na, nb = _conv_seq(g_p, u_p, x_p, mod3, i, *conv_w)
            new['pa'].append(na)
            new['pb'].append(nb)
            x_p = _ffn(g_p, x_p, mod3, i, gffn, *ffn_w)
            u_s = _in_proj(g_s, x_s, mod3, i, gmix, p['w_in'][j])
            x_s, na, nb = _conv_step(g_s, u_s, st_a[j], st_b[j], x_s, mod3, i, *conv_w)
            new['sa'].append(na)
            new['sb'].append(nb)
            x_s = _ffn(g_s, x_s, mod3, i, gffn, *ffn_w)
        else:
            ctx = p['pool_ctx']
            pool_w = (p['w_pool'][j], p['pool_scale'][j].reshape(1, d), p['wr_hi'][j], p['wr_lo'][j], p['br'][j])
            x3_p, h3_p, mi_p, mf_p, nc, cnt1 = _pool_seq(g_p, x_p, mod3, i, gmix, gffn, *pool_w, ctx)
            new['pc'].append(nc)
            x3_s, h3_s, mi_s, mf_s, nc, cnt = _pool_step(g_s, x_s, st_c[j], cnt1, mod3, i, gmix, gffn, *pool_w, ctx)
            new['sc'].append(nc)
            x_p, x_s = _moe_layer((g_p, g_s), (x3_p, x3_s), (h3_p, h3_s), (mi_p, mi_s), (mf_p, mf_s), cnt1, cnt,
                                  mod3, i, p['w_exp_gate'][j], p['w_exp_up'][j], p['w_exp_down'][j],
                                  p['norm_out'].reshape(1, d), i == depth - 1)
    return x_p, x_s, new


def kernel(x_prompt, x_sample, state_a, state_b, state_c, c_prompt, c_sample, w_ada, b_ada, norm_mix, norm_ffn, norm_out, w_in, w_conv_a, w_conv_b, b_conv_b, ln_b_g, ln_b_b, w_out, w_ffn_gate, w_ffn_up, w_ffn_down, w_pool, pool_scale, w_router, b_router, w_exp_gate, w_exp_up, w_exp_down):
    n_p, seq, d = x_prompt.shape
    n_s, dec_seq, _ = x_sample.shape
    depth = w_ada.shape[0]
    n_e = w_router.shape[2]
    assert dec_seq == 1 and depth % 2 == 0 and n_s % SUBLANES == 0 and n_e <= LANES
    n_even, n_odd = state_a.shape[0], state_c.shape[0]
    ctx = state_c.shape[2]

    wr = jnp.pad(w_router, ((0, 0), (0, 0), (0, LANES - n_e)))
    wr_hi = wr.astype(BF16)
    wr_lo = (wr - wr_hi.astype(F32)).astype(BF16)
    br = jnp.pad(b_router, ((0, 0), (0, LANES - n_e)), constant_values=NEG_BIG).reshape(n_odd, 1, LANES)

    p = {
        'w_ada': w_ada, 'norm_mix': norm_mix, 'norm_ffn': norm_ffn, 'norm_out': norm_out,
        'w_in': w_in.astype(BF16), 'w_conv_a': w_conv_a, 'w_conv_b': w_conv_b, 'b_conv_b': b_conv_b,
        'ln_b_g': ln_b_g, 'ln_b_b': ln_b_b, 'w_out': w_out.astype(BF16),
        'w_ffn_gate': w_ffn_gate.astype(BF16), 'w_ffn_up': w_ffn_up.astype(BF16),
        'w_ffn_down': w_ffn_down.astype(BF16),
        'w_pool': w_pool.astype(BF16), 'pool_scale': pool_scale, 'wr_hi': wr_hi, 'wr_lo': wr_lo, 'br': br,
        'w_exp_gate': w_exp_gate, 'w_exp_up': w_exp_up, 'w_exp_down': w_exp_down, 'pool_ctx': ctx,
    }

    mod3 = _ada(jnp.concatenate([c_sample, c_prompt], axis=0), w_ada, b_ada)

    g_prompt = _Group(n_p, seq, False, n_s, n_s)
    g_sample = _Group(n_s, 1, True, 0, n_s)

    y_p, y_s, new = _layers(
        g_prompt, g_sample, x_prompt.reshape(n_p * seq, d), x_sample.reshape(n_s, d), mod3,
        state_a.reshape(n_even, n_s, -1), state_b.reshape(n_even, n_s, -1), state_c.reshape(n_odd, n_s, -1), p)

    sa = [a.reshape(n_s, state_a.shape[2], -1) for a in new['sa']]
    sb = [b.reshape(n_s, state_b.shape[2], -1) for b in new['sb']]
    sc = [c.reshape(n_s, ctx, d) for c in new['sc']]
    return (y_p.reshape(n_p, seq, d), y_s.reshape(n_s, 1, d),
            jnp.stack(new['pa']), jnp.stack(sa), jnp.stack(new['pb']), jnp.stack(sb),
            jnp.stack(new['pc']), jnp.stack(sc))
```

```python
import functools

import jax
import jax.numpy as jnp
from jax import lax
from jax.experimental import pallas as pl
from jax.experimental.pallas import tpu as pltpu

EPS = 1e-6
PAST_LEN = 16384
POOL_WINDOWS = (2, 4, 8, 16)
N_MOD = 6
LANES = 128
SUBLANES = 8
VMEM_LIMIT = 56 * 1024 * 1024
NEG_BIG = -1e30

ROW_TILE = 512
MOE_TILE = 1024
MOE_SUB = 256
MOE_FF = 512

F32 = jnp.float32
BF16 = jnp.bfloat16


def _params(sem):
    return pltpu.CompilerParams(dimension_semantics=sem, vmem_limit_bytes=VMEM_LIMIT)


def _silu(x):
    return x * jax.nn.sigmoid(x)


def _rmsnorm(x, g):
    return x * lax.rsqrt(jnp.mean(x * x, axis=-1, keepdims=True) + EPS) * g


def _mod_rmsnorm(x, g, sc, sh):
    return _rmsnorm(x, g) * (1.0 + sc) + sh


def _resident(shape):
    nd = len(shape)
    return pl.BlockSpec(shape, lambda *_: (0,) * nd, pipeline_mode=pl.Buffered(1))


def _small(shape):
    nd = len(shape)
    return pl.BlockSpec(shape, lambda *_: (0,) * nd)


class _Group:
    def __init__(self, n_seq, seq_len, per_row_mod, mod_row0, n_sample):
        self.n_seq, self.seq_len = n_seq, seq_len
        self.per_row_mod = per_row_mod
        self.mod_row0 = mod_row0
        self.n_sample = n_sample
        if per_row_mod:
            self.tile = n_seq
            self.grid = (1, 1)
        else:
            self.tile = min(ROW_TILE, seq_len)
            assert seq_len % self.tile == 0 and self.tile % 32 == 0
            self.grid = (n_seq, seq_len // self.tile)
        self.rows = n_seq * seq_len

    def row_spec(self, width):
        nt = self.grid[1]
        return pl.BlockSpec((self.tile, width), lambda b, j, *_: (b * nt + j, 0))

    def mod_spec(self, layer, chunk, d):
        if self.per_row_mod:
            return pl.BlockSpec((None, self.n_sample, d), lambda b, j, *_: (layer, 0, chunk))
        r0 = self.mod_row0
        return pl.BlockSpec((None, None, 1, d), lambda b, j, *_: (layer, r0 + b, 0, chunk))

    def mod_arg(self, mod3):
        if self.per_row_mod:
            return mod3
        l, r, w = mod3.shape
        return mod3.reshape(l, r, 1, w)


def _ada_kernel(c_ref, w_ref, b_ref, o_ref):
    cs = _silu(c_ref[...]).astype(BF16)
    o_ref[...] = jnp.dot(cs, w_ref[...].astype(BF16), preferred_element_type=F32) + b_ref[...]


def _ada(c_all, w_ada, b_ada):
    depth, d, w6 = w_ada.shape
    r = c_all.shape[0]
    tn = w6 // 4
    return pl.pallas_call(
        _ada_kernel,
        out_shape=jax.ShapeDtypeStruct((depth, r, w6), F32),
        grid=(depth, w6 // tn),
        in_specs=[pl.BlockSpec((r, d), lambda l, n: (0, 0)),
                  pl.BlockSpec((None, d, tn), lambda l, n: (l, 0, n)),
                  pl.BlockSpec((None, 1, tn), lambda l, n: (l, 0, n))],
        out_specs=pl.BlockSpec((None, r, tn), lambda l, n: (l, 0, n)),
        compiler_params=_params(("arbitrary", "arbitrary")),
        name="ada",
    )(c_all, w_ada, b_ada.reshape(depth, 1, w6))


def _in_proj_kernel(x_ref, g_ref, sc_ref, sh_ref, w_ref, u_ref):
    h = _mod_rmsnorm(x_ref[...], g_ref[...], sc_ref[...], sh_ref[...])
    u_ref[...] = jnp.dot(h.astype(BF16), w_ref[...], preferred_element_type=F32)


def _in_proj(grp, x, mod3, layer, g, w_bf):
    d, n = w_bf.shape
    return pl.pallas_call(
        _in_proj_kernel,
        out_shape=jax.ShapeDtypeStruct((grp.rows, n), F32),
        grid=grp.grid,
        in_specs=[grp.row_spec(d), _small((1, d)),
                  grp.mod_spec(layer, 1, d), grp.mod_spec(layer, 0, d),
                  _resident((d, n))],
        out_specs=grp.row_spec(n),
        compiler_params=_params(("arbitrary", "arbitrary")),
        name="in_proj",
    )(x, g, grp.mod_arg(mod3), grp.mod_arg(mod3), w_bf)


def _layernorm(y, g, b):
    mu = jnp.mean(y, axis=-1, keepdims=True)
    yc = y - mu
    var = jnp.mean(yc * yc, axis=-1, keepdims=True)
    return yc * lax.rsqrt(var + EPS) * g + b


def _causal_taps(pad_ref, w_ref, k_taps, halo, tt):
    off0 = halo - (k_taps - 1)
    y = None
    for c in range(SUBLANES):
        n = tt + (SUBLANES if c else 0)
        q = None
        for k in range(k_taps):
            if (off0 + k) % SUBLANES == c:
                a = off0 + k - c
                term = w_ref[k:k + 1, :] * pad_ref[a:a + n, :]
                q = term if q is None else q + term
        if q is not None:
            part = q[c:c + tt, :]
            y = part if y is None else y + part
    return y


def _conv_seq_kernel(n_t, ka, kb, u_ref, uh_ref, x_ref, gm_ref, wa_ref, wb_ref, bb_ref,
                     lng_ref, lnb_ref, wout_ref, x1_ref, na_ref, nb_ref, apad, gpad):
    j = pl.program_id(1)
    tt = u_ref.shape[0]
    da = wa_ref.shape[1]
    hb = uh_ref.shape[0]
    keep = (j > 0).astype(F32)

    u = u_ref[...]
    a_b, a_c, a_x = u[:, 0:da], u[:, da:2 * da], u[:, 2 * da:3 * da]
    b_v, b_g = u[:, 3 * da:4 * da], u[:, 4 * da:5 * da]
    uh = uh_ref[...]
    apad[0:SUBLANES, :] = keep * (uh[hb - SUBLANES:hb, da:2 * da] * uh[hb - SUBLANES:hb, 2 * da:3 * da])
    apad[SUBLANES:SUBLANES + tt, :] = a_c * a_x
    gpad[0:hb, :] = keep * (uh[:, 3 * da:4 * da] * jax.nn.sigmoid(uh[:, 4 * da:5 * da]))
    gpad[hb:hb + tt, :] = b_v * jax.nn.sigmoid(b_g)

    zero_tail = jnp.zeros((SUBLANES, da), F32)
    apad[SUBLANES + tt:SUBLANES + tt + SUBLANES, :] = zero_tail
    gpad[hb + tt:hb + tt + SUBLANES, :] = zero_tail

    ya = a_b * _causal_taps(apad, wa_ref, ka, SUBLANES, tt)
    yb = _causal_taps(gpad, wb_ref, kb, hb, tt)
    yb = _silu(_layernorm(yb + bb_ref[...], lng_ref[...], lnb_ref[...]))

    ycat = jnp.concatenate([ya, yb], axis=-1).astype(BF16)
    y = jnp.dot(ycat, wout_ref[...], preferred_element_type=F32)
    x1_ref[...] = x_ref[...] + gm_ref[...] * y

    @pl.when(j == n_t - 1)
    def _():
        na_ref[...] = apad[SUBLANES + tt - (ka - 1):SUBLANES + tt, :]
        nb_ref[...] = gpad[hb + tt - (kb - 1):hb + tt, :]


def _conv_seq(grp, u, x, mod3, layer, wa, wb, bb, lng, lnb, wout_bf):
    tt = grp.tile
    n_b, n_t = grp.grid
    d = x.shape[1]
    ka, da = wa.shape
    kb = wb.shape[0]
    hb = 32
    assert kb - 1 <= hb and ka - 1 <= SUBLANES and tt % hb == 0
    per = tt // hb
    halo = pl.BlockSpec((hb, u.shape[1]), lambda b, j: (jnp.maximum((b * n_t + j) * per - 1, 0), 0))
    return pl.pallas_call(
        functools.partial(_conv_seq_kernel, n_t, ka, kb),
        out_shape=(jax.ShapeDtypeStruct((grp.rows, d), F32),
                   jax.ShapeDtypeStruct((n_b, ka - 1, da), F32),
                   jax.ShapeDtypeStruct((n_b, kb - 1, da), F32)),
        grid=grp.grid,
        in_specs=[grp.row_spec(u.shape[1]), halo, grp.row_spec(d), grp.mod_spec(layer, 2, d),
                  _small(wa.shape), _small(wb.shape), _small((1, da)), _small((1, da)), _small((1, da)),
                  _resident(wout_bf.shape)],
        out_specs=(grp.row_spec(d),
                   pl.BlockSpec((None, ka - 1, da), lambda b, j: (b, 0, 0)),
                   pl.BlockSpec((None, kb - 1, da), lambda b, j: (b, 0, 0))),
        scratch_shapes=[pltpu.VMEM((2 * SUBLANES + tt, da), F32), pltpu.VMEM((hb + tt + SUBLANES, da), F32)],
        compiler_params=_params(("arbitrary", "arbitrary")),
        name="conv_seq",
    )(u, u, x, grp.mod_arg(mod3), wa, wb, bb, lng, lnb, wout_bf)


def _conv_step_kernel(ka, kb, u_ref, sa_ref, sb_ref, x_ref, gm_ref, wa_ref, wb_ref, bb_ref,
                      lng_ref, lnb_ref, wout_ref, x1_ref, na_ref, nb_ref):
    da = wa_ref.shape[1]
    u = u_ref[...]
    a_b, a_c, a_x = u[:, 0:da], u[:, da:2 * da], u[:, 2 * da:3 * da]
    b_v, b_g = u[:, 3 * da:4 * da], u[:, 4 * da:5 * da]

    cur = a_c * a_x
    ya = wa_ref[ka - 1:ka, :] * cur
    for k in range(ka - 1):
        ya = ya + wa_ref[k:k + 1, :] * sa_ref[:, k * da:(k + 1) * da]
    ya = a_b * ya
    glu = b_v * jax.nn.sigmoid(b_g)
    yb = wb_ref[kb - 1:kb, :] * glu
    for k in range(kb - 1):
        yb = yb + wb_ref[k:k + 1, :] * sb_ref[:, k * da:(k + 1) * da]
    yb = _silu(_layernorm(yb + bb_ref[...], lng_ref[...], lnb_ref[...]))

    ycat = jnp.concatenate([ya, yb], axis=-1).astype(BF16)
    y = jnp.dot(ycat, wout_ref[...], preferred_element_type=F32)
    x1_ref[...] = x_ref[...] + gm_ref[...] * y

    if ka > 2:
        na_ref[:, 0:(ka - 2) * da] = sa_ref[:, da:(ka - 1) * da]
    na_ref[:, (ka - 2) * da:(ka - 1) * da] = cur
    if kb > 2:
        nb_ref[:, 0:(kb - 2) * da] = sb_ref[:, da:(kb - 1) * da]
    nb_ref[:, (kb - 2) * da:(kb - 1) * da] = glu


def _conv_step(grp, u, sa2, sb2, x, mod3, layer, wa, wb, bb, lng, lnb, wout_bf):
    n = grp.rows
    d = x.shape[1]
    ka, da = wa.shape
    kb = wb.shape[0]
    return pl.pallas_call(
        functools.partial(_conv_step_kernel, ka, kb),
        out_shape=(jax.ShapeDtypeStruct((n, d), F32),
                   jax.ShapeDtypeStruct(sa2.shape, F32),
                   jax.ShapeDtypeStruct(sb2.shape, F32)),
        grid=grp.grid,
        in_specs=[_small(u.shape), _small(sa2.shape), _small(sb2.shape), _small(x.shape),
                  grp.mod_spec(layer, 2, d),
                  _small(wa.shape), _small(wb.shape), _small((1, da)), _small((1, da)), _small((1, da)),
                  _small(wout_bf.shape)],
        out_specs=(_small((n, d)), _small(sa2.shape), _small(sb2.shape)),
        compiler_params=_params(("arbitrary", "arbitrary")),
        name="conv_step",
    )(u, sa2, sb2, x, grp.mod_arg(mod3), wa, wb, bb, lng, lnb, wout_bf)


def _ffn_kernel(x_ref, g_ref, sc_ref, sh_ref, gate_ref, wg_ref, wu_ref, wd_ref, o_ref):
    x = x_ref[...]
    h = _mod_rmsnorm(x, g_ref[...], sc_ref[...], sh_ref[...]).astype(BF16)
    a = jnp.dot(h, wg_ref[...], preferred_element_type=F32)
    b = jnp.dot(h, wu_ref[...], preferred_element_type=F32)
    act = (_silu(a) * b).astype(BF16)
    f = jnp.dot(act, wd_ref[...], preferred_element_type=F32)
    o_ref[...] = x + gate_ref[...] * f


def _ffn(grp, x, mod3, layer, g, wg_bf, wu_bf, wd_bf):
    d = x.shape[1]
    return pl.pallas_call(
        _ffn_kernel,
        out_shape=jax.ShapeDtypeStruct(x.shape, F32),
        grid=grp.grid,
        in_specs=[grp.row_spec(d), _small((1, d)),
                  grp.mod_spec(layer, 4, d), grp.mod_spec(layer, 3, d), grp.mod_spec(layer, 5, d),
                  _resident(wg_bf.shape), _resident(wu_bf.shape), _resident(wd_bf.shape)],
        out_specs=grp.row_spec(d),
        compiler_params=_params(("arbitrary", "arbitrary")),
        name="ffn",
    )(x, g, grp.mod_arg(mod3), grp.mod_arg(mod3), grp.mod_arg(mod3), wg_bf, wu_bf, wd_bf)


def _window_sum(v, win):
    assert win & (win - 1) == 0
    span = 1
    while span < win:
        v = v + pltpu.roll(v, span, 0)
        span *= 2
    return v


def _pool_project(diff, wp_ref):
    n_g, pg, _ = wp_ref.shape
    outs = [jnp.dot(diff[:, gi * pg:(gi + 1) * pg].astype(BF16), wp_ref[gi], preferred_element_type=F32)
            for gi in range(n_g)]
    return jnp.concatenate(outs, axis=-1)


def _route(h3, wrh_ref, wrl_ref, br_ref, tri_ref, base_ref, mi_ref, mf_ref, cnt_ref):
    tt = h3.shape[0]
    h_hi = h3.astype(BF16)
    h_lo = (h3 - h_hi.astype(F32)).astype(BF16)
    logits = (jnp.dot(h_hi, wrh_ref[...], preferred_element_type=F32)
              + jnp.dot(h_lo, wrh_ref[...], preferred_element_type=F32)
              + jnp.dot(h_hi, wrl_ref[...], preferred_element_type=F32)
              + br_ref[...])
    lane = lax.broadcasted_iota(jnp.int32, (tt, LANES), 1).astype(F32)
    m0 = jnp.max(logits, axis=-1, keepdims=True)
    e0 = jnp.min(jnp.where(logits == m0, lane, float(LANES)), axis=-1, keepdims=True)
    rest = jnp.where(lane == e0, NEG_BIG * 2.0, logits)
    m1 = jnp.max(rest, axis=-1, keepdims=True)
    e1 = jnp.min(jnp.where(rest == m1, lane, float(LANES)), axis=-1, keepdims=True)
    dd = jnp.exp(m1 - m0)
    p0 = 1.0 / (1.0 + dd)
    p1 = dd * p0

    oh0 = lane == e0
    oh1 = lane == e1
    c = jnp.where(oh0 | oh1, 1.0, 0.0)
    prefix = jnp.dot(tri_ref[...], c.astype(BF16), preferred_element_type=F32)
    tot = base_ref[...] + prefix
    r0 = jnp.sum(jnp.where(oh0, tot, 0.0), axis=-1, keepdims=True)
    r1 = jnp.sum(jnp.where(oh1, tot, 0.0), axis=-1, keepdims=True)
    new_base = base_ref[...] + jnp.sum(c, axis=0, keepdims=True)
    base_ref[...] = new_base
    cnt_ref[...] = jnp.broadcast_to(new_base, cnt_ref.shape)

    meta = jnp.where(lane == 0.0, e0, jnp.where(lane == 1.0, e1,
                     jnp.where(lane == 2.0, r0, jnp.where(lane == 3.0, r1, 0.0))))
    mi_ref[...] = meta.T[0:SUBLANES, :].astype(jnp.int32)
    mf_ref[...] = jnp.where(lane == 0.0, p0, jnp.where(lane == 1.0, p1, 0.0))


def _init_route_scratch(first, tri_ref, base_ref, base0=None):
    @pl.when(first)
    def _():
        tt = tri_ref.shape[0]
        r = lax.broadcasted_iota(jnp.int32, (tt, tt), 0)
        c = lax.broadcasted_iota(jnp.int32, (tt, tt), 1)
        tri_ref[...] = jnp.where(c < r, 1.0, 0.0).astype(BF16)
        base_ref[...] = jnp.zeros(base_ref.shape, F32) if base0 is None else base0


def _pool_seq_kernel(n_t, ctx, x_ref, xh_ref, gmix_ref, shm_ref, scm_ref, gm_ref, gffn_ref, shf_ref, scf_ref,
                     wp_ref, ps_ref, wrh_ref, wrl_ref, br_ref,
                     x3_ref, h3_ref, mi_ref, mf_ref, nc_ref, cnt_ref, hpad, tri_ref, base_ref):
    b = pl.program_id(0)
    j = pl.program_id(1)
    tt = x_ref.shape[0]
    hb = xh_ref.shape[0]
    n_g, pg, _ = wp_ref.shape
    _init_route_scratch((b == 0) & (j == 0), tri_ref, base_ref)

    keep = (j > 0).astype(F32)
    x = x_ref[...]
    h = _mod_rmsnorm(x, gmix_ref[...], scm_ref[...], shm_ref[...])
    hpad[0:hb, :] = keep * _mod_rmsnorm(xh_ref[...], gmix_ref[...], scm_ref[...], shm_ref[...])
    hpad[hb:hb + tt, :] = h

    pos = lax.broadcasted_iota(jnp.int32, (tt, pg), 0) + j * tt
    groups = []
    for gi, win in enumerate(POOL_WINDOWS):
        sl = slice(gi * pg, (gi + 1) * pg)
        s = _window_sum(hpad[:, sl], win)[hb:hb + tt, :]
        cnt = jnp.minimum(pos + 1, win).astype(F32)
        groups.append(s / cnt - h[:, sl])
    diff = jnp.concatenate(groups, axis=-1)
    y = _pool_project(diff, wp_ref) * ps_ref[...]
    x3 = x + gm_ref[...] * y
    x3_ref[...] = x3
    h3 = _mod_rmsnorm(x3, gffn_ref[...], scf_ref[...], shf_ref[...])
    h3_ref[...] = h3
    _route(h3, wrh_ref, wrl_ref, br_ref, tri_ref, base_ref, mi_ref, mf_ref, cnt_ref)

    @pl.when(j == n_t - 1)
    def _():
        nc_ref[...] = hpad[hb + tt - ctx:hb + tt, :]


def _route_out_shapes(rows, d):
    return (jax.ShapeDtypeStruct((rows, d), F32), jax.ShapeDtypeStruct((rows, d), F32),
            jax.ShapeDtypeStruct((SUBLANES, rows), jnp.int32), jax.ShapeDtypeStruct((rows, LANES), F32))


def _pool_seq(grp, x, mod3, layer, gmix, gffn, wp_bf, ps, wrh, wrl, br, ctx):
    tt = grp.tile
    n_b, n_t = grp.grid
    d = x.shape[1]
    hb = 16
    assert ctx <= hb and max(POOL_WINDOWS) - 1 <= hb and tt % hb == 0
    per = tt // hb
    halo = pl.BlockSpec((hb, d), lambda b, j: (jnp.maximum((b * n_t + j) * per - 1, 0), 0))
    ms = lambda c: grp.mod_spec(layer, c, d)
    return pl.pallas_call(
        functools.partial(_pool_seq_kernel, n_t, ctx),
        out_shape=_route_out_shapes(grp.rows, d) + (
            jax.ShapeDtypeStruct((n_b, ctx, d), F32), jax.ShapeDtypeStruct((SUBLANES, LANES), F32)),
        grid=grp.grid,
        in_specs=[grp.row_spec(d), halo, _small((1, d)), ms(0), ms(1), ms(2), _small((1, d)), ms(3), ms(4),
                  _small(wp_bf.shape), _small((1, d)), _small(wrh.shape), _small(wrl.shape), _small(br.shape)],
        out_specs=(grp.row_spec(d), grp.row_spec(d), pl.BlockSpec((SUBLANES, tt), lambda b, j: (0, b * n_t + j)),
                   grp.row_spec(LANES), pl.BlockSpec((None, ctx, d), lambda b, j: (b, 0, 0)), _small((SUBLANES, LANES))),
        scratch_shapes=[pltpu.VMEM((hb + tt, d), F32), pltpu.VMEM((tt, tt), BF16), pltpu.VMEM((1, LANES), F32)],
        compiler_params=_params(("arbitrary", "arbitrary")),
        name="pool_seq",
    )(x, x, gmix, *([grp.mod_arg(mod3)] * 3), gffn, *([grp.mod_arg(mod3)] * 2), wp_bf, ps, wrh, wrl, br)


def _pool_step_kernel(ctx, x_ref, sc_ref_state, cnt0_ref, gmix_ref, shm_ref, scm_ref, gm_ref, gffn_ref, shf_ref,
                      scf_ref, wp_ref, ps_ref, wrh_ref, wrl_ref, br_ref,
                      x3_ref, h3_ref, mi_ref, mf_ref, nc_ref, cnt_ref, tri_ref, base_ref):
    d = x_ref.shape[1]
    n_g, pg, _ = wp_ref.shape
    _init_route_scratch(pl.program_id(0) == 0, tri_ref, base_ref, cnt0_ref[0:1, :])

    x = x_ref[...]
    h = _mod_rmsnorm(x, gmix_ref[...], scm_ref[...], shm_ref[...])
    groups = []
    for gi, win in enumerate(POOL_WINDOWS):
        s = h[:, gi * pg:(gi + 1) * pg]
        for i in range(1, win):
            row = ctx - i
            s = s + sc_ref_state[:, row * d + gi * pg:row * d + (gi + 1) * pg]
        cnt = float(min(PAST_LEN + 1, win))
        groups.append(s / cnt - h[:, gi * pg:(gi + 1) * pg])
    diff = jnp.concatenate(groups, axis=-1)
    y = _pool_project(diff, wp_ref) * ps_ref[...]
    x3 = x + gm_ref[...] * y
    x3_ref[...] = x3
    h3 = _mod_rmsnorm(x3, gffn_ref[...], scf_ref[...], shf_ref[...])
    h3_ref[...] = h3
    _route(h3, wrh_ref, wrl_ref, br_ref, tri_ref, base_ref, mi_ref, mf_ref, cnt_ref)

    if ctx > 1:
        nc_ref[:, 0:(ctx - 1) * d] = sc_ref_state[:, d:ctx * d]
    nc_ref[:, (ctx - 1) * d:ctx * d] = h


def _pool_step(grp, x, sc2, cnt0, mod3, layer, gmix, gffn, wp_bf, ps, wrh, wrl, br, ctx):
    n, d = x.shape
    ms = lambda c: grp.mod_spec(layer, c, d)
    return pl.pallas_call(
        functools.partial(_pool_step_kernel, ctx),
        out_shape=_route_out_shapes(n, d) + (
            jax.ShapeDtypeStruct(sc2.shape, F32), jax.ShapeDtypeStruct((SUBLANES, LANES), F32)),
        grid=grp.grid,
        in_specs=[_small(x.shape), _small(sc2.shape), _small(cnt0.shape), _small((1, d)), ms(0), ms(1), ms(2),
                  _small((1, d)), ms(3), ms(4),
                  _small(wp_bf.shape), _small((1, d)), _small(wrh.shape), _small(wrl.shape), _small(br.shape)],
        out_specs=(_small((n, d)), _small((n, d)), _small((SUBLANES, n)), _small((n, LANES)),
                   _small(sc2.shape), _small((SUBLANES, LANES))),
        scratch_shapes=[pltpu.VMEM((n, n), BF16), pltpu.VMEM((1, LANES), F32)],
        compiler_params=_params(("arbitrary", "arbitrary")),
        name="pool_step",
    )(x, sc2, cnt0, gmix, *([grp.mod_arg(mod3)] * 3), gffn, *([grp.mod_arg(mod3)] * 2), wp_bf, ps, wrh, wrl, br)


def _invert_kernel(pos_ref, code_ref):
    n_real = pos_ref.shape[0]
    m = n_real // 2

    def body(i, carry):
        code_ref[pos_ref[i]] = 2 * i
        code_ref[pos_ref[m + i]] = 2 * i + 1
        return carry

    lax.fori_loop(0, m, body, 0, unroll=8)

    def pad(s, carry):
        code_ref[s] = s
        return carry

    lax.fori_loop(n_real, code_ref.shape[0], pad, 0)


def _invert(pos_flat, n_slots):
    smem = pl.BlockSpec(memory_space=pltpu.SMEM)
    return pl.pallas_call(
        _invert_kernel,
        out_shape=jax.ShapeDtypeStruct((n_slots,), jnp.int32),
        in_specs=[smem], out_specs=smem,
        name="invert",
    )(pos_flat)


FLAG_FIRST, FLAG_LAST, FLAG_FINAL = 1, 2, 4


def _moe_kernel(sub, n_fc, we_ref, wt_ref, wlo_ref, whi_ref, wflag_ref, wfix_lo_ref, wfix_hi_ref,
                code_prev_ref, code_ref, code_next_ref, h_hbm, h2_hbm, wg_ref, wu_ref, wd_ref, yk_hbm,
                xs_buf, acc, xb, wgb, wub, wdb, gsem, ssem, fsem):
    w = pl.program_id(0)
    j = pl.program_id(1)
    tm = xb.shape[0]
    t = wt_ref[w]
    slot = lax.rem(t, 2)
    other = 1 - slot
    lo = wlo_ref[w]
    hi = whi_ref[w]
    flag = wflag_ref[w]
    first = (flag & FLAG_FIRST) != 0
    last = (flag & FLAG_LAST) != 0
    final = (flag & FLAG_FINAL) != 0
    share = (tm // n_fc) // SUBLANES * SUBLANES
    rest = tm - share * n_fc

    m1 = h_hbm.shape[0]

    def gather_row(codes, s, r):
        tok = jnp.minimum(lax.shift_right_logical(codes[0, r], 1), m1 - 1)
        return pltpu.make_async_copy(h_hbm.at[pl.ds(tok, 1), :], xs_buf.at[s, pl.ds(r, 1), :], gsem.at[s])

    def second_group_row(r):
        tok = lax.shift_right_logical(code_ref[0, r], 1) - m1
        return pltpu.make_async_copy(h2_hbm.at[pl.ds(tok, 1), :], xs_buf.at[slot, pl.ds(r, 1), :], fsem)

    def refetch_second_group(r0, r1):
        def start(r, carry):
            second_group_row(r).start()
            return carry

        def wait(r, carry):
            second_group_row(r).wait()
            return carry

        lax.fori_loop(r0, r1, start, 0)
        lax.fori_loop(r0, r1, wait, 0)

    def scatter_row(codes, s, r):
        code = codes[0, r]
        dst = yk_hbm.at[code & 1, pl.ds(lax.shift_right_logical(code, 1), 1), :]
        return pltpu.make_async_copy(acc.at[s, pl.ds(r, 1), :], dst, ssem.at[s])

    def wait_gather(s):
        pltpu.make_async_copy(h_hbm.at[pl.ds(0, tm), :], xs_buf.at[s], gsem.at[s]).wait()

    def wait_scatter(s):
        pltpu.make_async_copy(acc.at[s], yk_hbm.at[0, pl.ds(0, tm), :], ssem.at[s]).wait()

    def rows_loop(r0, n, fn):
        def body(i, carry):
            fn(r0 + i)
            return carry

        lax.fori_loop(0, n, body, 0, unroll=SUBLANES)

    def neighbour_share():
        base = pl.multiple_of(j * share, SUBLANES)
        for u in range(share):
            gather_row(code_next_ref, other, base + u).start()
            scatter_row(code_prev_ref, other, base + u).start()

    @pl.when((w == 0) & (j == 0))
    def _():
        rows_loop(0, tm, lambda r: gather_row(code_ref, 0, r).start())
        acc[1] = jnp.zeros((tm, acc.shape[2]), F32)

    @pl.when(first & (j == 0))
    def _():
        wait_gather(slot)
        acc[slot] = jnp.zeros((tm, acc.shape[2]), F32)
        if rest:
            rows_loop(share * n_fc, rest, lambda r: gather_row(code_next_ref, other, r).start())
            rows_loop(share * n_fc, rest, lambda r: scatter_row(code_prev_ref, other, r).start())

    def swiglu(x, wg, wu, wd):
        a = jnp.dot(x, wg, preferred_element_type=F32)
        b = jnp.dot(x, wu, preferred_element_type=F32)
        return jnp.dot((_silu(a) * b).astype(BF16), wd, preferred_element_type=F32)

    whole = (lo == 0) & (hi == tm)

    @pl.when((hi > lo) & (j == 0))
    def _():
        refetch_second_group(wfix_lo_ref[w], wfix_hi_ref[w])
        rows = lax.broadcasted_iota(jnp.int32, xb.shape, 0)
        xb[...] = jnp.where((rows >= lo) & (rows < hi), xs_buf[slot], 0.0).astype(BF16)

    @pl.when(whole)
    def _():
        neighbour_share()
        acc[slot] += swiglu(xb[...], wg_ref[...].astype(BF16), wu_ref[...].astype(BF16),
                            wd_ref[...].astype(BF16))

    @pl.when((hi > lo) & jnp.logical_not(whole))
    def _():
        @pl.when(first)
        def _():
            neighbour_share()

        wgb[...] = wg_ref[...].astype(BF16)
        wub[...] = wu_ref[...].astype(BF16)
        wdb[...] = wd_ref[...].astype(BF16)
        for s in range(tm // sub):
            @pl.when((s * sub < hi) & ((s + 1) * sub > lo))
            def _():
                rs = pl.ds(s * sub, sub)
                acc[slot, rs, :] += swiglu(xb[rs, :], wgb[...], wub[...], wdb[...])

    @pl.when(last & (j == n_fc - 1))
    def _():
        wait_scatter(other)

        @pl.when(final)
        def _():
            rows_loop(0, tm, lambda r: scatter_row(code_ref, slot, r).start())
            wait_scatter(slot)
            wait_gather(other)


def _moe(h3, h3b, codes, work, wg, wu, wd, tm):
    d = h3.shape[1]
    n_e, _, ff = wg.shape
    fc = min(MOE_FF, ff)
    assert ff % fc == 0
    n_fc = ff // fc
    sub = min(MOE_SUB, tm)
    n_work = work[0].shape[0]
    n_tiles = codes.shape[0]

    def jj(w, j, wlo, whi):
        return jnp.where(whi[w] > wlo[w], j, n_fc - 1)

    grid_spec = pltpu.PrefetchScalarGridSpec(
        num_scalar_prefetch=7,
        grid=(n_work, n_fc),
        in_specs=[pl.BlockSpec((None, 1, tm), lambda w, j, we, wt, wlo, whi, wf, fl, fh: (jnp.maximum(wt[w] - 1, 0), 0, 0),
                               memory_space=pltpu.SMEM),
                  pl.BlockSpec((None, 1, tm), lambda w, j, we, wt, wlo, whi, wf, fl, fh: (wt[w], 0, 0),
                               memory_space=pltpu.SMEM),
                  pl.BlockSpec((None, 1, tm),
                               lambda w, j, we, wt, wlo, whi, wf, fl, fh: (jnp.minimum(wt[w] + 1, n_tiles - 1), 0, 0),
                               memory_space=pltpu.SMEM),
                  pl.BlockSpec(memory_space=pl.ANY), pl.BlockSpec(memory_space=pl.ANY),
                  pl.BlockSpec((None, d, fc), lambda w, j, we, wt, wlo, whi, wf, fl, fh: (we[w], 0, jj(w, j, wlo, whi))),
                  pl.BlockSpec((None, d, fc), lambda w, j, we, wt, wlo, whi, wf, fl, fh: (we[w], 0, jj(w, j, wlo, whi))),
                  pl.BlockSpec((None, fc, d), lambda w, j, we, wt, wlo, whi, wf, fl, fh: (we[w], jj(w, j, wlo, whi), 0))],
        out_specs=pl.BlockSpec(memory_space=pl.ANY),
        scratch_shapes=[pltpu.VMEM((2, tm, d), F32), pltpu.VMEM((2, tm, d), F32), pltpu.VMEM((tm, d), BF16),
                        pltpu.VMEM((d, fc), BF16), pltpu.VMEM((d, fc), BF16), pltpu.VMEM((fc, d), BF16),
                        pltpu.SemaphoreType.DMA((2,)), pltpu.SemaphoreType.DMA((2,)), pltpu.SemaphoreType.DMA(())])
    return pl.pallas_call(
        functools.partial(_moe_kernel, sub, n_fc),
        out_shape=jax.ShapeDtypeStruct((2, n_tiles * tm // 2, d), F32),
        grid_spec=grid_spec,
        compiler_params=_params(("arbitrary", "arbitrary")),
        name="moe",
    )(*work, codes, codes, codes, h3, h3b, wg, wu, wd)


def _moe_work_items(counts, counts2, tm, n_tiles):
    n_e = counts.shape[0]
    n_work = n_tiles + n_e - 1
    uend = jnp.cumsum(counts)
    ustart = uend - counts
    first_tile = ustart // tm
    n_w = jnp.where(counts > 0, (uend - 1) // tm - first_tile + 1, 0)
    wend = jnp.cumsum(n_w)
    wstart = wend - n_w
    total = wend[-1]
    idx = jnp.arange(n_work, dtype=jnp.int32)
    w = jnp.minimum(idx, total - 1)
    we = jnp.sum((w[:, None] >= wend[None, :]).astype(jnp.int32), axis=1)
    wt = first_tile[we] + (w - wstart[we])
    lo = jnp.clip(ustart[we] - wt * tm, 0, tm)
    hi = jnp.clip(uend[we] - wt * tm, 0, tm)
    live = idx < total
    hi = jnp.where(live, hi, lo)
    prev_t = jnp.concatenate([jnp.full((1,), -1, jnp.int32), wt[:-1]])
    next_t = jnp.concatenate([wt[1:], jnp.full((1,), -1, jnp.int32)])
    final = idx == total - 1
    flags = (jnp.where(live & (wt != prev_t), FLAG_FIRST, 0)
             + jnp.where(live & ((wt != next_t) | final), FLAG_LAST, 0)
             + jnp.where(final, FLAG_FINAL, 0))
    fix_lo = jnp.clip(uend[we] - counts2[we] - wt * tm, lo, hi)
    as_i32 = lambda a: a.astype(jnp.int32)
    return ustart, (as_i32(we), as_i32(wt), as_i32(lo), as_i32(hi), as_i32(flags), as_i32(fix_lo), as_i32(hi))


def _combine_kernel(final, y0_ref, y1_ref, mf_ref, x3_ref, gate_ref, gout_ref, o_ref):
    mf = mf_ref[...]
    moe = mf[:, 0:1] * y0_ref[...] + mf[:, 1:2] * y1_ref[...]
    x4 = x3_ref[...] + gate_ref[...] * moe
    o_ref[...] = _rmsnorm(x4, gout_ref[...]) if final else x4


def _combine(grp, yk, row0, mf, x3, mod3, layer, gout, final):
    m, d = x3.shape
    nt = grp.grid[1]
    assert row0 % grp.tile == 0
    t0 = row0 // grp.tile
    choice = lambda k: pl.BlockSpec((None, grp.tile, d), lambda b, j: (k, t0 + b * nt + j, 0))
    return pl.pallas_call(
        functools.partial(_combine_kernel, final),
        out_shape=jax.ShapeDtypeStruct((m, d), F32),
        grid=grp.grid,
        in_specs=[choice(0), choice(1), grp.row_spec(LANES), grp.row_spec(d), grp.mod_spec(layer, 5, d),
                  _small((1, d))],
        out_specs=grp.row_spec(d),
        compiler_params=_params(("arbitrary", "arbitrary")),
        name="combine",
    )(yk, yk, mf, x3, grp.mod_arg(mod3), gout)


def _moe_layer(groups, x3s, h3s, mis, mfs, cnt1, cnt, mod3, layer, wg, wu, wd, gout, final):
    m1, d = h3s[0].shape
    m = m1 + h3s[1].shape[0]
    n_e = wg.shape[0]
    tm = min(MOE_TILE, m1)
    n_tiles = pl.cdiv(2 * m, tm)
    counts = cnt[0, :n_e].astype(jnp.int32)
    counts2 = counts - cnt1[0, :n_e].astype(jnp.int32)
    ustart, work = _moe_work_items(counts, counts2, tm, n_tiles)

    def positions(mi):
        e, pos = mi[0:2], mi[2:4]
        for k in range(n_e):
            pos = pos + jnp.where(e == k, ustart[k], 0)
        return pos

    pos = jnp.concatenate([positions(mi) for mi in mis], axis=1)
    codes = _invert(pos.reshape(2 * m), n_tiles * tm).reshape(n_tiles, 1, tm)

    yk = _moe(h3s[0], h3s[1], codes, work, wg, wu, wd, tm)
    return [_combine(g, yk, r0, mf, x3, mod3, layer, gout, final)
            for g, r0, mf, x3 in zip(groups, (0, m1), mfs, x3s)]


def _layers(g_p, g_s, x_p, x_s, mod3, st_a, st_b, st_c, p):
    depth = p['w_ada'].shape[0]
    d = x_p.shape[1]
    new = {k: [] for k in ('pa', 'sa', 'pb', 'sb', 'pc', 'sc')}
    for i in range(depth):
        j = i // 2
        gmix = p['norm_mix'][i].reshape(1, d)
        gffn = p['norm_ffn'][i].reshape(1, d)
        if i % 2 == 0:
            conv_w = (p['w_conv_a'][j], p['w_conv_b'][j], p['b_conv_b'][j].reshape(1, -1),
                      p['ln_b_g'][j].reshape(1, -1), p['ln_b_b'][j].reshape(1, -1), p['w_out'][j])
            ffn_w = (p['w_ffn_gate'][j], p['w_ffn_up'][j], p['w_ffn_down'][j])
            u_p = _in_proj(g_p, x_p, mod3, i, gmix, p['w_in'][j])
            x_p, na, nb = _conv_seq(g_p, u_p, x_p, mod3, i, *conv_w)
            new['pa'].append(na)
            new['pb'].append(nb)
            x_p = _ffn(g_p, x_p, mod3, i, gffn, *ffn_w)
            u_s = _in_proj(g_s, x_s, mod3, i, gmix, p['w_in'][j])
            x_s, na, nb = _conv_step(g_s, u_s, st_a[j], st_b[j], x_s, mod3, i, *conv_w)
            new['sa'].append(na)
            new['sb'].append(nb)
            x_s = _ffn(g_s, x_s, mod3, i, gffn, *ffn_w)
        else:
            ctx = p['pool_ctx']
            pool_w = (p['w_pool'][j], p['pool_scale'][j].reshape(1, d), p['wr_hi'][j], p['wr_lo'][j], p['br'][j])
            x3_p, h3_p, mi_p, mf_p, nc, cnt1 = _pool_seq(g_p, x_p, mod3, i, gmix, gffn, *pool_w, ctx)
            new['pc'].append(nc)
            x3_s, h3_s, mi_s, mf_s, nc, cnt = _pool_step(g_s, x_s, st_c[j], cnt1, mod3, i, gmix, gffn, *pool_w, ctx)
            new['sc'].append(nc)
            x_p, x_s = _moe_layer((g_p, g_s), (x3_p, x3_s), (h3_p, h3_s), (mi_p, mi_s), (mf_p, mf_s), cnt1, cnt,
                                  mod3, i, p['w_exp_gate'][j], p['w_exp_up'][j], p['w_exp_down'][j],
                                  p['norm_out'].reshape(1, d), i == depth - 1)
    return x_p, x_s, new


def kernel(x_prompt, x_sample, state_a, state_b, state_c, c_prompt, c_sample, w_ada, b_ada, norm_mix, norm_ffn, norm_out, w_in, w_conv_a, w_conv_b, b_conv_b, ln_b_g, ln_b_b, w_out, w_ffn_gate, w_ffn_up, w_ffn_down, w_pool, pool_scale, w_router, b_router, w_exp_gate, w_exp_up, w_exp_down):
    n_p, seq, d = x_prompt.shape
    n_s, dec_seq, _ = x_sample.shape
    depth = w_ada.shape[0]
    n_e = w_router.shape[2]
    assert dec_seq == 1 and depth % 2 == 0 and n_s % SUBLANES == 0 and n_e <= LANES
    n_even, n_odd = state_a.shape[0], state_c.shape[0]
    ctx = state_c.shape[2]

    wr = jnp.pad(w_router, ((0, 0), (0, 0), (0, LANES - n_e)))
    wr_hi = wr.astype(BF16)
    wr_lo = (wr - wr_hi.astype(F32)).astype(BF16)
    br = jnp.pad(b_router, ((0, 0), (0, LANES - n_e)), constant_values=NEG_BIG).reshape(n_odd, 1, LANES)

    p = {
        'w_ada': w_ada, 'norm_mix': norm_mix, 'norm_ffn': norm_ffn, 'norm_out': norm_out,
        'w_in': w_in.astype(BF16), 'w_conv_a': w_conv_a, 'w_conv_b': w_conv_b, 'b_conv_b': b_conv_b,
        'ln_b_g': ln_b_g, 'ln_b_b': ln_b_b, 'w_out': w_out.astype(BF16),
        'w_ffn_gate': w_ffn_gate.astype(BF16), 'w_ffn_up': w_ffn_up.astype(BF16),
        'w_ffn_down': w_ffn_down.astype(BF16),
        'w_pool': w_pool.astype(BF16), 'pool_scale': pool_scale, 'wr_hi': wr_hi, 'wr_lo': wr_lo, 'br': br,
        'w_exp_gate': w_exp_gate, 'w_exp_up': w_exp_up, 'w_exp_down': w_exp_down, 'pool_ctx': ctx,
    }

    mod3 = _ada(jnp.concatenate([c_sample, c_prompt], axis=0), w_ada, b_ada)

    g_prompt = _Group(n_p, seq, False, n_s, n_s)
    g_sample = _Group(n_s, 1, True, 0, n_s)

    y_p, y_s, new = _layers(
        g_prompt, g_sample, x_prompt.reshape(n_p * seq, d), x_sample.reshape(n_s, d), mod3,
        state_a.reshape(n_even, n_s, -1), state_b.reshape(n_even, n_s, -1), state_c.reshape(n_odd, n_s, -1), p)

    sa = [a.reshape(n_s, state_a.shape[2], -1) for a in new['sa']]
    sb = [b.reshape(n_s, state_b.shape[2], -1) for b in new['sb']]
    sc = [c.reshape(n_s, ctx, d) for c in new['sc']]
    return (y_p.reshape(n_p, seq, d), y_s.reshape(n_s, 1, d),
            jnp.stack(new['pa']), jnp.stack(sa), jnp.stack(new['pb']), jnp.stack(sb),
            jnp.stack(new['pc']), jnp.stack(sc))
```

```python
import functools

import jax
import jax.numpy as jnp
from jax import lax
from jax.experimental import pallas as pl
from jax.experimental.pallas import tpu as pltpu

EPS = 1e-6
PAST_LEN = 16384
POOL_WINDOWS = (2, 4, 8, 16)
N_MOD = 6
LANES = 128
SUBLANES = 8
VMEM_LIMIT = 56 * 1024 * 1024
NEG_BIG = -1e30

ROW_TILE = 512
MOE_TILE = 1024
MOE_SUB = 256
MOE_FF = 512

F32 = jnp.float32
BF16 = jnp.bfloat16


def _params(sem):
    return pltpu.CompilerParams(dimension_semantics=sem, vmem_limit_bytes=VMEM_LIMIT)


def _silu(x):
    return x * jax.nn.sigmoid(x)


def _rmsnorm(x, g):
    return x * lax.rsqrt(jnp.mean(x * x, axis=-1, keepdims=True) + EPS) * g


def _mod_rmsnorm(x, g, sc, sh):
    return _rmsnorm(x, g) * (1.0 + sc) + sh


def _resident(shape):
    nd = len(shape)
    return pl.BlockSpec(shape, lambda *_: (0,) * nd, pipeline_mode=pl.Buffered(1))


def _small(shape):
    nd = len(shape)
    return pl.BlockSpec(shape, lambda *_: (0,) * nd)


class _Group:
    def __init__(self, n_seq, seq_len, per_row_mod, mod_row0, n_sample):
        self.n_seq, self.seq_len = n_seq, seq_len
        self.per_row_mod = per_row_mod
        self.mod_row0 = mod_row0
        self.n_sample = n_sample
        if per_row_mod:
            self.tile = n_seq
            self.grid = (1, 1)
        else:
            self.tile = min(ROW_TILE, seq_len)
            assert seq_len % self.tile == 0 and self.tile % 32 == 0
            self.grid = (n_seq, seq_len // self.tile)
        self.rows = n_seq * seq_len

    def row_spec(self, width):
        nt = self.grid[1]
        return pl.BlockSpec((self.tile, width), lambda b, j, *_: (b * nt + j, 0))

    def mod_spec(self, layer, chunk, d):
        if self.per_row_mod:
            return pl.BlockSpec((None, self.n_sample, d), lambda b, j, *_: (layer, 0, chunk))
        r0 = self.mod_row0
        return pl.BlockSpec((None, None, 1, d), lambda b, j, *_: (layer, r0 + b, 0, chunk))

    def mod_arg(self, mod3):
        if self.per_row_mod:
            return mod3
        l, r, w = mod3.shape
        return mod3.reshape(l, r, 1, w)


def _ada_kernel(c_ref, w_ref, b_ref, o_ref):
    cs = _silu(c_ref[...]).astype(BF16)
    o_ref[...] = jnp.dot(cs, w_ref[...].astype(BF16), preferred_element_type=F32) + b_ref[...]


def _ada(c_all, w_ada, b_ada):
    depth, d, w6 = w_ada.shape
    r = c_all.shape[0]
    tn = w6 // 4
    return pl.pallas_call(
        _ada_kernel,
        out_shape=jax.ShapeDtypeStruct((depth, r, w6), F32),
        grid=(depth, w6 // tn),
        in_specs=[pl.BlockSpec((r, d), lambda l, n: (0, 0)),
                  pl.BlockSpec((None, d, tn), lambda l, n: (l, 0, n)),
                  pl.BlockSpec((None, 1, tn), lambda l, n: (l, 0, n))],
        out_specs=pl.BlockSpec((None, r, tn), lambda l, n: (l, 0, n)),
        compiler_params=_params(("arbitrary", "arbitrary")),
        name="ada",
    )(c_all, w_ada, b_ada.reshape(depth, 1, w6))


def _in_proj_kernel(x_ref, g_ref, sc_ref, sh_ref, w_ref, u_ref):
    h = _mod_rmsnorm(x_ref[...], g_ref[...], sc_ref[...], sh_ref[...])
    u_ref[...] = jnp.dot(h.astype(BF16), w_ref[...], preferred_element_type=F32)


def _in_proj(grp, x, mod3, layer, g, w_bf):
    d, n = w_bf.shape
    return pl.pallas_call(
        _in_proj_kernel,
        out_shape=jax.ShapeDtypeStruct((grp.rows, n), F32),
        grid=grp.grid,
        in_specs=[grp.row_spec(d), _small((1, d)),
                  grp.mod_spec(layer, 1, d), grp.mod_spec(layer, 0, d),
                  _resident((d, n))],
        out_specs=grp.row_spec(n),
        compiler_params=_params(("arbitrary", "arbitrary")),
        name="in_proj",
    )(x, g, grp.mod_arg(mod3), grp.mod_arg(mod3), w_bf)


def _layernorm(y, g, b):
    mu = jnp.mean(y, axis=-1, keepdims=True)
    yc = y - mu
    var = jnp.mean(yc * yc, axis=-1, keepdims=True)
    return yc * lax.rsqrt(var + EPS) * g + b


def _causal_taps(pad_ref, w_ref, k_taps, halo, tt):
    off0 = halo - (k_taps - 1)
    y = None
    for c in range(SUBLANES):
        n = tt + (SUBLANES if c else 0)
        q = None
        for k in range(k_taps):
            if (off0 + k) % SUBLANES == c:
                a = off0 + k - c
                term = w_ref[k:k + 1, :] * pad_ref[a:a + n, :]
                q = term if q is None else q + term
        if q is not None:
            part = q[c:c + tt, :]
            y = part if y is None else y + part
    return y


def _conv_seq_kernel(n_t, ka, kb, u_ref, uh_ref, x_ref, gm_ref, wa_ref, wb_ref, bb_ref,
                     lng_ref, lnb_ref, wout_ref, x1_ref, na_ref, nb_ref, apad, gpad):
    j = pl.program_id(1)
    tt = u_ref.shape[0]
    da = wa_ref.shape[1]
    hb = uh_ref.shape[0]
    keep = (j > 0).astype(F32)

    u = u_ref[...]
    a_b, a_c, a_x = u[:, 0:da], u[:, da:2 * da], u[:, 2 * da:3 * da]
    b_v, b_g = u[:, 3 * da:4 * da], u[:, 4 * da:5 * da]
    uh = uh_ref[...]
    apad[0:SUBLANES, :] = keep * (uh[hb - SUBLANES:hb, da:2 * da] * uh[hb - SUBLANES:hb, 2 * da:3 * da])
    apad[SUBLANES:SUBLANES + tt, :] = a_c * a_x
    gpad[0:hb, :] = keep * (uh[:, 3 * da:4 * da] * jax.nn.sigmoid(uh[:, 4 * da:5 * da]))
    gpad[hb:hb + tt, :] = b_v * jax.nn.sigmoid(b_g)

    zero_tail = jnp.zeros((SUBLANES, da), F32)
    apad[SUBLANES + tt:SUBLANES + tt + SUBLANES, :] = zero_tail
    gpad[hb + tt:hb + tt + SUBLANES, :] = zero_tail

    ya = a_b * _causal_taps(apad, wa_ref, ka, SUBLANES, tt)
    yb = _causal_taps(gpad, wb_ref, kb, hb, tt)
    yb = _silu(_layernorm(yb + bb_ref[...], lng_ref[...], lnb_ref[...]))

    ycat = jnp.concatenate([ya, yb], axis=-1).astype(BF16)
    y = jnp.dot(ycat, wout_ref[...], preferred_element_type=F32)
    x1_ref[...] = x_ref[...] + gm_ref[...] * y

    @pl.when(j == n_t - 1)
    def _():
        na_ref[...] = apad[SUBLANES + tt - (ka - 1):SUBLANES + tt, :]
        nb_ref[...] = gpad[hb + tt - (kb - 1):hb + tt, :]


def _conv_seq(grp, u, x, mod3, layer, wa, wb, bb, lng, lnb, wout_bf):
    tt = grp.tile
    n_b, n_t = grp.grid
    d = x.shape[1]
    ka, da = wa.shape
    kb = wb.shape[0]
    hb = 32
    assert kb - 1 <= hb and ka - 1 <= SUBLANES and tt % hb == 0
    per = tt // hb
    halo = pl.BlockSpec((hb, u.shape[1]), lambda b, j: (jnp.maximum((b * n_t + j) * per - 1, 0), 0))
    return pl.pallas_call(
        functools.partial(_conv_seq_kernel, n_t, ka, kb),
        out_shape=(jax.ShapeDtypeStruct((grp.rows, d), F32),
                   jax.ShapeDtypeStruct((n_b, ka - 1, da), F32),
                   jax.ShapeDtypeStruct((n_b, kb - 1, da), F32)),
        grid=grp.grid,
        in_specs=[grp.row_spec(u.shape[1]), halo, grp.row_spec(d), grp.mod_spec(layer, 2, d),
                  _small(wa.shape), _small(wb.shape), _small((1, da)), _small((1, da)), _small((1, da)),
                  _resident(wout_bf.shape)],
        out_specs=(grp.row_spec(d),
                   pl.BlockSpec((None, ka - 1, da), lambda b, j: (b, 0, 0)),
                   pl.BlockSpec((None, kb - 1, da), lambda b, j: (b, 0, 0))),
        scratch_shapes=[pltpu.VMEM((2 * SUBLANES + tt, da), F32), pltpu.VMEM((hb + tt + SUBLANES, da), F32)],
        compiler_params=_params(("arbitrary", "arbitrary")),
        name="conv_seq",
    )(u, u, x, grp.mod_arg(mod3), wa, wb, bb, lng, lnb, wout_bf)


def _conv_step_kernel(ka, kb, u_ref, sa_ref, sb_ref, x_ref, gm_ref, wa_ref, wb_ref, bb_ref,
                      lng_ref, lnb_ref, wout_ref, x1_ref, na_ref, nb_ref):
    da = wa_ref.shape[1]
    u = u_ref[...]
    a_b, a_c, a_x = u[:, 0:da], u[:, da:2 * da], u[:, 2 * da:3 * da]
    b_v, b_g = u[:, 3 * da:4 * da], u[:, 4 * da:5 * da]

    cur = a_c * a_x
    ya = wa_ref[ka - 1:ka, :] * cur
    for k in range(ka - 1):
        ya = ya + wa_ref[k:k + 1, :] * sa_ref[:, k * da:(k + 1) * da]
    ya = a_b * ya
    glu = b_v * jax.nn.sigmoid(b_g)
    yb = wb_ref[kb - 1:kb, :] * glu
    for k in range(kb - 1):
        yb = yb + wb_ref[k:k + 1, :] * sb_ref[:, k * da:(k + 1) * da]
    yb = _silu(_layernorm(yb + bb_ref[...], lng_ref[...], lnb_ref[...]))

    ycat = jnp.concatenate([ya, yb], axis=-1).astype(BF16)
    y = jnp.dot(ycat, wout_ref[...], preferred_element_type=F32)
    x1_ref[...] = x_ref[...] + gm_ref[...] * y

    if ka > 2:
        na_ref[:, 0:(ka - 2) * da] = sa_ref[:, da:(ka - 1) * da]
    na_ref[:, (ka - 2) * da:(ka - 1) * da] = cur
    if kb > 2:
        nb_ref[:, 0:(kb - 2) * da] = sb_ref[:, da:(kb - 1) * da]
    nb_ref[:, (kb - 2) * da:(kb - 1) * da] = glu


def _conv_step(grp, u, sa2, sb2, x, mod3, layer, wa, wb, bb, lng, lnb, wout_bf):
    n = grp.rows
    d = x.shape[1]
    ka, da = wa.shape
    kb = wb.shape[0]
    return pl.pallas_call(
        functools.partial(_conv_step_kernel, ka, kb),
        out_shape=(jax.ShapeDtypeStruct((n, d), F32),
                   jax.ShapeDtypeStruct(sa2.shape, F32),
                   jax.ShapeDtypeStruct(sb2.shape, F32)),
        grid=grp.grid,
        in_specs=[_small(u.shape), _small(sa2.shape), _small(sb2.shape), _small(x.shape),
                  grp.mod_spec(layer, 2, d),
                  _small(wa.shape), _small(wb.shape), _small((1, da)), _small((1, da)), _small((1, da)),
                  _small(wout_bf.shape)],
        out_specs=(_small((n, d)), _small(sa2.shape), _small(sb2.shape)),
        compiler_params=_params(("arbitrary", "arbitrary")),
        name="conv_step",
    )(u, sa2, sb2, x, grp.mod_arg(mod3), wa, wb, bb, lng, lnb, wout_bf)


def _ffn_kernel(x_ref, g_ref, sc_ref, sh_ref, gate_ref, wg_ref, wu_ref, wd_ref, o_ref):
    x = x_ref[...]
    h = _mod_rmsnorm(x, g_ref[...], sc_ref[...], sh_ref[...]).astype(BF16)
    a = jnp.dot(h, wg_ref[...], preferred_element_type=F32)
    b = jnp.dot(h, wu_ref[...], preferred_element_type=F32)
    act = (_silu(a) * b).astype(BF16)
    f = jnp.dot(act, wd_ref[...], preferred_element_type=F32)
    o_ref[...] = x + gate_ref[...] * f


def _ffn(grp, x, mod3, layer, g, wg_bf, wu_bf, wd_bf):
    d = x.shape[1]
    return pl.pallas_call(
        _ffn_kernel,
        out_shape=jax.ShapeDtypeStruct(x.shape, F32),
        grid=grp.grid,
        in_specs=[grp.row_spec(d), _small((1, d)),
                  grp.mod_spec(layer, 4, d), grp.mod_spec(layer, 3, d), grp.mod_spec(layer, 5, d),
                  _resident(wg_bf.shape), _resident(wu_bf.shape), _resident(wd_bf.shape)],
        out_specs=grp.row_spec(d),
        compiler_params=_params(("arbitrary", "arbitrary")),
        name="ffn",
    )(x, g, grp.mod_arg(mod3), grp.mod_arg(mod3), grp.mod_arg(mod3), wg_bf, wu_bf, wd_bf)


def _window_sum(v, win):
    assert win & (win - 1) == 0
    span = 1
    while span < win:
        v = v + pltpu.roll(v, span, 0)
        span *= 2
    return v


def _pool_project(diff, wp_ref):
    n_g, pg, _ = wp_ref.shape
    outs = [jnp.dot(diff[:, gi * pg:(gi + 1) * pg].astype(BF16), wp_ref[gi], preferred_element_type=F32)
            for gi in range(n_g)]
    return jnp.concatenate(outs, axis=-1)


def _route(h3, wrh_ref, wrl_ref, br_ref, tri_ref, base_ref, mi_ref, mf_ref, cnt_ref):
    tt = h3.shape[0]
    h_hi = h3.astype(BF16)
    h_lo = (h3 - h_hi.astype(F32)).astype(BF16)
    logits = (jnp.dot(h_hi, wrh_ref[...], preferred_element_type=F32)
              + jnp.dot(h_lo, wrh_ref[...], preferred_element_type=F32)
              + jnp.dot(h_hi, wrl_ref[...], preferred_element_type=F32)
              + br_ref[...])
    lane = lax.broadcasted_iota(jnp.int32, (tt, LANES), 1).astype(F32)
    m0 = jnp.max(logits, axis=-1, keepdims=True)
    e0 = jnp.min(jnp.where(logits == m0, lane, float(LANES)), axis=-1, keepdims=True)
    rest = jnp.where(lane == e0, NEG_BIG * 2.0, logits)
    m1 = jnp.max(rest, axis=-1, keepdims=True)
    e1 = jnp.min(jnp.where(rest == m1, lane, float(LANES)), axis=-1, keepdims=True)
    dd = jnp.exp(m1 - m0)
    p0 = 1.0 / (1.0 + dd)
    p1 = dd * p0

    oh0 = lane == e0
    oh1 = lane == e1
    c = jnp.where(oh0 | oh1, 1.0, 0.0)
    prefix = jnp.dot(tri_ref[...], c.astype(BF16), preferred_element_type=F32)
    tot = base_ref[...] + prefix
    r0 = jnp.sum(jnp.where(oh0, tot, 0.0), axis=-1, keepdims=True)
    r1 = jnp.sum(jnp.where(oh1, tot, 0.0), axis=-1, keepdims=True)
    new_base = base_ref[...] + jnp.sum(c, axis=0, keepdims=True)
    base_ref[...] = new_base
    cnt_ref[...] = jnp.broadcast_to(new_base, cnt_ref.shape)

    meta = jnp.where(lane == 0.0, e0, jnp.where(lane == 1.0, e1,
                     jnp.where(lane == 2.0, r0, jnp.where(lane == 3.0, r1, 0.0))))
    mi_ref[...] = meta.T[0:SUBLANES, :].astype(jnp.int32)
    mf_ref[...] = jnp.where(lane == 0.0, p0, jnp.where(lane == 1.0, p1, 0.0))


def _init_route_scratch(first, tri_ref, base_ref, base0=None):
    @pl.when(first)
    def _():
        tt = tri_ref.shape[0]
        r = lax.broadcasted_iota(jnp.int32, (tt, tt), 0)
        c = lax.broadcasted_iota(jnp.int32, (tt, tt), 1)
        tri_ref[...] = jnp.where(c < r, 1.0, 0.0).astype(BF16)
        base_ref[...] = jnp.zeros(base_ref.shape, F32) if base0 is None else base0


def _pool_seq_kernel(n_t, ctx, x_ref, xh_ref, gmix_ref, shm_ref, scm_ref, gm_ref, gffn_ref, shf_ref, scf_ref,
                     wp_ref, ps_ref, wrh_ref, wrl_ref, br_ref,
                     x3_ref, h3_ref, mi_ref, mf_ref, nc_ref, cnt_ref, hpad, tri_ref, base_ref):
    b = pl.program_id(0)
    j = pl.program_id(1)
    tt = x_ref.shape[0]
    hb = xh_ref.shape[0]
    n_g, pg, _ = wp_ref.shape
    _init_route_scratch((b == 0) & (j == 0), tri_ref, base_ref)

    keep = (j > 0).astype(F32)
    x = x_ref[...]
    h = _mod_rmsnorm(x, gmix_ref[...], scm_ref[...], shm_ref[...])
    hpad[0:hb, :] = keep * _mod_rmsnorm(xh_ref[...], gmix_ref[...], scm_ref[...], shm_ref[...])
    hpad[hb:hb + tt, :] = h

    pos = lax.broadcasted_iota(jnp.int32, (tt, pg), 0) + j * tt
    groups = []
    for gi, win in enumerate(POOL_WINDOWS):
        sl = slice(gi * pg, (gi + 1) * pg)
        s = _window_sum(hpad[:, sl], win)[hb:hb + tt, :]
        cnt = jnp.minimum(pos + 1, win).astype(F32)
        groups.append(s / cnt - h[:, sl])
    diff = jnp.concatenate(groups, axis=-1)
    y = _pool_project(diff, wp_ref) * ps_ref[...]
    x3 = x + gm_ref[...] * y
    x3_ref[...] = x3
    h3 = _mod_rmsnorm(x3, gffn_ref[...], scf_ref[...], shf_ref[...])
    h3_ref[...] = h3
    _route(h3, wrh_ref, wrl_ref, br_ref, tri_ref, base_ref, mi_ref, mf_ref, cnt_ref)

    @pl.when(j == n_t - 1)
    def _():
        nc_ref[...] = hpad[hb + tt - ctx:hb + tt, :]


def _route_out_shapes(rows, d):
    return (jax.ShapeDtypeStruct((rows, d), F32), jax.ShapeDtypeStruct((rows, d), F32),
            jax.ShapeDtypeStruct((SUBLANES, rows), jnp.int32), jax.ShapeDtypeStruct((rows, LANES), F32))


def _pool_seq(grp, x, mod3, layer, gmix, gffn, wp_bf, ps, wrh, wrl, br, ctx):
    tt = grp.tile
    n_b, n_t = grp.grid
    d = x.shape[1]
    hb = 16
    assert ctx <= hb and max(POOL_WINDOWS) - 1 <= hb and tt % hb == 0
    per = tt // hb
    halo = pl.BlockSpec((hb, d), lambda b, j: (jnp.maximum((b * n_t + j) * per - 1, 0), 0))
    ms = lambda c: grp.mod_spec(layer, c, d)
    return pl.pallas_call(
        functools.partial(_pool_seq_kernel, n_t, ctx),
        out_shape=_route_out_shapes(grp.rows, d) + (
            jax.ShapeDtypeStruct((n_b, ctx, d), F32), jax.ShapeDtypeStruct((SUBLANES, LANES), F32)),
        grid=grp.grid,
        in_specs=[grp.row_spec(d), halo, _small((1, d)), ms(0), ms(1), ms(2), _small((1, d)), ms(3), ms(4),
                  _small(wp_bf.shape), _small((1, d)), _small(wrh.shape), _small(wrl.shape), _small(br.shape)],
        out_specs=(grp.row_spec(d), grp.row_spec(d), pl.BlockSpec((SUBLANES, tt), lambda b, j: (0, b * n_t + j)),
                   grp.row_spec(LANES), pl.BlockSpec((None, ctx, d), lambda b, j: (b, 0, 0)), _small((SUBLANES, LANES))),
        scratch_shapes=[pltpu.VMEM((hb + tt, d), F32), pltpu.VMEM((tt, tt), BF16), pltpu.VMEM((1, LANES), F32)],
        compiler_params=_params(("arbitrary", "arbitrary")),
        name="pool_seq",
    )(x, x, gmix, *([grp.mod_arg(mod3)] * 3), gffn, *([grp.mod_arg(mod3)] * 2), wp_bf, ps, wrh, wrl, br)


def _pool_step_kernel(ctx, x_ref, sc_ref_state, cnt0_ref, gmix_ref, shm_ref, scm_ref, gm_ref, gffn_ref, shf_ref,
                      scf_ref, wp_ref, ps_ref, wrh_ref, wrl_ref, br_ref,
                      x3_ref, h3_ref, mi_ref, mf_ref, nc_ref, cnt_ref, tri_ref, base_ref):
    d = x_ref.shape[1]
    n_g, pg, _ = wp_ref.shape
    _init_route_scratch(pl.program_id(0) == 0, tri_ref, base_ref, cnt0_ref[0:1, :])

    x = x_ref[...]
    h = _mod_rmsnorm(x, gmix_ref[...], scm_ref[...], shm_ref[...])
    groups = []
    for gi, win in enumerate(POOL_WINDOWS):
        s = h[:, gi * pg:(gi + 1) * pg]
        for i in range(1, win):
            row = ctx - i
            s = s + sc_ref_state[:, row * d + gi * pg:row * d + (gi + 1) * pg]
        cnt = float(min(PAST_LEN + 1, win))
        groups.append(s / cnt - h[:, gi * pg:(gi + 1) * pg])
    diff = jnp.concatenate(groups, axis=-1)
    y = _pool_project(diff, wp_ref) * ps_ref[...]
    x3 = x + gm_ref[...] * y
    x3_ref[...] = x3
    h3 = _mod_rmsnorm(x3, gffn_ref[...], scf_ref[...], shf_ref[...])
    h3_ref[...] = h3
    _route(h3, wrh_ref, wrl_ref, br_ref, tri_ref, base_ref, mi_ref, mf_ref, cnt_ref)

    if ctx > 1:
        nc_ref[:, 0:(ctx - 1) * d] = sc_ref_state[:, d:ctx * d]
    nc_ref[:, (ctx - 1) * d:ctx * d] = h


def _pool_step(grp, x, sc2, cnt0, mod3, layer, gmix, gffn, wp_bf, ps, wrh, wrl, br, ctx):
    n, d = x.shape
    ms = lambda c: grp.mod_spec(layer, c, d)
    return pl.pallas_call(
        functools.partial(_pool_step_kernel, ctx),
        out_shape=_route_out_shapes(n, d) + (
            jax.ShapeDtypeStruct(sc2.shape, F32), jax.ShapeDtypeStruct((SUBLANES, LANES), F32)),
        grid=grp.grid,
        in_specs=[_small(x.shape), _small(sc2.shape), _small(cnt0.shape), _small((1, d)), ms(0), ms(1), ms(2),
                  _small((1, d)), ms(3), ms(4),
                  _small(wp_bf.shape), _small((1, d)), _small(wrh.shape), _small(wrl.shape), _small(br.shape)],
        out_specs=(_small((n, d)), _small((n, d)), _small((SUBLANES, n)), _small((n, LANES)),
                   _small(sc2.shape), _small((SUBLANES, LANES))),
        scratch_shapes=[pltpu.VMEM((n, n), BF16), pltpu.VMEM((1, LANES), F32)],
        compiler_params=_params(("arbitrary", "arbitrary")),
        name="pool_step",
    )(x, sc2, cnt0, gmix, *([grp.mod_arg(mod3)] * 3), gffn, *([grp.mod_arg(mod3)] * 2), wp_bf, ps, wrh, wrl, br)


def _invert_kernel(pos_ref, code_ref):
    n_real = pos_ref.shape[0]
    m = n_real // 2

    def body(i, carry):
        code_ref[pos_ref[i]] = 2 * i
        code_ref[pos_ref[m + i]] = 2 * i + 1
        return carry

    lax.fori_loop(0, m, body, 0, unroll=8)

    def pad(s, carry):
        code_ref[s] = s
        return carry

    lax.fori_loop(n_real, code_ref.shape[0], pad, 0)


def _invert(pos_flat, n_slots):
    smem = pl.BlockSpec(memory_space=pltpu.SMEM)
    return pl.pallas_call(
        _invert_kernel,
        out_shape=jax.ShapeDtypeStruct((n_slots,), jnp.int32),
        in_specs=[smem], out_specs=smem,
        name="invert",
    )(pos_flat)


FLAG_FIRST, FLAG_LAST, FLAG_FINAL = 1, 2, 4


def _moe_kernel(sub, n_fc, we_ref, wt_ref, wlo_ref, whi_ref, wflag_ref, wfix_lo_ref, wfix_hi_ref,
                code_prev_ref, code_ref, code_next_ref, h_hbm, h2_hbm, wg_ref, wu_ref, wd_ref, yk_hbm,
                xs_buf, acc, xb, wgb, wub, wdb, gsem, ssem, fsem):
    w = pl.program_id(0)
    j = pl.program_id(1)
    tm = xb.shape[0]
    t = wt_ref[w]
    slot = lax.rem(t, 2)
    other = 1 - slot
    lo = wlo_ref[w]
    hi = whi_ref[w]
    flag = wflag_ref[w]
    first = (flag & FLAG_FIRST) != 0
    last = (flag & FLAG_LAST) != 0
    final = (flag & FLAG_FINAL) != 0
    n_share = max(n_fc - 1, 1)
    share = (tm // n_share) // SUBLANES * SUBLANES
    rest = tm - share * n_share

    m1 = h_hbm.shape[0]

    def gather_row(codes, s, r):
        tok = jnp.minimum(lax.shift_right_logical(codes[0, r], 1), m1 - 1)
        return pltpu.make_async_copy(h_hbm.at[pl.ds(tok, 1), :], xs_buf.at[s, pl.ds(r, 1), :], gsem.at[s])

    def second_group_row(r):
        tok = lax.shift_right_logical(code_ref[0, r], 1) - m1
        return pltpu.make_async_copy(h2_hbm.at[pl.ds(tok, 1), :], xs_buf.at[slot, pl.ds(r, 1), :], fsem)

    def refetch_second_group(r0, r1):
        def start(r, carry):
            second_group_row(r).start()
            return carry

        def wait(r, carry):
            second_group_row(r).wait()
            return carry

        lax.fori_loop(r0, r1, start, 0)
        lax.fori_loop(r0, r1, wait, 0)

    def scatter_row(codes, s, r):
        code = codes[0, r]
        dst = yk_hbm.at[code & 1, pl.ds(lax.shift_right_logical(code, 1), 1), :]
        return pltpu.make_async_copy(acc.at[s, pl.ds(r, 1), :], dst, ssem.at[s])

    def wait_gather(s):
        pltpu.make_async_copy(h_hbm.at[pl.ds(0, tm), :], xs_buf.at[s], gsem.at[s]).wait()

    def wait_scatter(s):
        pltpu.make_async_copy(acc.at[s], yk_hbm.at[0, pl.ds(0, tm), :], ssem.at[s]).wait()

    def rows_loop(r0, n, fn):
        def body(i, carry):
            fn(r0 + i)
            return carry

        lax.fori_loop(0, n, body, 0, unroll=SUBLANES)

    def neighbour_share():
        base = pl.multiple_of(j * share, SUBLANES)
        for u in range(share):
            gather_row(code_next_ref, other, base + u).start()
            scatter_row(code_prev_ref, other, base + u).start()

    @pl.when((w == 0) & (j == 0))
    def _():
        rows_loop(0, tm, lambda r: gather_row(code_ref, 0, r).start())
        acc[1] = jnp.zeros((tm, acc.shape[2]), F32)

    @pl.when(first & (j == 0))
    def _():
        wait_gather(slot)
        acc[slot] = jnp.zeros((tm, acc.shape[2]), F32)
        if rest:
            rows_loop(share * n_share, rest, lambda r: gather_row(code_next_ref, other, r).start())
            rows_loop(share * n_share, rest, lambda r: scatter_row(code_prev_ref, other, r).start())

    def swiglu(x, wg, wu, wd):
        a = jnp.dot(x, wg, preferred_element_type=F32)
        b = jnp.dot(x, wu, preferred_element_type=F32)
        return jnp.dot((_silu(a) * b).astype(BF16), wd, preferred_element_type=F32)

    n_sub = tm // sub
    touched = lax.div(hi + (sub - 1), sub) - lax.div(lo, sub)
    straight = first & (touched * 4 >= n_sub * 3)

    @pl.when((hi > lo) & (j == 0))
    def _():
        refetch_second_group(wfix_lo_ref[w], wfix_hi_ref[w])
        rows = lax.broadcasted_iota(jnp.int32, xb.shape, 0)
        xb[...] = jnp.where((rows >= lo) & (rows < hi), xs_buf[slot], 0.0).astype(BF16)

    def whole_tile():
        acc[slot] += swiglu(xb[...], wg_ref[...].astype(BF16), wu_ref[...].astype(BF16),
                            wd_ref[...].astype(BF16))

    @pl.when(straight & (j < n_share))
    def _():
        neighbour_share()
        whole_tile()

    @pl.when(straight & (j >= n_share))
    def _():
        whole_tile()

    @pl.when((hi > lo) & jnp.logical_not(straight))
    def _():
        @pl.when(first & (j < n_share))
        def _():
            neighbour_share()

        wgb[...] = wg_ref[...].astype(BF16)
        wub[...] = wu_ref[...].astype(BF16)
        wdb[...] = wd_ref[...].astype(BF16)
        for s in range(n_sub):
            @pl.when((s * sub < hi) & ((s + 1) * sub > lo))
            def _():
                rs = pl.ds(s * sub, sub)
                acc[slot, rs, :] += swiglu(xb[rs, :], wgb[...], wub[...], wdb[...])

    @pl.when(last & (j == n_fc - 1))
    def _():
        wait_scatter(other)

        @pl.when(final)
        def _():
            rows_loop(0, tm, lambda r: scatter_row(code_ref, slot, r).start())
            wait_scatter(slot)
            wait_gather(other)


def _moe(h3, h3b, codes, work, wg, wu, wd, tm):
    d = h3.shape[1]
    n_e, _, ff = wg.shape
    fc = min(MOE_FF, ff)
    assert ff % fc == 0
    n_fc = ff // fc
    sub = min(MOE_SUB, tm)
    n_work = work[0].shape[0]
    n_tiles = codes.shape[0]

    def jj(w, j, wlo, whi):
        return jnp.where(whi[w] > wlo[w], j, n_fc - 1)

    grid_spec = pltpu.PrefetchScalarGridSpec(
        num_scalar_prefetch=7,
        grid=(n_work, n_fc),
        in_specs=[pl.BlockSpec((None, 1, tm), lambda w, j, we, wt, wlo, whi, wf, fl, fh: (jnp.maximum(wt[w] - 1, 0), 0, 0),
                               memory_space=pltpu.SMEM),
                  pl.BlockSpec((None, 1, tm), lambda w, j, we, wt, wlo, whi, wf, fl, fh: (wt[w], 0, 0),
                               memory_space=pltpu.SMEM),
                  pl.BlockSpec((None, 1, tm),
                               lambda w, j, we, wt, wlo, whi, wf, fl, fh: (jnp.minimum(wt[w] + 1, n_tiles - 1), 0, 0),
                               memory_space=pltpu.SMEM),
                  pl.BlockSpec(memory_space=pl.ANY), pl.BlockSpec(memory_space=pl.ANY),
                  pl.BlockSpec((None, d, fc), lambda w, j, we, wt, wlo, whi, wf, fl, fh: (we[w], 0, jj(w, j, wlo, whi))),
                  pl.BlockSpec((None, d, fc), lambda w, j, we, wt, wlo, whi, wf, fl, fh: (we[w], 0, jj(w, j, wlo, whi))),
                  pl.BlockSpec((None, fc, d), lambda w, j, we, wt, wlo, whi, wf, fl, fh: (we[w], jj(w, j, wlo, whi), 0))],
        out_specs=pl.BlockSpec(memory_space=pl.ANY),
        scratch_shapes=[pltpu.VMEM((2, tm, d), F32), pltpu.VMEM((2, tm, d), F32), pltpu.VMEM((tm, d), BF16),
                        pltpu.VMEM((d, fc), BF16), pltpu.VMEM((d, fc), BF16), pltpu.VMEM((fc, d), BF16),
                        pltpu.SemaphoreType.DMA((2,)), pltpu.SemaphoreType.DMA((2,)), pltpu.SemaphoreType.DMA(())])
    return pl.pallas_call(
        functools.partial(_moe_kernel, sub, n_fc),
        out_shape=jax.ShapeDtypeStruct((2, n_tiles * tm // 2, d), F32),
        grid_spec=grid_spec,
        compiler_params=_params(("arbitrary", "arbitrary")),
        name="moe",
    )(*work, codes, codes, codes, h3, h3b, wg, wu, wd)


def _moe_work_items(counts, counts2, tm, n_tiles):
    n_e = counts.shape[0]
    n_work = n_tiles + n_e - 1
    uend = jnp.cumsum(counts)
    ustart = uend - counts
    first_tile = ustart // tm
    n_w = jnp.where(counts > 0, (uend - 1) // tm - first_tile + 1, 0)
    wend = jnp.cumsum(n_w)
    wstart = wend - n_w
    total = wend[-1]
    idx = jnp.arange(n_work, dtype=jnp.int32)
    w = jnp.minimum(idx, total - 1)
    we = jnp.sum((w[:, None] >= wend[None, :]).astype(jnp.int32), axis=1)
    wt = first_tile[we] + (w - wstart[we])
    lo = jnp.clip(ustart[we] - wt * tm, 0, tm)
    hi = jnp.clip(uend[we] - wt * tm, 0, tm)
    live = idx < total
    hi = jnp.where(live, hi, lo)
    prev_t = jnp.concatenate([jnp.full((1,), -1, jnp.int32), wt[:-1]])
    next_t = jnp.concatenate([wt[1:], jnp.full((1,), -1, jnp.int32)])
    final = idx == total - 1
    flags = (jnp.where(live & (wt != prev_t), FLAG_FIRST, 0)
             + jnp.where(live & ((wt != next_t) | final), FLAG_LAST, 0)
             + jnp.where(final, FLAG_FINAL, 0))
    fix_lo = jnp.clip(uend[we] - counts2[we] - wt * tm, lo, hi)
    as_i32 = lambda a: a.astype(jnp.int32)
    return ustart, (as_i32(we), as_i32(wt), as_i32(lo), as_i32(hi), as_i32(flags), as_i32(fix_lo), as_i32(hi))


def _combine_kernel(final, y0_ref, y1_ref, mf_ref, x3_ref, gate_ref, gout_ref, o_ref):
    mf = mf_ref[...]
    moe = mf[:, 0:1] * y0_ref[...] + mf[:, 1:2] * y1_ref[...]
    x4 = x3_ref[...] + gate_ref[...] * moe
    o_ref[...] = _rmsnorm(x4, gout_ref[...]) if final else x4


def _combine(grp, yk, row0, mf, x3, mod3, layer, gout, final):
    m, d = x3.shape
    nt = grp.grid[1]
    assert row0 % grp.tile == 0
    t0 = row0 // grp.tile
    choice = lambda k: pl.BlockSpec((None, grp.tile, d), lambda b, j: (k, t0 + b * nt + j, 0))
    return pl.pallas_call(
        functools.partial(_combine_kernel, final),
        out_shape=jax.ShapeDtypeStruct((m, d), F32),
        grid=grp.grid,
        in_specs=[choice(0), choice(1), grp.row_spec(LANES), grp.row_spec(d), grp.mod_spec(layer, 5, d),
                  _small((1, d))],
        out_specs=grp.row_spec(d),
        compiler_params=_params(("arbitrary", "arbitrary")),
        name="combine",
    )(yk, yk, mf, x3, grp.mod_arg(mod3), gout)


def _moe_layer(groups, x3s, h3s, mis, mfs, cnt1, cnt, mod3, layer, wg, wu, wd, gout, final):
    m1, d = h3s[0].shape
    m = m1 + h3s[1].shape[0]
    n_e = wg.shape[0]
    tm = min(MOE_TILE, m1)
    n_tiles = pl.cdiv(2 * m, tm)
    counts = cnt[0, :n_e].astype(jnp.int32)
    counts2 = counts - cnt1[0, :n_e].astype(jnp.int32)
    ustart, work = _moe_work_items(counts, counts2, tm, n_tiles)

    def positions(mi):
        e, pos = mi[0:2], mi[2:4]
        for k in range(n_e):
            pos = pos + jnp.where(e == k, ustart[k], 0)
        return pos

    pos = jnp.concatenate([positions(mi) for mi in mis], axis=1)
    codes = _invert(pos.reshape(2 * m), n_tiles * tm).reshape(n_tiles, 1, tm)

    yk = _moe(h3s[0], h3s[1], codes, work, wg, wu, wd, tm)
    return [_combine(g, yk, r0, mf, x3, mod3, layer, gout, final)
            for g, r0, mf, x3 in zip(groups, (0, m1), mfs, x3s)]


def _layers(g_p, g_s, x_p, x_s, mod3, st_a, st_b, st_c, p):
    depth = p['w_ada'].shape[0]
    d = x_p.shape[1]
    new = {k: [] for k in ('pa', 'sa', 'pb', 'sb', 'pc', 'sc')}
    for i in range(depth):
        j = i // 2
        gmix = p['norm_mix'][i].reshape(1, d)
        gffn = p['norm_ffn'][i].reshape(1, d)
        if i % 2 == 0:
            conv_w = (p['w_conv_a'][j], p['w_conv_b'][j], p['b_conv_b'][j].reshape(1, -1),
                      p['ln_b_g'][j].reshape(1, -1), p['ln_b_b'][j].reshape(1, -1), p['w_out'][j])
            ffn_w = (p['w_ffn_gate'][j], p['w_ffn_up'][j], p['w_ffn_down'][j])
            u_p = _in_proj(g_p, x_p, mod3, i, gmix, p['w_in'][j])
            x_p, na, nb = _conv_seq(g_p, u_p, x_p, mod3, i, *conv_w)
            new['pa'].append(na)
            new['pb'].append(nb)
            x_p = _ffn(g_p, x_p, mod3, i, gffn, *ffn_w)
            u_s = _in_proj(g_s, x_s, mod3, i, gmix, p['w_in'][j])
            x_s, na, nb = _conv_step(g_s, u_s, st_a[j], st_b[j], x_s, mod3, i, *conv_w)
            new['sa'].append(na)
            new['sb'].append(nb)
            x_s = _ffn(g_s, x_s, mod3, i, gffn, *ffn_w)
        else:
            ctx = p['pool_ctx']
            pool_w = (p['w_pool'][j], p['pool_scale'][j].reshape(1, d), p['wr_hi'][j], p['wr_lo'][j], p['br'][j])
            x3_p, h3_p, mi_p, mf_p, nc, cnt1 = _pool_seq(g_p, x_p, mod3, i, gmix, gffn, *pool_w, ctx)
            new['pc'].append(nc)
            x3_s, h3_s, mi_s, mf_s, nc, cnt = _pool_step(g_s, x_s, st_c[j], cnt1, mod3, i, gmix, gffn, *pool_w, ctx)
            new['sc'].append(nc)
            x_p, x_s = _moe_layer((g_p, g_s), (x3_p, x3_s), (h3_p, h3_s), (mi_p, mi_s), (mf_p, mf_s), cnt1, cnt,
                                  mod3, i, p['w_exp_gate'][j], p['w_exp_up'][j], p['w_exp_down'][j],
                                  p['norm_out'].reshape(1, d), i == depth - 1)
    return x_p, x_s, new


def kernel(x_prompt, x_sample, state_a, state_b, state_c, c_prompt, c_sample, w_ada, b_ada, norm_mix, norm_ffn, norm_out, w_in, w_conv_a, w_conv_b, b_conv_b, ln_b_g, ln_b_b, w_out, w_ffn_gate, w_ffn_up, w_ffn_down, w_pool, pool_scale, w_router, b_router, w_exp_gate, w_exp_up, w_exp_down):
    n_p, seq, d = x_prompt.shape
    n_s, dec_seq, _ = x_sample.shape
    depth = w_ada.shape[0]
    n_e = w_router.shape[2]
    assert dec_seq == 1 and depth % 2 == 0 and n_s % SUBLANES == 0 and n_e <= LANES
    n_even, n_odd = state_a.shape[0], state_c.shape[0]
    ctx = state_c.shape[2]

    wr = jnp.pad(w_router, ((0, 0), (0, 0), (0, LANES - n_e)))
    wr_hi = wr.astype(BF16)
    wr_lo = (wr - wr_hi.astype(F32)).astype(BF16)
    br = jnp.pad(b_router, ((0, 0), (0, LANES - n_e)), constant_values=NEG_BIG).reshape(n_odd, 1, LANES)

    p = {
        'w_ada': w_ada, 'norm_mix': norm_mix, 'norm_ffn': norm_ffn, 'norm_out': norm_out,
        'w_in': w_in.astype(BF16), 'w_conv_a': w_conv_a, 'w_conv_b': w_conv_b, 'b_conv_b': b_conv_b,
        'ln_b_g': ln_b_g, 'ln_b_b': ln_b_b, 'w_out': w_out.astype(BF16),
        'w_ffn_gate': w_ffn_gate.astype(BF16), 'w_ffn_up': w_ffn_up.astype(BF16),
        'w_ffn_down': w_ffn_down.astype(BF16),
        'w_pool': w_pool.astype(BF16), 'pool_scale': pool_scale, 'wr_hi': wr_hi, 'wr_lo': wr_lo, 'br': br,
        'w_exp_gate': w_exp_gate, 'w_exp_up': w_exp_up, 'w_exp_down': w_exp_down, 'pool_ctx': ctx,
    }

    mod3 = _ada(jnp.concatenate([c_sample, c_prompt], axis=0), w_ada, b_ada)

    g_prompt = _Group(n_p, seq, False, n_s, n_s)
    g_sample = _Group(n_s, 1, True, 0, n_s)

    y_p, y_s, new = _layers(
        g_prompt, g_sample, x_prompt.reshape(n_p * seq, d), x_sample.reshape(n_s, d), mod3,
        state_a.reshape(n_even, n_s, -1), state_b.reshape(n_even, n_s, -1), state_c.reshape(n_odd, n_s, -1), p)

    sa = [a.reshape(n_s, state_a.shape[2], -1) for a in new['sa']]
    sb = [b.reshape(n_s, state_b.shape[2], -1) for b in new['sb']]
    sc = [c.reshape(n_s, ctx, d) for c in new['sc']]
    return (y_p.reshape(n_p, seq, d), y_s.reshape(n_s, 1, d),
            jnp.stack(new['pa']), jnp.stack(sa), jnp.stack(new['pb']), jnp.stack(sb),
            jnp.stack(new['pc']), jnp.stack(sc))
```

```python
import functools

import jax
import jax.numpy as jnp
from jax import lax
from jax.experimental import pallas as pl
from jax.experimental.pallas import tpu as pltpu

EPS = 1e-6
PAST_LEN = 16384
POOL_WINDOWS = (2, 4, 8, 16)
N_MOD = 6
LANES = 128
SUBLANES = 8
VMEM_LIMIT = 56 * 1024 * 1024
NEG_BIG = -1e30

ROW_TILE = 512
MOE_TILE = 1024
MOE_SUB = 256
MOE_FF = 512

F32 = jnp.float32
BF16 = jnp.bfloat16


def _params(sem):
    return pltpu.CompilerParams(dimension_semantics=sem, vmem_limit_bytes=VMEM_LIMIT)


def _silu(x):
    return x * jax.nn.sigmoid(x)


def _rmsnorm(x, g):
    return x * lax.rsqrt(jnp.mean(x * x, axis=-1, keepdims=True) + EPS) * g


def _mod_rmsnorm(x, g, sc, sh):
    return _rmsnorm(x, g) * (1.0 + sc) + sh


def _resident(shape):
    nd = len(shape)
    return pl.BlockSpec(shape, lambda *_: (0,) * nd, pipeline_mode=pl.Buffered(1))


def _mod(ref, mod_row):
    if mod_row is None:
        return ref[...]
    return ref[pl.ds(lax.rem(mod_row + pl.program_id(0), SUBLANES), 1), :]


def _small(shape):
    nd = len(shape)
    return pl.BlockSpec(shape, lambda *_: (0,) * nd)


class _Group:
    def __init__(self, n_seq, seq_len, per_row_mod, mod_row0, n_sample):
        self.n_seq, self.seq_len = n_seq, seq_len
        self.per_row_mod = per_row_mod
        self.mod_row0 = mod_row0
        self.n_sample = n_sample
        if per_row_mod:
            self.tile = n_seq
            self.grid = (1, 1)
        else:
            self.tile = min(ROW_TILE, seq_len)
            assert seq_len % self.tile == 0 and self.tile % 32 == 0
            self.grid = (n_seq, seq_len // self.tile)
        self.rows = n_seq * seq_len

    def row_spec(self, width):
        nt = self.grid[1]
        return pl.BlockSpec((self.tile, width), lambda b, j, *_: (b * nt + j, 0))

    def mod_spec(self, layer, chunk, d):
        if self.per_row_mod:
            return pl.BlockSpec((None, self.n_sample, d), lambda b, j, *_: (layer, 0, chunk))
        r0 = self.mod_row0
        return pl.BlockSpec((None, SUBLANES, d), lambda b, j, *_: (layer, (r0 + b) // SUBLANES, chunk))

    @property
    def mod_row(self):
        return None if self.per_row_mod else self.mod_row0


def _ada_kernel(c_ref, w_ref, b_ref, o_ref):
    cs = _silu(c_ref[...]).astype(BF16)
    o_ref[...] = jnp.dot(cs, w_ref[...].astype(BF16), preferred_element_type=F32) + b_ref[...]


def _ada(c_all, w_ada, b_ada):
    depth, d, w6 = w_ada.shape
    r = c_all.shape[0]
    tn = w6 // 4
    return pl.pallas_call(
        _ada_kernel,
        out_shape=jax.ShapeDtypeStruct((depth, r, w6), F32),
        grid=(depth, w6 // tn),
        in_specs=[pl.BlockSpec((r, d), lambda l, n: (0, 0)),
                  pl.BlockSpec((None, d, tn), lambda l, n: (l, 0, n)),
                  pl.BlockSpec((None, 1, tn), lambda l, n: (l, 0, n))],
        out_specs=pl.BlockSpec((None, r, tn), lambda l, n: (l, 0, n)),
        compiler_params=_params(("arbitrary", "arbitrary")),
        name="ada",
    )(c_all, w_ada, b_ada.reshape(depth, 1, w6))


def _in_proj_kernel(mrow, x_ref, g_ref, sc_ref, sh_ref, w_ref, u_ref):
    h = _mod_rmsnorm(x_ref[...], g_ref[...], _mod(sc_ref, mrow), _mod(sh_ref, mrow))
    u_ref[...] = jnp.dot(h.astype(BF16), w_ref[...], preferred_element_type=F32)


def _in_proj(grp, x, mod3, layer, g, w_bf):
    d, n = w_bf.shape
    return pl.pallas_call(
        functools.partial(_in_proj_kernel, grp.mod_row),
        out_shape=jax.ShapeDtypeStruct((grp.rows, n), F32),
        grid=grp.grid,
        in_specs=[grp.row_spec(d), _small((1, d)),
                  grp.mod_spec(layer, 1, d), grp.mod_spec(layer, 0, d),
                  _resident((d, n))],
        out_specs=grp.row_spec(n),
        compiler_params=_params(("arbitrary", "arbitrary")),
        name="in_proj",
    )(x, g, mod3, mod3, w_bf)


def _layernorm(y, g, b):
    mu = jnp.mean(y, axis=-1, keepdims=True)
    yc = y - mu
    var = jnp.mean(yc * yc, axis=-1, keepdims=True)
    return yc * lax.rsqrt(var + EPS) * g + b


def _causal_taps(pad_ref, w_ref, k_taps, halo, tt):
    off0 = halo - (k_taps - 1)
    y = None
    for c in range(SUBLANES):
        n = tt + (SUBLANES if c else 0)
        q = None
        for k in range(k_taps):
            if (off0 + k) % SUBLANES == c:
                a = off0 + k - c
                term = w_ref[k:k + 1, :] * pad_ref[a:a + n, :]
                q = term if q is None else q + term
        if q is not None:
            part = q[c:c + tt, :]
            y = part if y is None else y + part
    return y


def _conv_seq_kernel(mrow, n_t, ka, kb, u_ref, uh_ref, x_ref, gm_ref, wa_ref, wb_ref, bb_ref,
                     lng_ref, lnb_ref, wout_ref, x1_ref, na_ref, nb_ref, apad, gpad):
    j = pl.program_id(1)
    tt = u_ref.shape[0]
    da = wa_ref.shape[1]
    hb = uh_ref.shape[0]
    keep = (j > 0).astype(F32)

    u = u_ref[...]
    a_b, a_c, a_x = u[:, 0:da], u[:, da:2 * da], u[:, 2 * da:3 * da]
    b_v, b_g = u[:, 3 * da:4 * da], u[:, 4 * da:5 * da]
    uh = uh_ref[...]
    apad[0:SUBLANES, :] = keep * (uh[hb - SUBLANES:hb, da:2 * da] * uh[hb - SUBLANES:hb, 2 * da:3 * da])
    apad[SUBLANES:SUBLANES + tt, :] = a_c * a_x
    gpad[0:hb, :] = keep * (uh[:, 3 * da:4 * da] * jax.nn.sigmoid(uh[:, 4 * da:5 * da]))
    gpad[hb:hb + tt, :] = b_v * jax.nn.sigmoid(b_g)

    zero_tail = jnp.zeros((SUBLANES, da), F32)
    apad[SUBLANES + tt:SUBLANES + tt + SUBLANES, :] = zero_tail
    gpad[hb + tt:hb + tt + SUBLANES, :] = zero_tail

    ya = a_b * _causal_taps(apad, wa_ref, ka, SUBLANES, tt)
    yb = _causal_taps(gpad, wb_ref, kb, hb, tt)
    yb = _silu(_layernorm(yb + bb_ref[...], lng_ref[...], lnb_ref[...]))

    ycat = jnp.concatenate([ya, yb], axis=-1).astype(BF16)
    y = jnp.dot(ycat, wout_ref[...], preferred_element_type=F32)
    x1_ref[...] = x_ref[...] + _mod(gm_ref, mrow) * y

    @pl.when(j == n_t - 1)
    def _():
        na_ref[...] = apad[SUBLANES + tt - (ka - 1):SUBLANES + tt, :]
        nb_ref[...] = gpad[hb + tt - (kb - 1):hb + tt, :]


def _conv_seq(grp, u, x, mod3, layer, wa, wb, bb, lng, lnb, wout_bf):
    tt = grp.tile
    n_b, n_t = grp.grid
    d = x.shape[1]
    ka, da = wa.shape
    kb = wb.shape[0]
    hb = 32
    assert kb - 1 <= hb and ka - 1 <= SUBLANES and tt % hb == 0
    per = tt // hb
    halo = pl.BlockSpec((hb, u.shape[1]), lambda b, j: (jnp.maximum((b * n_t + j) * per - 1, 0), 0))
    return pl.pallas_call(
        functools.partial(_conv_seq_kernel, grp.mod_row, n_t, ka, kb),
        out_shape=(jax.ShapeDtypeStruct((grp.rows, d), F32),
                   jax.ShapeDtypeStruct((n_b, ka - 1, da), F32),
                   jax.ShapeDtypeStruct((n_b, kb - 1, da), F32)),
        grid=grp.grid,
        in_specs=[grp.row_spec(u.shape[1]), halo, grp.row_spec(d), grp.mod_spec(layer, 2, d),
                  _small(wa.shape), _small(wb.shape), _small((1, da)), _small((1, da)), _small((1, da)),
                  _resident(wout_bf.shape)],
        out_specs=(grp.row_spec(d),
                   pl.BlockSpec((None, ka - 1, da), lambda b, j: (b, 0, 0)),
                   pl.BlockSpec((None, kb - 1, da), lambda b, j: (b, 0, 0))),
        scratch_shapes=[pltpu.VMEM((2 * SUBLANES + tt, da), F32), pltpu.VMEM((hb + tt + SUBLANES, da), F32)],
        compiler_params=_params(("arbitrary", "arbitrary")),
        name="conv_seq",
    )(u, u, x, mod3, wa, wb, bb, lng, lnb, wout_bf)


def _conv_step_kernel(ka, kb, u_ref, sa_ref, sb_ref, x_ref, gm_ref, wa_ref, wb_ref, bb_ref,
                      lng_ref, lnb_ref, wout_ref, x1_ref, na_ref, nb_ref):
    da = wa_ref.shape[1]
    u = u_ref[...]
    a_b, a_c, a_x = u[:, 0:da], u[:, da:2 * da], u[:, 2 * da:3 * da]
    b_v, b_g = u[:, 3 * da:4 * da], u[:, 4 * da:5 * da]

    cur = a_c * a_x
    ya = wa_ref[ka - 1:ka, :] * cur
    for k in range(ka - 1):
        ya = ya + wa_ref[k:k + 1, :] * sa_ref[:, k, :]
    ya = a_b * ya
    glu = b_v * jax.nn.sigmoid(b_g)
    yb = wb_ref[kb - 1:kb, :] * glu
    for k in range(kb - 1):
        yb = yb + wb_ref[k:k + 1, :] * sb_ref[:, k, :]
    yb = _silu(_layernorm(yb + bb_ref[...], lng_ref[...], lnb_ref[...]))

    ycat = jnp.concatenate([ya, yb], axis=-1).astype(BF16)
    y = jnp.dot(ycat, wout_ref[...], preferred_element_type=F32)
    x1_ref[...] = x_ref[...] + gm_ref[...] * y

    if ka > 2:
        na_ref[:, 0:ka - 2, :] = sa_ref[:, 1:ka - 1, :]
    na_ref[:, ka - 2, :] = cur
    if kb > 2:
        nb_ref[:, 0:kb - 2, :] = sb_ref[:, 1:kb - 1, :]
    nb_ref[:, kb - 2, :] = glu


def _conv_step(grp, u, sa2, sb2, x, mod3, layer, wa, wb, bb, lng, lnb, wout_bf):
    n = grp.rows
    d = x.shape[1]
    ka, da = wa.shape
    kb = wb.shape[0]
    return pl.pallas_call(
        functools.partial(_conv_step_kernel, ka, kb),
        out_shape=(jax.ShapeDtypeStruct((n, d), F32),
                   jax.ShapeDtypeStruct(sa2.shape, F32),
                   jax.ShapeDtypeStruct(sb2.shape, F32)),
        grid=grp.grid,
        in_specs=[_small(u.shape), _small(sa2.shape), _small(sb2.shape), _small(x.shape),
                  grp.mod_spec(layer, 2, d),
                  _small(wa.shape), _small(wb.shape), _small((1, da)), _small((1, da)), _small((1, da)),
                  _small(wout_bf.shape)],
        out_specs=(_small((n, d)), _small(sa2.shape), _small(sb2.shape)),
        compiler_params=_params(("arbitrary", "arbitrary")),
        name="conv_step",
    )(u, sa2, sb2, x, mod3, wa, wb, bb, lng, lnb, wout_bf)


def _ffn_kernel(mrow, x_ref, g_ref, sc_ref, sh_ref, gate_ref, wg_ref, wu_ref, wd_ref, o_ref):
    x = x_ref[...]
    h = _mod_rmsnorm(x, g_ref[...], _mod(sc_ref, mrow), _mod(sh_ref, mrow)).astype(BF16)
    a = jnp.dot(h, wg_ref[...], preferred_element_type=F32)
    b = jnp.dot(h, wu_ref[...], preferred_element_type=F32)
    act = (_silu(a) * b).astype(BF16)
    f = jnp.dot(act, wd_ref[...], preferred_element_type=F32)
    o_ref[...] = x + _mod(gate_ref, mrow) * f


def _ffn(grp, x, mod3, layer, g, wg_bf, wu_bf, wd_bf):
    d = x.shape[1]
    return pl.pallas_call(
        functools.partial(_ffn_kernel, grp.mod_row),
        out_shape=jax.ShapeDtypeStruct(x.shape, F32),
        grid=grp.grid,
        in_specs=[grp.row_spec(d), _small((1, d)),
                  grp.mod_spec(layer, 4, d), grp.mod_spec(layer, 3, d), grp.mod_spec(layer, 5, d),
                  _resident(wg_bf.shape), _resident(wu_bf.shape), _resident(wd_bf.shape)],
        out_specs=grp.row_spec(d),
        compiler_params=_params(("arbitrary", "arbitrary")),
        name="ffn",
    )(x, g, mod3, mod3, mod3, wg_bf, wu_bf, wd_bf)


def _window_sum(v, win):
    assert win & (win - 1) == 0
    span = 1
    while span < win:
        v = v + pltpu.roll(v, span, 0)
        span *= 2
    return v


def _pool_project(diff, wp_ref):
    n_g, pg, _ = wp_ref.shape
    outs = [jnp.dot(diff[:, gi * pg:(gi + 1) * pg].astype(BF16), wp_ref[gi], preferred_element_type=F32)
            for gi in range(n_g)]
    return jnp.concatenate(outs, axis=-1)


def _route(h3, wrh_ref, wrl_ref, br_ref, tri_ref, base_ref, mi_ref, mf_ref, cnt_ref):
    tt = h3.shape[0]
    h_hi = h3.astype(BF16)
    h_lo = (h3 - h_hi.astype(F32)).astype(BF16)
    logits = (jnp.dot(h_hi, wrh_ref[...], preferred_element_type=F32)
              + jnp.dot(h_lo, wrh_ref[...], preferred_element_type=F32)
              + jnp.dot(h_hi, wrl_ref[...], preferred_element_type=F32)
              + br_ref[...])
    lane = lax.broadcasted_iota(jnp.int32, (tt, LANES), 1).astype(F32)
    m0 = jnp.max(logits, axis=-1, keepdims=True)
    e0 = jnp.min(jnp.where(logits == m0, lane, float(LANES)), axis=-1, keepdims=True)
    rest = jnp.where(lane == e0, NEG_BIG * 2.0, logits)
    m1 = jnp.max(rest, axis=-1, keepdims=True)
    e1 = jnp.min(jnp.where(rest == m1, lane, float(LANES)), axis=-1, keepdims=True)
    dd = jnp.exp(m1 - m0)
    p0 = 1.0 / (1.0 + dd)
    p1 = dd * p0

    oh0 = lane == e0
    oh1 = lane == e1
    c = jnp.where(oh0 | oh1, 1.0, 0.0)
    prefix = jnp.dot(tri_ref[...], c.astype(BF16), preferred_element_type=F32)
    tot = base_ref[...] + prefix
    r0 = jnp.sum(jnp.where(oh0, tot, 0.0), axis=-1, keepdims=True)
    r1 = jnp.sum(jnp.where(oh1, tot, 0.0), axis=-1, keepdims=True)
    new_base = base_ref[...] + jnp.sum(c, axis=0, keepdims=True)
    base_ref[...] = new_base
    cnt_ref[...] = jnp.broadcast_to(new_base, cnt_ref.shape)

    meta = jnp.where(lane == 0.0, e0, jnp.where(lane == 1.0, e1,
                     jnp.where(lane == 2.0, r0, jnp.where(lane == 3.0, r1, 0.0))))
    mi_ref[...] = meta.T[0:SUBLANES, :].astype(jnp.int32)
    mf_ref[...] = jnp.where(lane == 0.0, p0, jnp.where(lane == 1.0, p1, 0.0))


def _init_route_scratch(first, tri_ref, base_ref, base0=None):
    @pl.when(first)
    def _():
        tt = tri_ref.shape[0]
        r = lax.broadcasted_iota(jnp.int32, (tt, tt), 0)
        c = lax.broadcasted_iota(jnp.int32, (tt, tt), 1)
        tri_ref[...] = jnp.where(c < r, 1.0, 0.0).astype(BF16)
        base_ref[...] = jnp.zeros(base_ref.shape, F32) if base0 is None else base0


def _pool_seq_kernel(mrow, n_t, ctx, x_ref, xh_ref, gmix_ref, shm_ref, scm_ref, gm_ref, gffn_ref, shf_ref, scf_ref,
                     wp_ref, ps_ref, wrh_ref, wrl_ref, br_ref,
                     x3_ref, h3_ref, mi_ref, mf_ref, nc_ref, cnt_ref, hpad, tri_ref, base_ref):
    b = pl.program_id(0)
    j = pl.program_id(1)
    tt = x_ref.shape[0]
    hb = xh_ref.shape[0]
    n_g, pg, _ = wp_ref.shape
    _init_route_scratch((b == 0) & (j == 0), tri_ref, base_ref)

    keep = (j > 0).astype(F32)
    x = x_ref[...]
    scm, shm = _mod(scm_ref, mrow), _mod(shm_ref, mrow)
    h = _mod_rmsnorm(x, gmix_ref[...], scm, shm)
    hpad[0:hb, :] = keep * _mod_rmsnorm(xh_ref[...], gmix_ref[...], scm, shm)
    hpad[hb:hb + tt, :] = h

    pos = lax.broadcasted_iota(jnp.int32, (tt, pg), 0) + j * tt
    groups = []
    for gi, win in enumerate(POOL_WINDOWS):
        sl = slice(gi * pg, (gi + 1) * pg)
        s = _window_sum(hpad[:, sl], win)[hb:hb + tt, :]
        cnt = jnp.minimum(pos + 1, win).astype(F32)
        groups.append(s / cnt - h[:, sl])
    diff = jnp.concatenate(groups, axis=-1)
    y = _pool_project(diff, wp_ref) * ps_ref[...]
    x3 = x + _mod(gm_ref, mrow) * y
    x3_ref[...] = x3
    h3 = _mod_rmsnorm(x3, gffn_ref[...], _mod(scf_ref, mrow), _mod(shf_ref, mrow))
    h3_ref[...] = h3
    _route(h3, wrh_ref, wrl_ref, br_ref, tri_ref, base_ref, mi_ref, mf_ref, cnt_ref)

    @pl.when(j == n_t - 1)
    def _():
        nc_ref[...] = hpad[hb + tt - ctx:hb + tt, :]


def _route_out_shapes(rows, d):
    return (jax.ShapeDtypeStruct((rows, d), F32), jax.ShapeDtypeStruct((rows, d), F32),
            jax.ShapeDtypeStruct((SUBLANES, rows), jnp.int32), jax.ShapeDtypeStruct((rows, LANES), F32))


def _pool_seq(grp, x, mod3, layer, gmix, gffn, wp_bf, ps, wrh, wrl, br, ctx):
    tt = grp.tile
    n_b, n_t = grp.grid
    d = x.shape[1]
    hb = 16
    assert ctx <= hb and max(POOL_WINDOWS) - 1 <= hb and tt % hb == 0
    per = tt // hb
    halo = pl.BlockSpec((hb, d), lambda b, j: (jnp.maximum((b * n_t + j) * per - 1, 0), 0))
    ms = lambda c: grp.mod_spec(layer, c, d)
    return pl.pallas_call(
        functools.partial(_pool_seq_kernel, grp.mod_row, n_t, ctx),
        out_shape=_route_out_shapes(grp.rows, d) + (
            jax.ShapeDtypeStruct((n_b, ctx, d), F32), jax.ShapeDtypeStruct((SUBLANES, LANES), F32)),
        grid=grp.grid,
        in_specs=[grp.row_spec(d), halo, _small((1, d)), ms(0), ms(1), ms(2), _small((1, d)), ms(3), ms(4),
                  _small(wp_bf.shape), _small((1, d)), _small(wrh.shape), _small(wrl.shape), _small(br.shape)],
        out_specs=(grp.row_spec(d), grp.row_spec(d), pl.BlockSpec((SUBLANES, tt), lambda b, j: (0, b * n_t + j)),
                   grp.row_spec(LANES), pl.BlockSpec((None, ctx, d), lambda b, j: (b, 0, 0)), _small((SUBLANES, LANES))),
        scratch_shapes=[pltpu.VMEM((hb + tt, d), F32), pltpu.VMEM((tt, tt), BF16), pltpu.VMEM((1, LANES), F32)],
        compiler_params=_params(("arbitrary", "arbitrary")),
        name="pool_seq",
    )(x, x, gmix, *([mod3] * 3), gffn, *([mod3] * 2), wp_bf, ps, wrh, wrl, br)


def _pool_step_kernel(ctx, x_ref, sc_ref_state, cnt0_ref, gmix_ref, shm_ref, scm_ref, gm_ref, gffn_ref, shf_ref,
                      scf_ref, wp_ref, ps_ref, wrh_ref, wrl_ref, br_ref,
                      x3_ref, h3_ref, mi_ref, mf_ref, nc_ref, cnt_ref, tri_ref, base_ref):
    d = x_ref.shape[1]
    n_g, pg, _ = wp_ref.shape
    _init_route_scratch(pl.program_id(0) == 0, tri_ref, base_ref, cnt0_ref[0:1, :])

    x = x_ref[...]
    h = _mod_rmsnorm(x, gmix_ref[...], scm_ref[...], shm_ref[...])
    groups = []
    for gi, win in enumerate(POOL_WINDOWS):
        s = h[:, gi * pg:(gi + 1) * pg]
        for i in range(1, win):
            row = ctx - i
            s = s + sc_ref_state[:, row, gi * pg:(gi + 1) * pg]
        cnt = float(min(PAST_LEN + 1, win))
        groups.append(s / cnt - h[:, gi * pg:(gi + 1) * pg])
    diff = jnp.concatenate(groups, axis=-1)
    y = _pool_project(diff, wp_ref) * ps_ref[...]
    x3 = x + gm_ref[...] * y
    x3_ref[...] = x3
    h3 = _mod_rmsnorm(x3, gffn_ref[...], scf_ref[...], shf_ref[...])
    h3_ref[...] = h3
    _route(h3, wrh_ref, wrl_ref, br_ref, tri_ref, base_ref, mi_ref, mf_ref, cnt_ref)

    if ctx > 1:
        nc_ref[:, 0:ctx - 1, :] = sc_ref_state[:, 1:ctx, :]
    nc_ref[:, ctx - 1, :] = h


def _pool_step(grp, x, sc2, cnt0, mod3, layer, gmix, gffn, wp_bf, ps, wrh, wrl, br, ctx):
    n, d = x.shape
    ms = lambda c: grp.mod_spec(layer, c, d)
    return pl.pallas_call(
        functools.partial(_pool_step_kernel, ctx),
        out_shape=_route_out_shapes(n, d) + (
            jax.ShapeDtypeStruct(sc2.shape, F32), jax.ShapeDtypeStruct((SUBLANES, LANES), F32)),
        grid=grp.grid,
        in_specs=[_small(x.shape), _small(sc2.shape), _small(cnt0.shape), _small((1, d)), ms(0), ms(1), ms(2),
                  _small((1, d)), ms(3), ms(4),
                  _small(wp_bf.shape), _small((1, d)), _small(wrh.shape), _small(wrl.shape), _small(br.shape)],
        out_specs=(_small((n, d)), _small((n, d)), _small((SUBLANES, n)), _small((n, LANES)),
                   _small(sc2.shape), _small((SUBLANES, LANES))),
        scratch_shapes=[pltpu.VMEM((n, n), BF16), pltpu.VMEM((1, LANES), F32)],
        compiler_params=_params(("arbitrary", "arbitrary")),
        name="pool_step",
    )(x, sc2, cnt0, gmix, *([mod3] * 3), gffn, *([mod3] * 2), wp_bf, ps, wrh, wrl, br)


def _invert_kernel(pos_ref, code_ref):
    n_real = pos_ref.shape[0]
    m = n_real // 2

    def body(i, carry):
        code_ref[pos_ref[i]] = 2 * i
        code_ref[pos_ref[m + i]] = 2 * i + 1
        return carry

    lax.fori_loop(0, m, body, 0, unroll=8)

    def pad(s, carry):
        code_ref[s] = s
        return carry

    lax.fori_loop(n_real, code_ref.shape[0], pad, 0)


def _invert(pos_flat, n_slots):
    smem = pl.BlockSpec(memory_space=pltpu.SMEM)
    return pl.pallas_call(
        _invert_kernel,
        out_shape=jax.ShapeDtypeStruct((n_slots,), jnp.int32),
        in_specs=[smem], out_specs=smem,
        name="invert",
    )(pos_flat)


FLAG_FIRST, FLAG_LAST, FLAG_FINAL = 1, 2, 4


def _moe_kernel(sub, n_fc, we_ref, wt_ref, wlo_ref, whi_ref, wflag_ref, wfix_lo_ref, wfix_hi_ref,
                code_prev_ref, code_ref, code_next_ref, h_hbm, h2_hbm, wg_ref, wu_ref, wd_ref, yk_hbm,
                xs_buf, acc, xb, wgb, wub, wdb, gsem, ssem, fsem):
    w = pl.program_id(0)
    j = pl.program_id(1)
    tm = xb.shape[0]
    t = wt_ref[w]
    slot = lax.rem(t, 2)
    other = 1 - slot
    lo = wlo_ref[w]
    hi = whi_ref[w]
    flag = wflag_ref[w]
    first = (flag & FLAG_FIRST) != 0
    last = (flag & FLAG_LAST) != 0
    final = (flag & FLAG_FINAL) != 0
    n_share = max(n_fc - 1, 1)
    share = (tm // n_share) // SUBLANES * SUBLANES
    rest = tm - share * n_share

    m1 = h_hbm.shape[0]

    def gather_row(codes, s, r):
        tok = jnp.minimum(lax.shift_right_logical(codes[0, r], 1), m1 - 1)
        return pltpu.make_async_copy(h_hbm.at[pl.ds(tok, 1), :], xs_buf.at[s, pl.ds(r, 1), :], gsem.at[s])

    def second_group_row(r):
        tok = lax.shift_right_logical(code_ref[0, r], 1) - m1
        return pltpu.make_async_copy(h2_hbm.at[pl.ds(tok, 1), :], xs_buf.at[slot, pl.ds(r, 1), :], fsem)

    def refetch_second_group(r0, r1):
        def start(r, carry):
            second_group_row(r).start()
            return carry

        def wait(r, carry):
            second_group_row(r).wait()
            return carry

        lax.fori_loop(r0, r1, start, 0)
        lax.fori_loop(r0, r1, wait, 0)

    def scatter_row(codes, s, r):
        code = codes[0, r]
        dst = yk_hbm.at[code & 1, pl.ds(lax.shift_right_logical(code, 1), 1), :]
        return pltpu.make_async_copy(acc.at[s, pl.ds(r, 1), :], dst, ssem.at[s])

    def wait_gather(s):
        pltpu.make_async_copy(h_hbm.at[pl.ds(0, tm), :], xs_buf.at[s], gsem.at[s]).wait()

    def wait_scatter(s):
        pltpu.make_async_copy(acc.at[s], yk_hbm.at[0, pl.ds(0, tm), :], ssem.at[s]).wait()

    def rows_loop(r0, n, fn):
        def body(i, carry):
            fn(r0 + i)
            return carry

        lax.fori_loop(0, n, body, 0, unroll=SUBLANES)

    def neighbour_share():
        base = pl.multiple_of(j * share, SUBLANES)
        for u in range(share):
            gather_row(code_next_ref, other, base + u).start()
            scatter_row(code_prev_ref, other, base + u).start()

    @pl.when((w == 0) & (j == 0))
    def _():
        rows_loop(0, tm, lambda r: gather_row(code_ref, 0, r).start())
        acc[1] = jnp.zeros((tm, acc.shape[2]), F32)

    @pl.when(first & (j == 0))
    def _():
        wait_gather(slot)
        acc[slot] = jnp.zeros((tm, acc.shape[2]), F32)
        if rest:
            rows_loop(share * n_share, rest, lambda r: gather_row(code_next_ref, other, r).start())
            rows_loop(share * n_share, rest, lambda r: scatter_row(code_prev_ref, other, r).start())

    def swiglu(x, wg, wu, wd):
        a = jnp.dot(x, wg, preferred_element_type=F32)
        b = jnp.dot(x, wu, preferred_element_type=F32)
        return jnp.dot((_silu(a) * b).astype(BF16), wd, preferred_element_type=F32)

    n_sub = tm // sub
    touched = lax.div(hi + (sub - 1), sub) - lax.div(lo, sub)
    straight = first & (touched * 4 >= n_sub * 3)

    @pl.when((hi > lo) & (j == 0))
    def _():
        refetch_second_group(wfix_lo_ref[w], wfix_hi_ref[w])
        rows = lax.broadcasted_iota(jnp.int32, xb.shape, 0)
        xb[...] = jnp.where((rows >= lo) & (rows < hi), xs_buf[slot], 0.0).astype(BF16)

    def whole_tile():
        acc[slot] += swiglu(xb[...], wg_ref[...].astype(BF16), wu_ref[...].astype(BF16),
                            wd_ref[...].astype(BF16))

    @pl.when(straight & (j < n_share))
    def _():
        neighbour_share()
        whole_tile()

    @pl.when(straight & (j >= n_share))
    def _():
        whole_tile()

    @pl.when((hi > lo) & jnp.logical_not(straight))
    def _():
        @pl.when(first & (j < n_share))
        def _():
            neighbour_share()

        wgb[...] = wg_ref[...].astype(BF16)
        wub[...] = wu_ref[...].astype(BF16)
        wdb[...] = wd_ref[...].astype(BF16)
        for s in range(n_sub):
            @pl.when((s * sub < hi) & ((s + 1) * sub > lo))
            def _():
                rs = pl.ds(s * sub, sub)
                acc[slot, rs, :] += swiglu(xb[rs, :], wgb[...], wub[...], wdb[...])

    @pl.when(last & (j == n_fc - 1))
    def _():
        wait_scatter(other)

        @pl.when(final)
        def _():
            rows_loop(0, tm, lambda r: scatter_row(code_ref, slot, r).start())
            wait_scatter(slot)
            wait_gather(other)


def _moe(h3, h3b, codes, work, wg, wu, wd, tm):
    d = h3.shape[1]
    n_e, _, ff = wg.shape
    fc = min(MOE_FF, ff)
    assert ff % fc == 0
    n_fc = ff // fc
    sub = min(MOE_SUB, tm)
    n_work = work[0].shape[0]
    n_tiles = codes.shape[0]

    def jj(w, j, wlo, whi):
        return jnp.where(whi[w] > wlo[w], j, n_fc - 1)

    grid_spec = pltpu.PrefetchScalarGridSpec(
        num_scalar_prefetch=7,
        grid=(n_work, n_fc),
        in_specs=[pl.BlockSpec((None, 1, tm), lambda w, j, we, wt, wlo, whi, wf, fl, fh: (jnp.maximum(wt[w] - 1, 0), 0, 0),
                               memory_space=pltpu.SMEM),
                  pl.BlockSpec((None, 1, tm), lambda w, j, we, wt, wlo, whi, wf, fl, fh: (wt[w], 0, 0),
                               memory_space=pltpu.SMEM),
                  pl.BlockSpec((None, 1, tm),
                               lambda w, j, we, wt, wlo, whi, wf, fl, fh: (jnp.minimum(wt[w] + 1, n_tiles - 1), 0, 0),
                               memory_space=pltpu.SMEM),
                  pl.BlockSpec(memory_space=pl.ANY), pl.BlockSpec(memory_space=pl.ANY),
                  pl.BlockSpec((None, d, fc), lambda w, j, we, wt, wlo, whi, wf, fl, fh: (we[w], 0, jj(w, j, wlo, whi))),
                  pl.BlockSpec((None, d, fc), lambda w, j, we, wt, wlo, whi, wf, fl, fh: (we[w], 0, jj(w, j, wlo, whi))),
                  pl.BlockSpec((None, fc, d), lambda w, j, we, wt, wlo, whi, wf, fl, fh: (we[w], jj(w, j, wlo, whi), 0))],
        out_specs=pl.BlockSpec(memory_space=pl.ANY),
        scratch_shapes=[pltpu.VMEM((2, tm, d), F32), pltpu.VMEM((2, tm, d), F32), pltpu.VMEM((tm, d), BF16),
                        pltpu.VMEM((d, fc), BF16), pltpu.VMEM((d, fc), BF16), pltpu.VMEM((fc, d), BF16),
                        pltpu.SemaphoreType.DMA((2,)), pltpu.SemaphoreType.DMA((2,)), pltpu.SemaphoreType.DMA(())])
    return pl.pallas_call(
        functools.partial(_moe_kernel, sub, n_fc),
        out_shape=jax.ShapeDtypeStruct((2, n_tiles * tm // 2, d), F32),
        grid_spec=grid_spec,
        compiler_params=_params(("arbitrary", "arbitrary")),
        name="moe",
    )(*work, codes, codes, codes, h3, h3b, wg, wu, wd)


def _moe_work_items(counts, counts2, tm, n_tiles):
    n_e = counts.shape[0]
    n_work = n_tiles + n_e - 1
    uend = jnp.cumsum(counts)
    ustart = uend - counts
    first_tile = ustart // tm
    n_w = jnp.where(counts > 0, (uend - 1) // tm - first_tile + 1, 0)
    wend = jnp.cumsum(n_w)
    wstart = wend - n_w
    total = wend[-1]
    idx = jnp.arange(n_work, dtype=jnp.int32)
    w = jnp.minimum(idx, total - 1)
    we = jnp.sum((w[:, None] >= wend[None, :]).astype(jnp.int32), axis=1)
    onehot = we[:, None] == jnp.arange(n_e, dtype=jnp.int32)[None, :]
    of_expert = lambda table: jnp.sum(jnp.where(onehot, table[None, :], 0), axis=1)
    wt = of_expert(first_tile) + (w - of_expert(wstart))
    lo = jnp.clip(of_expert(ustart) - wt * tm, 0, tm)
    hi = jnp.clip(of_expert(uend) - wt * tm, 0, tm)
    live = idx < total
    hi = jnp.where(live, hi, lo)
    prev_t = jnp.concatenate([jnp.full((1,), -1, jnp.int32), wt[:-1]])
    next_t = jnp.concatenate([wt[1:], jnp.full((1,), -1, jnp.int32)])
    final = idx == total - 1
    flags = (jnp.where(live & (wt != prev_t), FLAG_FIRST, 0)
             + jnp.where(live & ((wt != next_t) | final), FLAG_LAST, 0)
             + jnp.where(final, FLAG_FINAL, 0))
    fix_lo = jnp.clip(of_expert(uend - counts2) - wt * tm, lo, hi)
    as_i32 = lambda a: a.astype(jnp.int32)
    return ustart, (as_i32(we), as_i32(wt), as_i32(lo), as_i32(hi), as_i32(flags), as_i32(fix_lo), as_i32(hi))


def _combine_kernel(mrow, final, y0_ref, y1_ref, mf_ref, x3_ref, gate_ref, gout_ref, o_ref):
    mf = mf_ref[...]
    moe = mf[:, 0:1] * y0_ref[...] + mf[:, 1:2] * y1_ref[...]
    x4 = x3_ref[...] + _mod(gate_ref, mrow) * moe
    o_ref[...] = _rmsnorm(x4, gout_ref[...]) if final else x4


def _combine(grp, yk, row0, mf, x3, mod3, layer, gout, final):
    m, d = x3.shape
    nt = grp.grid[1]
    assert row0 % grp.tile == 0
    t0 = row0 // grp.tile
    choice = lambda k: pl.BlockSpec((None, grp.tile, d), lambda b, j: (k, t0 + b * nt + j, 0))
    return pl.pallas_call(
        functools.partial(_combine_kernel, grp.mod_row, final),
        out_shape=jax.ShapeDtypeStruct((m, d), F32),
        grid=grp.grid,
        in_specs=[choice(0), choice(1), grp.row_spec(LANES), grp.row_spec(d), grp.mod_spec(layer, 5, d),
                  _small((1, d))],
        out_specs=grp.row_spec(d),
        compiler_params=_params(("arbitrary", "arbitrary")),
        name="combine",
    )(yk, yk, mf, x3, mod3, gout)


def _moe_layer(groups, x3s, h3s, mis, mfs, cnt1, cnt, mod3, layer, wg, wu, wd, gout, final):
    m1, d = h3s[0].shape
    m = m1 + h3s[1].shape[0]
    n_e = wg.shape[0]
    tm = min(MOE_TILE, m1)
    n_tiles = pl.cdiv(2 * m, tm)
    counts = cnt[0, :n_e].astype(jnp.int32)
    counts2 = counts - cnt1[0, :n_e].astype(jnp.int32)
    ustart, work = _moe_work_items(counts, counts2, tm, n_tiles)

    def positions(mi):
        e, pos = mi[0:2], mi[2:4]
        for k in range(n_e):
            pos = pos + jnp.where(e == k, ustart[k], 0)
        return pos

    pos = jnp.concatenate([positions(mi) for mi in mis], axis=1)
    codes = _invert(pos.reshape(2 * m), n_tiles * tm).reshape(n_tiles, 1, tm)

    yk = _moe(h3s[0], h3s[1], codes, work, wg, wu, wd, tm)
    return [_combine(g, yk, r0, mf, x3, mod3, layer, gout, final)
            for g, r0, mf, x3 in zip(groups, (0, m1), mfs, x3s)]


def _layers(g_p, g_s, x_p, x_s, mod3, st_a, st_b, st_c, p):
    depth = p['w_ada'].shape[0]
    d = x_p.shape[1]
    new = {k: [] for k in ('pa', 'sa', 'pb', 'sb', 'pc', 'sc')}
    for i in range(depth):
        j = i // 2
        gmix = p['norm_mix'][i].reshape(1, d)
        gffn = p['norm_ffn'][i].reshape(1, d)
        if i % 2 == 0:
            conv_w = (p['w_conv_a'][j], p['w_conv_b'][j], p['b_conv_b'][j].reshape(1, -1),
                      p['ln_b_g'][j].reshape(1, -1), p['ln_b_b'][j].reshape(1, -1), p['w_out'][j])
            ffn_w = (p['w_ffn_gate'][j], p['w_ffn_up'][j], p['w_ffn_down'][j])
            u_p = _in_proj(g_p, x_p, mod3, i, gmix, p['w_in'][j])
            x_p, na, nb = _conv_seq(g_p, u_p, x_p, mod3, i, *conv_w)
            new['pa'].append(na)
            new['pb'].append(nb)
            x_p = _ffn(g_p, x_p, mod3, i, gffn, *ffn_w)
            u_s = _in_proj(g_s, x_s, mod3, i, gmix, p['w_in'][j])
            x_s, na, nb = _conv_step(g_s, u_s, st_a[j], st_b[j], x_s, mod3, i, *conv_w)
            new['sa'].append(na)
            new['sb'].append(nb)
            x_s = _ffn(g_s, x_s, mod3, i, gffn, *ffn_w)
        else:
            ctx = p['pool_ctx']
            pool_w = (p['w_pool'][j], p['pool_scale'][j].reshape(1, d), p['wr_hi'][j], p['wr_lo'][j], p['br'][j])
            x3_p, h3_p, mi_p, mf_p, nc, cnt1 = _pool_seq(g_p, x_p, mod3, i, gmix, gffn, *pool_w, ctx)
            new['pc'].append(nc)
            x3_s, h3_s, mi_s, mf_s, nc, cnt = _pool_step(g_s, x_s, st_c[j], cnt1, mod3, i, gmix, gffn, *pool_w, ctx)
            new['sc'].append(nc)
            x_p, x_s = _moe_layer((g_p, g_s), (x3_p, x3_s), (h3_p, h3_s), (mi_p, mi_s), (mf_p, mf_s), cnt1, cnt,
                                  mod3, i, p['w_exp_gate'][j], p['w_exp_up'][j], p['w_exp_down'][j],
                                  p['norm_out'].reshape(1, d), i == depth - 1)
    return x_p, x_s, new


def kernel(x_prompt, x_sample, state_a, state_b, state_c, c_prompt, c_sample, w_ada, b_ada, norm_mix, norm_ffn, norm_out, w_in, w_conv_a, w_conv_b, b_conv_b, ln_b_g, ln_b_b, w_out, w_ffn_gate, w_ffn_up, w_ffn_down, w_pool, pool_scale, w_router, b_router, w_exp_gate, w_exp_up, w_exp_down):
    n_p, seq, d = x_prompt.shape
    n_s, dec_seq, _ = x_sample.shape
    depth = w_ada.shape[0]
    n_e = w_router.shape[2]
    assert dec_seq == 1 and depth % 2 == 0 and n_s % SUBLANES == 0 and n_e <= LANES
    n_even, n_odd = state_a.shape[0], state_c.shape[0]
    ctx = state_c.shape[2]

    wr = jnp.pad(w_router, ((0, 0), (0, 0), (0, LANES - n_e)))
    wr_hi = wr.astype(BF16)
    wr_lo = (wr - wr_hi.astype(F32)).astype(BF16)
    br = jnp.pad(b_router, ((0, 0), (0, LANES - n_e)), constant_values=NEG_BIG).reshape(n_odd, 1, LANES)

    p = {
        'w_ada': w_ada, 'norm_mix': norm_mix, 'norm_ffn': norm_ffn, 'norm_out': norm_out,
        'w_in': w_in.astype(BF16), 'w_conv_a': w_conv_a, 'w_conv_b': w_conv_b, 'b_conv_b': b_conv_b,
        'ln_b_g': ln_b_g, 'ln_b_b': ln_b_b, 'w_out': w_out.astype(BF16),
        'w_ffn_gate': w_ffn_gate.astype(BF16), 'w_ffn_up': w_ffn_up.astype(BF16),
        'w_ffn_down': w_ffn_down.astype(BF16),
        'w_pool': w_pool.astype(BF16), 'pool_scale': pool_scale, 'wr_hi': wr_hi, 'wr_lo': wr_lo, 'br': br,
        'w_exp_gate': w_exp_gate, 'w_exp_up': w_exp_up, 'w_exp_down': w_exp_down, 'pool_ctx': ctx,
    }

    mod3 = _ada(jnp.concatenate([c_sample, c_prompt], axis=0), w_ada, b_ada)

    g_prompt = _Group(n_p, seq, False, n_s, n_s)
    g_sample = _Group(n_s, 1, True, 0, n_s)

    y_p, y_s, new = _layers(
        g_prompt, g_sample, x_prompt.reshape(n_p * seq, d), x_sample.reshape(n_s, d), mod3,
        state_a, state_b, state_c, p)

    return (y_p.reshape(n_p, seq, d), y_s.reshape(n_s, 1, d),
            jnp.stack(new['pa']), jnp.stack(new['sa']), jnp.stack(new['pb']), jnp.stack(new['sb']),
            jnp.stack(new['pc']), jnp.stack(new['sc']))
```

```python
import functools

import jax
import jax.numpy as jnp
from jax import lax
from jax.experimental import pallas as pl
from jax.experimental.pallas import tpu as pltpu

EPS = 1e-6
PAST_LEN = 16384
POOL_WINDOWS = (2, 4, 8, 16)
N_MOD = 6
LANES = 128
SUBLANES = 8
VMEM_LIMIT = 56 * 1024 * 1024
NEG_BIG = -1e30

ROW_TILE = 512
MOE_TILE = 1024
MOE_SUB = 256
MOE_FF = 512

F32 = jnp.float32
BF16 = jnp.bfloat16


def _params(sem):
    return pltpu.CompilerParams(dimension_semantics=sem, vmem_limit_bytes=VMEM_LIMIT)


def _silu(x):
    return x * jax.nn.sigmoid(x)


def _rmsnorm(x, g):
    return x * lax.rsqrt(jnp.mean(x * x, axis=-1, keepdims=True) + EPS) * g


def _mod_rmsnorm(x, g, sc, sh):
    return _rmsnorm(x, g) * (1.0 + sc) + sh


def _resident(shape):
    nd = len(shape)
    return pl.BlockSpec(shape, lambda *_: (0,) * nd, pipeline_mode=pl.Buffered(1))


def _mod(ref, mod_row):
    if mod_row is None:
        return ref[...]
    return ref[pl.ds(lax.rem(mod_row + pl.program_id(0), SUBLANES), 1), :]


def _small(shape):
    nd = len(shape)
    return pl.BlockSpec(shape, lambda *_: (0,) * nd)


class _Group:
    def __init__(self, n_seq, seq_len, per_row_mod, mod_row0, n_sample):
        self.n_seq, self.seq_len = n_seq, seq_len
        self.per_row_mod = per_row_mod
        self.mod_row0 = mod_row0
        self.n_sample = n_sample
        if per_row_mod:
            self.tile = n_seq
            self.grid = (1, 1)
        else:
            self.tile = min(ROW_TILE, seq_len)
            assert seq_len % self.tile == 0 and self.tile % 32 == 0
            self.grid = (n_seq, seq_len // self.tile)
        self.rows = n_seq * seq_len

    def row_spec(self, width):
        nt = self.grid[1]
        return pl.BlockSpec((self.tile, width), lambda b, j, *_: (b * nt + j, 0))

    def mod_spec(self, layer, chunk, d):
        if self.per_row_mod:
            return pl.BlockSpec((None, self.n_sample, d), lambda b, j, *_: (layer, 0, chunk))
        r0 = self.mod_row0
        return pl.BlockSpec((None, SUBLANES, d), lambda b, j, *_: (layer, (r0 + b) // SUBLANES, chunk))

    @property
    def mod_row(self):
        return None if self.per_row_mod else self.mod_row0


def _ada_kernel(c_ref, w_ref, b_ref, o_ref):
    cs = _silu(c_ref[...]).astype(BF16)
    o_ref[...] = jnp.dot(cs, w_ref[...].astype(BF16), preferred_element_type=F32) + b_ref[...]


def _ada(c_all, w_ada, b_ada):
    depth, d, w6 = w_ada.shape
    r = c_all.shape[0]
    tn = w6 // 4
    return pl.pallas_call(
        _ada_kernel,
        out_shape=jax.ShapeDtypeStruct((depth, r, w6), F32),
        grid=(depth, w6 // tn),
        in_specs=[pl.BlockSpec((r, d), lambda l, n: (0, 0)),
                  pl.BlockSpec((None, d, tn), lambda l, n: (l, 0, n)),
                  pl.BlockSpec((None, 1, tn), lambda l, n: (l, 0, n))],
        out_specs=pl.BlockSpec((None, r, tn), lambda l, n: (l, 0, n)),
        compiler_params=_params(("arbitrary", "arbitrary")),
        name="ada",
    )(c_all, w_ada, b_ada.reshape(depth, 1, w6))


def _in_proj_kernel(mrow, x_ref, g_ref, sc_ref, sh_ref, w_ref, u_ref):
    h = _mod_rmsnorm(x_ref[...], g_ref[...], _mod(sc_ref, mrow), _mod(sh_ref, mrow))
    u_ref[...] = jnp.dot(h.astype(BF16), w_ref[...], preferred_element_type=F32)


def _in_proj(grp, x, mod3, layer, g, w_bf):
    d, n = w_bf.shape
    return pl.pallas_call(
        functools.partial(_in_proj_kernel, grp.mod_row),
        out_shape=jax.ShapeDtypeStruct((grp.rows, n), F32),
        grid=grp.grid,
        in_specs=[grp.row_spec(d), _small((1, d)),
                  grp.mod_spec(layer, 1, d), grp.mod_spec(layer, 0, d),
                  _resident((d, n))],
        out_specs=grp.row_spec(n),
        compiler_params=_params(("arbitrary", "arbitrary")),
        name="in_proj",
    )(x, g, mod3, mod3, w_bf)


def _layernorm(y, g, b):
    mu = jnp.mean(y, axis=-1, keepdims=True)
    yc = y - mu
    var = jnp.mean(yc * yc, axis=-1, keepdims=True)
    return yc * lax.rsqrt(var + EPS) * g + b


def _causal_taps(pad_ref, w_ref, k_taps, halo, tt):
    off0 = halo - (k_taps - 1)
    y = None
    for c in range(SUBLANES):
        n = tt + (SUBLANES if c else 0)
        q = None
        for k in range(k_taps):
            if (off0 + k) % SUBLANES == c:
                a = off0 + k - c
                term = w_ref[k:k + 1, :] * pad_ref[a:a + n, :]
                q = term if q is None else q + term
        if q is not None:
            part = q[c:c + tt, :]
            y = part if y is None else y + part
    return y


def _conv_seq_kernel(mrow, n_t, ka, kb, u_ref, uh_ref, x_ref, gm_ref, wa_ref, wb_ref, bb_ref,
                     lng_ref, lnb_ref, wout_ref, x1_ref, na_ref, nb_ref, apad, gpad):
    j = pl.program_id(1)
    tt = u_ref.shape[0]
    da = wa_ref.shape[1]
    hb = uh_ref.shape[0]
    keep = (j > 0).astype(F32)

    u = u_ref[...]
    a_b, a_c, a_x = u[:, 0:da], u[:, da:2 * da], u[:, 2 * da:3 * da]
    b_v, b_g = u[:, 3 * da:4 * da], u[:, 4 * da:5 * da]
    uh = uh_ref[...]
    apad[0:SUBLANES, :] = keep * (uh[hb - SUBLANES:hb, da:2 * da] * uh[hb - SUBLANES:hb, 2 * da:3 * da])
    apad[SUBLANES:SUBLANES + tt, :] = a_c * a_x
    gpad[0:hb, :] = keep * (uh[:, 3 * da:4 * da] * jax.nn.sigmoid(uh[:, 4 * da:5 * da]))
    gpad[hb:hb + tt, :] = b_v * jax.nn.sigmoid(b_g)

    zero_tail = jnp.zeros((SUBLANES, da), F32)
    apad[SUBLANES + tt:SUBLANES + tt + SUBLANES, :] = zero_tail
    gpad[hb + tt:hb + tt + SUBLANES, :] = zero_tail

    ya = a_b * _causal_taps(apad, wa_ref, ka, SUBLANES, tt)
    yb = _causal_taps(gpad, wb_ref, kb, hb, tt)
    yb = _silu(_layernorm(yb + bb_ref[...], lng_ref[...], lnb_ref[...]))

    ycat = jnp.concatenate([ya, yb], axis=-1).astype(BF16)
    y = jnp.dot(ycat, wout_ref[...], preferred_element_type=F32)
    x1_ref[...] = x_ref[...] + _mod(gm_ref, mrow) * y

    @pl.when(j == n_t - 1)
    def _():
        na_ref[...] = apad[SUBLANES + tt - (ka - 1):SUBLANES + tt, :]
        nb_ref[...] = gpad[hb + tt - (kb - 1):hb + tt, :]


def _conv_seq(grp, u, x, mod3, layer, wa, wb, bb, lng, lnb, wout_bf):
    tt = grp.tile
    n_b, n_t = grp.grid
    d = x.shape[1]
    ka, da = wa.shape
    kb = wb.shape[0]
    hb = 32
    assert kb - 1 <= hb and ka - 1 <= SUBLANES and tt % hb == 0
    per = tt // hb
    halo = pl.BlockSpec((hb, u.shape[1]), lambda b, j: (jnp.maximum((b * n_t + j) * per - 1, 0), 0))
    return pl.pallas_call(
        functools.partial(_conv_seq_kernel, grp.mod_row, n_t, ka, kb),
        out_shape=(jax.ShapeDtypeStruct((grp.rows, d), F32),
                   jax.ShapeDtypeStruct((n_b, ka - 1, da), F32),
                   jax.ShapeDtypeStruct((n_b, kb - 1, da), F32)),
        grid=grp.grid,
        in_specs=[grp.row_spec(u.shape[1]), halo, grp.row_spec(d), grp.mod_spec(layer, 2, d),
                  _small(wa.shape), _small(wb.shape), _small((1, da)), _small((1, da)), _small((1, da)),
                  _resident(wout_bf.shape)],
        out_specs=(grp.row_spec(d),
                   pl.BlockSpec((None, ka - 1, da), lambda b, j: (b, 0, 0)),
                   pl.BlockSpec((None, kb - 1, da), lambda b, j: (b, 0, 0))),
        scratch_shapes=[pltpu.VMEM((2 * SUBLANES + tt, da), F32), pltpu.VMEM((hb + tt + SUBLANES, da), F32)],
        compiler_params=_params(("arbitrary", "arbitrary")),
        name="conv_seq",
    )(u, u, x, mod3, wa, wb, bb, lng, lnb, wout_bf)


def _conv_step_kernel(ka, kb, u_ref, sa_ref, sb_ref, x_ref, gm_ref, wa_ref, wb_ref, bb_ref,
                      lng_ref, lnb_ref, wout_ref, x1_ref, na_ref, nb_ref):
    da = wa_ref.shape[1]
    u = u_ref[...]
    a_b, a_c, a_x = u[:, 0:da], u[:, da:2 * da], u[:, 2 * da:3 * da]
    b_v, b_g = u[:, 3 * da:4 * da], u[:, 4 * da:5 * da]

    cur = a_c * a_x
    ya = wa_ref[ka - 1:ka, :] * cur
    for k in range(ka - 1):
        ya = ya + wa_ref[k:k + 1, :] * sa_ref[:, k, :]
    ya = a_b * ya
    glu = b_v * jax.nn.sigmoid(b_g)
    yb = wb_ref[kb - 1:kb, :] * glu
    for k in range(kb - 1):
        yb = yb + wb_ref[k:k + 1, :] * sb_ref[:, k, :]
    yb = _silu(_layernorm(yb + bb_ref[...], lng_ref[...], lnb_ref[...]))

    ycat = jnp.concatenate([ya, yb], axis=-1).astype(BF16)
    y = jnp.dot(ycat, wout_ref[...], preferred_element_type=F32)
    x1_ref[...] = x_ref[...] + gm_ref[...] * y

    if ka > 2:
        na_ref[:, 0:ka - 2, :] = sa_ref[:, 1:ka - 1, :]
    na_ref[:, ka - 2, :] = cur
    if kb > 2:
        nb_ref[:, 0:kb - 2, :] = sb_ref[:, 1:kb - 1, :]
    nb_ref[:, kb - 2, :] = glu


def _conv_step(grp, u, sa2, sb2, x, mod3, layer, wa, wb, bb, lng, lnb, wout_bf):
    n = grp.rows
    d = x.shape[1]
    ka, da = wa.shape
    kb = wb.shape[0]
    return pl.pallas_call(
        functools.partial(_conv_step_kernel, ka, kb),
        out_shape=(jax.ShapeDtypeStruct((n, d), F32),
                   jax.ShapeDtypeStruct(sa2.shape, F32),
                   jax.ShapeDtypeStruct(sb2.shape, F32)),
        grid=grp.grid,
        in_specs=[_small(u.shape), _small(sa2.shape), _small(sb2.shape), _small(x.shape),
                  grp.mod_spec(layer, 2, d),
                  _small(wa.shape), _small(wb.shape), _small((1, da)), _small((1, da)), _small((1, da)),
                  _small(wout_bf.shape)],
        out_specs=(_small((n, d)), _small(sa2.shape), _small(sb2.shape)),
        compiler_params=_params(("arbitrary", "arbitrary")),
        name="conv_step",
    )(u, sa2, sb2, x, mod3, wa, wb, bb, lng, lnb, wout_bf)


def _ffn_kernel(mrow, x_ref, g_ref, sc_ref, sh_ref, gate_ref, wg_ref, wu_ref, wd_ref, o_ref):
    x = x_ref[...]
    h = _mod_rmsnorm(x, g_ref[...], _mod(sc_ref, mrow), _mod(sh_ref, mrow)).astype(BF16)
    a = jnp.dot(h, wg_ref[...], preferred_element_type=F32)
    b = jnp.dot(h, wu_ref[...], preferred_element_type=F32)
    act = (_silu(a) * b).astype(BF16)
    f = jnp.dot(act, wd_ref[...], preferred_element_type=F32)
    o_ref[...] = x + _mod(gate_ref, mrow) * f


def _ffn(grp, x, mod3, layer, g, wg_bf, wu_bf, wd_bf):
    d = x.shape[1]
    return pl.pallas_call(
        functools.partial(_ffn_kernel, grp.mod_row),
        out_shape=jax.ShapeDtypeStruct(x.shape, F32),
        grid=grp.grid,
        in_specs=[grp.row_spec(d), _small((1, d)),
                  grp.mod_spec(layer, 4, d), grp.mod_spec(layer, 3, d), grp.mod_spec(layer, 5, d),
                  _resident(wg_bf.shape), _resident(wu_bf.shape), _resident(wd_bf.shape)],
        out_specs=grp.row_spec(d),
        compiler_params=_params(("arbitrary", "arbitrary")),
        name="ffn",
    )(x, g, mod3, mod3, mod3, wg_bf, wu_bf, wd_bf)


def _window_sum(v, win):
    assert win & (win - 1) == 0
    span = 1
    while span < win:
        v = v + pltpu.roll(v, span, 0)
        span *= 2
    return v


def _pool_project(diff, wp_ref):
    n_g, pg, _ = wp_ref.shape
    outs = [jnp.dot(diff[:, gi * pg:(gi + 1) * pg].astype(BF16), wp_ref[gi], preferred_element_type=F32)
            for gi in range(n_g)]
    return jnp.concatenate(outs, axis=-1)


def _route(h3, wrh_ref, wrl_ref, br_ref, tri_ref, base_ref, mi_ref, mf_ref, cnt_ref):
    tt = h3.shape[0]
    h_hi = h3.astype(BF16)
    h_lo = (h3 - h_hi.astype(F32)).astype(BF16)
    logits = (jnp.dot(h_hi, wrh_ref[...], preferred_element_type=F32)
              + jnp.dot(h_lo, wrh_ref[...], preferred_element_type=F32)
              + jnp.dot(h_hi, wrl_ref[...], preferred_element_type=F32)
              + br_ref[...])
    lane = lax.broadcasted_iota(jnp.int32, (tt, LANES), 1).astype(F32)
    m0 = jnp.max(logits, axis=-1, keepdims=True)
    e0 = jnp.min(jnp.where(logits == m0, lane, float(LANES)), axis=-1, keepdims=True)
    rest = jnp.where(lane == e0, NEG_BIG * 2.0, logits)
    m1 = jnp.max(rest, axis=-1, keepdims=True)
    e1 = jnp.min(jnp.where(rest == m1, lane, float(LANES)), axis=-1, keepdims=True)
    dd = jnp.exp(m1 - m0)
    p0 = 1.0 / (1.0 + dd)
    p1 = dd * p0

    oh0 = lane == e0
    oh1 = lane == e1
    c = jnp.where(oh0 | oh1, 1.0, 0.0)
    prefix = jnp.dot(tri_ref[...], c.astype(BF16), preferred_element_type=F32)
    tot = base_ref[...] + prefix
    r0 = jnp.sum(jnp.where(oh0, tot, 0.0), axis=-1, keepdims=True)
    r1 = jnp.sum(jnp.where(oh1, tot, 0.0), axis=-1, keepdims=True)
    new_base = base_ref[...] + jnp.sum(c, axis=0, keepdims=True)
    base_ref[...] = new_base
    cnt_ref[...] = jnp.broadcast_to(new_base, cnt_ref.shape)

    meta = jnp.where(lane == 0.0, e0, jnp.where(lane == 1.0, e1,
                     jnp.where(lane == 2.0, r0, jnp.where(lane == 3.0, r1, 0.0))))
    mi_ref[...] = meta.T[0:SUBLANES, :].astype(jnp.int32)
    mf_ref[...] = jnp.where(lane == 0.0, p0, jnp.where(lane == 1.0, p1, 0.0))


def _init_route_scratch(first, tri_ref, base_ref, base0=None):
    @pl.when(first)
    def _():
        tt = tri_ref.shape[0]
        r = lax.broadcasted_iota(jnp.int32, (tt, tt), 0)
        c = lax.broadcasted_iota(jnp.int32, (tt, tt), 1)
        tri_ref[...] = jnp.where(c < r, 1.0, 0.0).astype(BF16)
        base_ref[...] = jnp.zeros(base_ref.shape, F32) if base0 is None else base0


def _pool_seq_kernel(mrow, n_t, ctx, x_ref, xh_ref, gmix_ref, shm_ref, scm_ref, gm_ref, gffn_ref, shf_ref, scf_ref,
                     wp_ref, ps_ref, wrh_ref, wrl_ref, br_ref,
                     x3_ref, h3_ref, mi_ref, mf_ref, nc_ref, cnt_ref, hpad, tri_ref, base_ref):
    b = pl.program_id(0)
    j = pl.program_id(1)
    tt = x_ref.shape[0]
    hb = xh_ref.shape[0]
    n_g, pg, _ = wp_ref.shape
    _init_route_scratch((b == 0) & (j == 0), tri_ref, base_ref)

    keep = (j > 0).astype(F32)
    x = x_ref[...]
    scm, shm = _mod(scm_ref, mrow), _mod(shm_ref, mrow)
    h = _mod_rmsnorm(x, gmix_ref[...], scm, shm)
    hpad[0:hb, :] = keep * _mod_rmsnorm(xh_ref[...], gmix_ref[...], scm, shm)
    hpad[hb:hb + tt, :] = h

    pos = lax.broadcasted_iota(jnp.int32, (tt, pg), 0) + j * tt
    groups = []
    for gi, win in enumerate(POOL_WINDOWS):
        sl = slice(gi * pg, (gi + 1) * pg)
        s = _window_sum(hpad[:, sl], win)[hb:hb + tt, :]
        cnt = jnp.minimum(pos + 1, win).astype(F32)
        groups.append(s / cnt - h[:, sl])
    diff = jnp.concatenate(groups, axis=-1)
    y = _pool_project(diff, wp_ref) * ps_ref[...]
    x3 = x + _mod(gm_ref, mrow) * y
    x3_ref[...] = x3
    h3 = _mod_rmsnorm(x3, gffn_ref[...], _mod(scf_ref, mrow), _mod(shf_ref, mrow))
    h3_ref[...] = h3
    _route(h3, wrh_ref, wrl_ref, br_ref, tri_ref, base_ref, mi_ref, mf_ref, cnt_ref)

    @pl.when(j == n_t - 1)
    def _():
        nc_ref[...] = hpad[hb + tt - ctx:hb + tt, :]


def _route_out_shapes(rows, d):
    return (jax.ShapeDtypeStruct((rows, d), F32), jax.ShapeDtypeStruct((rows, d), F32),
            jax.ShapeDtypeStruct((SUBLANES, rows), jnp.int32), jax.ShapeDtypeStruct((rows, LANES), F32))


def _pool_seq(grp, x, mod3, layer, gmix, gffn, wp_bf, ps, wrh, wrl, br, ctx):
    tt = grp.tile
    n_b, n_t = grp.grid
    d = x.shape[1]
    hb = 16
    assert ctx <= hb and max(POOL_WINDOWS) - 1 <= hb and tt % hb == 0
    per = tt // hb
    halo = pl.BlockSpec((hb, d), lambda b, j: (jnp.maximum((b * n_t + j) * per - 1, 0), 0))
    ms = lambda c: grp.mod_spec(layer, c, d)
    return pl.pallas_call(
        functools.partial(_pool_seq_kernel, grp.mod_row, n_t, ctx),
        out_shape=_route_out_shapes(grp.rows, d) + (
            jax.ShapeDtypeStruct((n_b, ctx, d), F32), jax.ShapeDtypeStruct((SUBLANES, LANES), F32)),
        grid=grp.grid,
        in_specs=[grp.row_spec(d), halo, _small((1, d)), ms(0), ms(1), ms(2), _small((1, d)), ms(3), ms(4),
                  _small(wp_bf.shape), _small((1, d)), _small(wrh.shape), _small(wrl.shape), _small(br.shape)],
        out_specs=(grp.row_spec(d), grp.row_spec(d), pl.BlockSpec((SUBLANES, tt), lambda b, j: (0, b * n_t + j)),
                   grp.row_spec(LANES), pl.BlockSpec((None, ctx, d), lambda b, j: (b, 0, 0)), _small((SUBLANES, LANES))),
        scratch_shapes=[pltpu.VMEM((hb + tt, d), F32), pltpu.VMEM((tt, tt), BF16), pltpu.VMEM((1, LANES), F32)],
        compiler_params=_params(("arbitrary", "arbitrary")),
        name="pool_seq",
    )(x, x, gmix, *([mod3] * 3), gffn, *([mod3] * 2), wp_bf, ps, wrh, wrl, br)


def _pool_step_kernel(ctx, x_ref, sc_ref_state, cnt0_ref, gmix_ref, shm_ref, scm_ref, gm_ref, gffn_ref, shf_ref,
                      scf_ref, wp_ref, ps_ref, wrh_ref, wrl_ref, br_ref,
                      x3_ref, h3_ref, mi_ref, mf_ref, nc_ref, cnt_ref, tri_ref, base_ref):
    d = x_ref.shape[1]
    n_g, pg, _ = wp_ref.shape
    _init_route_scratch(pl.program_id(0) == 0, tri_ref, base_ref, cnt0_ref[0:1, :])

    x = x_ref[...]
    h = _mod_rmsnorm(x, gmix_ref[...], scm_ref[...], shm_ref[...])
    groups = []
    for gi, win in enumerate(POOL_WINDOWS):
        s = h[:, gi * pg:(gi + 1) * pg]
        for i in range(1, win):
            row = ctx - i
            s = s + sc_ref_state[:, row, gi * pg:(gi + 1) * pg]
        cnt = float(min(PAST_LEN + 1, win))
        groups.append(s / cnt - h[:, gi * pg:(gi + 1) * pg])
    diff = jnp.concatenate(groups, axis=-1)
    y = _pool_project(diff, wp_ref) * ps_ref[...]
    x3 = x + gm_ref[...] * y
    x3_ref[...] = x3
    h3 = _mod_rmsnorm(x3, gffn_ref[...], scf_ref[...], shf_ref[...])
    h3_ref[...] = h3
    _route(h3, wrh_ref, wrl_ref, br_ref, tri_ref, base_ref, mi_ref, mf_ref, cnt_ref)

    if ctx > 1:
        nc_ref[:, 0:ctx - 1, :] = sc_ref_state[:, 1:ctx, :]
    nc_ref[:, ctx - 1, :] = h


def _pool_step(grp, x, sc2, cnt0, mod3, layer, gmix, gffn, wp_bf, ps, wrh, wrl, br, ctx):
    n, d = x.shape
    ms = lambda c: grp.mod_spec(layer, c, d)
    return pl.pallas_call(
        functools.partial(_pool_step_kernel, ctx),
        out_shape=_route_out_shapes(n, d) + (
            jax.ShapeDtypeStruct(sc2.shape, F32), jax.ShapeDtypeStruct((SUBLANES, LANES), F32)),
        grid=grp.grid,
        in_specs=[_small(x.shape), _small(sc2.shape), _small(cnt0.shape), _small((1, d)), ms(0), ms(1), ms(2),
                  _small((1, d)), ms(3), ms(4),
                  _small(wp_bf.shape), _small((1, d)), _small(wrh.shape), _small(wrl.shape), _small(br.shape)],
        out_specs=(_small((n, d)), _small((n, d)), _small((SUBLANES, n)), _small((n, LANES)),
                   _small(sc2.shape), _small((SUBLANES, LANES))),
        scratch_shapes=[pltpu.VMEM((n, n), BF16), pltpu.VMEM((1, LANES), F32)],
        compiler_params=_params(("arbitrary", "arbitrary")),
        name="pool_step",
    )(x, sc2, cnt0, gmix, *([mod3] * 3), gffn, *([mod3] * 2), wp_bf, ps, wrh, wrl, br)


CODE_BITS = 16
CODE_MASK = (1 << CODE_BITS) - 1


def _invert_kernel(m1, pos_ref, code_ref):
    n_real = pos_ref.shape[0]
    m = n_real // 2
    n_slots = code_ref.shape[0]
    rows = n_slots // 2

    def code(token, choice):
        return lax.shift_left(jnp.minimum(token, m1 - 1), CODE_BITS) | (choice * rows + token)

    def body(i, carry):
        code_ref[pos_ref[i]] = code(i, 0)
        code_ref[pos_ref[m + i]] = code(i, 1)
        return carry

    lax.fori_loop(0, m, body, 0, unroll=8)

    def pad(s, carry):
        code_ref[s] = code(lax.shift_right_logical(s, 1), s & 1)
        return carry

    lax.fori_loop(n_real, n_slots, pad, 0)


def _invert(pos_flat, n_slots, m1):
    assert m1 <= 1 << (31 - CODE_BITS) and n_slots <= 1 << CODE_BITS and n_slots % 2 == 0
    smem = pl.BlockSpec(memory_space=pltpu.SMEM)
    return pl.pallas_call(
        functools.partial(_invert_kernel, m1),
        out_shape=jax.ShapeDtypeStruct((n_slots,), jnp.int32),
        in_specs=[smem], out_specs=smem,
        name="invert",
    )(pos_flat)


FLAG_FIRST, FLAG_LAST, FLAG_FINAL = 1, 2, 4


def _moe_kernel(sub, n_fc, we_ref, wt_ref, wlo_ref, whi_ref, wflag_ref, wfix_lo_ref, wfix_hi_ref,
                code_prev_ref, code_ref, code_next_ref, h_hbm, h2_hbm, wg_ref, wu_ref, wd_ref, yk_hbm,
                xs_buf, acc, xb, wgb, wub, wdb, gsem, ssem, fsem):
    w = pl.program_id(0)
    j = pl.program_id(1)
    tm = xb.shape[0]
    t = wt_ref[w]
    slot = lax.rem(t, 2)
    other = 1 - slot
    lo = wlo_ref[w]
    hi = whi_ref[w]
    flag = wflag_ref[w]
    first = (flag & FLAG_FIRST) != 0
    last = (flag & FLAG_LAST) != 0
    final = (flag & FLAG_FINAL) != 0
    n_share = max(n_fc - 1, 1)
    share = (tm // n_share) // SUBLANES * SUBLANES
    rest = tm - share * n_share

    m1 = h_hbm.shape[0]

    def gather_row(codes, s, r):
        tok = lax.shift_right_logical(codes[0, r], CODE_BITS)
        return pltpu.make_async_copy(h_hbm.at[pl.ds(tok, 1), :], xs_buf.at[s, pl.ds(r, 1), :], gsem.at[s])

    rows_per_choice = yk_hbm.shape[0] // 2

    def second_group_row(r):
        row = code_ref[0, r] & CODE_MASK
        tok = jnp.where(row >= rows_per_choice, row - rows_per_choice, row) - m1
        return pltpu.make_async_copy(h2_hbm.at[pl.ds(tok, 1), :], xs_buf.at[slot, pl.ds(r, 1), :], fsem)

    def refetch_second_group(r0, r1):
        def start(r, carry):
            second_group_row(r).start()
            return carry

        def wait(r, carry):
            second_group_row(r).wait()
            return carry

        lax.fori_loop(r0, r1, start, 0)
        lax.fori_loop(r0, r1, wait, 0)

    def scatter_row(codes, s, r):
        code = codes[0, r]
        dst = yk_hbm.at[pl.ds(code & CODE_MASK, 1), :]
        return pltpu.make_async_copy(acc.at[s, pl.ds(r, 1), :], dst, ssem.at[s])

    def wait_gather(s):
        pltpu.make_async_copy(h_hbm.at[pl.ds(0, tm), :], xs_buf.at[s], gsem.at[s]).wait()

    def wait_scatter(s):
        pltpu.make_async_copy(acc.at[s], yk_hbm.at[pl.ds(0, tm), :], ssem.at[s]).wait()

    def rows_loop(r0, n, fn):
        def body(i, carry):
            fn(r0 + i)
            return carry

        lax.fori_loop(0, n, body, 0, unroll=SUBLANES)

    def neighbour_share():
        base = pl.multiple_of(j * share, SUBLANES)
        for u in range(share):
            gather_row(code_next_ref, other, base + u).start()
            scatter_row(code_prev_ref, other, base + u).start()

    @pl.when((w == 0) & (j == 0))
    def _():
        rows_loop(0, tm, lambda r: gather_row(code_ref, 0, r).start())
        acc[1] = jnp.zeros((tm, acc.shape[2]), F32)

    @pl.when(first & (j == 0))
    def _():
        wait_gather(slot)
        acc[slot] = jnp.zeros((tm, acc.shape[2]), F32)
        if rest:
            rows_loop(share * n_share, rest, lambda r: gather_row(code_next_ref, other, r).start())
            rows_loop(share * n_share, rest, lambda r: scatter_row(code_prev_ref, other, r).start())

    def swiglu(x, wg, wu, wd):
        a = jnp.dot(x, wg, preferred_element_type=F32)
        b = jnp.dot(x, wu, preferred_element_type=F32)
        return jnp.dot((_silu(a) * b).astype(BF16), wd, preferred_element_type=F32)

    n_sub = tm // sub
    touched = lax.div(hi + (sub - 1), sub) - lax.div(lo, sub)
    straight = first & (touched * 4 >= n_sub * 3)

    @pl.when((hi > lo) & (j == 0))
    def _():
        refetch_second_group(wfix_lo_ref[w], wfix_hi_ref[w])
        rows = lax.broadcasted_iota(jnp.int32, xb.shape, 0)
        xb[...] = jnp.where((rows >= lo) & (rows < hi), xs_buf[slot], 0.0).astype(BF16)

    def whole_tile():
        acc[slot] += swiglu(xb[...], wg_ref[...].astype(BF16), wu_ref[...].astype(BF16),
                            wd_ref[...].astype(BF16))

    @pl.when(straight & (j < n_share))
    def _():
        neighbour_share()
        whole_tile()

    @pl.when(straight & (j >= n_share))
    def _():
        whole_tile()

    @pl.when((hi > lo) & jnp.logical_not(straight))
    def _():
        @pl.when(first & (j < n_share))
        def _():
            neighbour_share()

        wgb[...] = wg_ref[...].astype(BF16)
        wub[...] = wu_ref[...].astype(BF16)
        wdb[...] = wd_ref[...].astype(BF16)
        for s in range(n_sub):
            @pl.when((s * sub < hi) & ((s + 1) * sub > lo))
            def _():
                rs = pl.ds(s * sub, sub)
                acc[slot, rs, :] += swiglu(xb[rs, :], wgb[...], wub[...], wdb[...])

    @pl.when(last & (j == n_fc - 1))
    def _():
        wait_scatter(other)

        @pl.when(final)
        def _():
            rows_loop(0, tm, lambda r: scatter_row(code_ref, slot, r).start())
            wait_scatter(slot)
            wait_gather(other)


def _moe(h3, h3b, codes, work, wg, wu, wd, tm):
    d = h3.shape[1]
    n_e, _, ff = wg.shape
    fc = min(MOE_FF, ff)
    assert ff % fc == 0
    n_fc = ff // fc
    sub = min(MOE_SUB, tm)
    n_work = work[0].shape[0]
    n_tiles = codes.shape[0]

    def jj(w, j, wlo, whi):
        return jnp.where(whi[w] > wlo[w], j, n_fc - 1)

    grid_spec = pltpu.PrefetchScalarGridSpec(
        num_scalar_prefetch=7,
        grid=(n_work, n_fc),
        in_specs=[pl.BlockSpec((None, 1, tm), lambda w, j, we, wt, wlo, whi, wf, fl, fh: (jnp.maximum(wt[w] - 1, 0), 0, 0),
                               memory_space=pltpu.SMEM),
                  pl.BlockSpec((None, 1, tm), lambda w, j, we, wt, wlo, whi, wf, fl, fh: (wt[w], 0, 0),
                               memory_space=pltpu.SMEM),
                  pl.BlockSpec((None, 1, tm),
                               lambda w, j, we, wt, wlo, whi, wf, fl, fh: (jnp.minimum(wt[w] + 1, n_tiles - 1), 0, 0),
                               memory_space=pltpu.SMEM),
                  pl.BlockSpec(memory_space=pl.ANY), pl.BlockSpec(memory_space=pl.ANY),
                  pl.BlockSpec((None, d, fc), lambda w, j, we, wt, wlo, whi, wf, fl, fh: (we[w], 0, jj(w, j, wlo, whi))),
                  pl.BlockSpec((None, d, fc), lambda w, j, we, wt, wlo, whi, wf, fl, fh: (we[w], 0, jj(w, j, wlo, whi))),
                  pl.BlockSpec((None, fc, d), lambda w, j, we, wt, wlo, whi, wf, fl, fh: (we[w], jj(w, j, wlo, whi), 0))],
        out_specs=pl.BlockSpec(memory_space=pl.ANY),
        scratch_shapes=[pltpu.VMEM((2, tm, d), F32), pltpu.VMEM((2, tm, d), F32), pltpu.VMEM((tm, d), BF16),
                        pltpu.VMEM((d, fc), BF16), pltpu.VMEM((d, fc), BF16), pltpu.VMEM((fc, d), BF16),
                        pltpu.SemaphoreType.DMA((2,)), pltpu.SemaphoreType.DMA((2,)), pltpu.SemaphoreType.DMA(())])
    return pl.pallas_call(
        functools.partial(_moe_kernel, sub, n_fc),
        out_shape=jax.ShapeDtypeStruct((n_tiles * tm, d), F32),
        grid_spec=grid_spec,
        compiler_params=_params(("arbitrary", "arbitrary")),
        name="moe",
    )(*work, codes, codes, codes, h3, h3b, wg, wu, wd)


def _moe_work_items(counts, counts2, tm, n_tiles):
    n_e = counts.shape[0]
    n_work = n_tiles + n_e - 1
    uend = jnp.cumsum(counts)
    ustart = uend - counts
    first_tile = ustart // tm
    n_w = jnp.where(counts > 0, (uend - 1) // tm - first_tile + 1, 0)
    wend = jnp.cumsum(n_w)
    wstart = wend - n_w
    total = wend[-1]
    idx = jnp.arange(n_work, dtype=jnp.int32)
    w = jnp.minimum(idx, total - 1)
    we = jnp.sum((w[:, None] >= wend[None, :]).astype(jnp.int32), axis=1)
    onehot = we[:, None] == jnp.arange(n_e, dtype=jnp.int32)[None, :]
    of_expert = lambda table: jnp.sum(jnp.where(onehot, table[None, :], 0), axis=1)
    wt = of_expert(first_tile) + (w - of_expert(wstart))
    lo = jnp.clip(of_expert(ustart) - wt * tm, 0, tm)
    hi = jnp.clip(of_expert(uend) - wt * tm, 0, tm)
    live = idx < total
    hi = jnp.where(live, hi, lo)
    prev_t = jnp.concatenate([jnp.full((1,), -1, jnp.int32), wt[:-1]])
    next_t = jnp.concatenate([wt[1:], jnp.full((1,), -1, jnp.int32)])
    final = idx == total - 1
    flags = (jnp.where(live & (wt != prev_t), FLAG_FIRST, 0)
             + jnp.where(live & ((wt != next_t) | final), FLAG_LAST, 0)
             + jnp.where(final, FLAG_FINAL, 0))
    fix_lo = jnp.clip(of_expert(uend - counts2) - wt * tm, lo, hi)
    as_i32 = lambda a: a.astype(jnp.int32)
    return ustart, (as_i32(we), as_i32(wt), as_i32(lo), as_i32(hi), as_i32(flags), as_i32(fix_lo), as_i32(hi))


def _combine_kernel(mrow, final, y0_ref, y1_ref, mf_ref, x3_ref, gate_ref, gout_ref, o_ref):
    mf = mf_ref[...]
    moe = mf[:, 0:1] * y0_ref[...] + mf[:, 1:2] * y1_ref[...]
    x4 = x3_ref[...] + _mod(gate_ref, mrow) * moe
    o_ref[...] = _rmsnorm(x4, gout_ref[...]) if final else x4


def _combine(grp, yk, row0, mf, x3, mod3, layer, gout, final):
    m, d = x3.shape
    nt = grp.grid[1]
    assert row0 % grp.tile == 0
    t0 = row0 // grp.tile
    choice = lambda k: pl.BlockSpec((None, grp.tile, d), lambda b, j: (k, t0 + b * nt + j, 0))
    return pl.pallas_call(
        functools.partial(_combine_kernel, grp.mod_row, final),
        out_shape=jax.ShapeDtypeStruct((m, d), F32),
        grid=grp.grid,
        in_specs=[choice(0), choice(1), grp.row_spec(LANES), grp.row_spec(d), grp.mod_spec(layer, 5, d),
                  _small((1, d))],
        out_specs=grp.row_spec(d),
        compiler_params=_params(("arbitrary", "arbitrary")),
        name="combine",
    )(yk, yk, mf, x3, mod3, gout)


def _moe_layer(groups, x3s, h3s, mis, mfs, cnt1, cnt, mod3, layer, wg, wu, wd, gout, final):
    m1, d = h3s[0].shape
    m = m1 + h3s[1].shape[0]
    n_e = wg.shape[0]
    tm = min(MOE_TILE, m1)
    n_tiles = pl.cdiv(2 * m, tm)
    counts = cnt[0, :n_e].astype(jnp.int32)
    counts2 = counts - cnt1[0, :n_e].astype(jnp.int32)
    ustart, work = _moe_work_items(counts, counts2, tm, n_tiles)

    def positions(mi):
        e, pos = mi[0:2], mi[2:4]
        for k in range(n_e):
            pos = pos + jnp.where(e == k, ustart[k], 0)
        return pos

    pos = jnp.concatenate([positions(mi) for mi in mis], axis=1)
    codes = _invert(pos.reshape(2 * m), n_tiles * tm, m1).reshape(n_tiles, 1, tm)

    yk = _moe(h3s[0], h3s[1], codes, work, wg, wu, wd, tm).reshape(2, n_tiles * tm // 2, d)
    return [_combine(g, yk, r0, mf, x3, mod3, layer, gout, final)
            for g, r0, mf, x3 in zip(groups, (0, m1), mfs, x3s)]


def _layers(g_p, g_s, x_p, x_s, mod3, st_a, st_b, st_c, p):
    depth = p['w_ada'].shape[0]
    d = x_p.shape[1]
    new = {k: [] for k in ('pa', 'sa', 'pb', 'sb', 'pc', 'sc')}
    for i in range(depth):
        j = i // 2
        gmix = p['norm_mix'][i].reshape(1, d)
        gffn = p['norm_ffn'][i].reshape(1, d)
        if i % 2 == 0:
            conv_w = (p['w_conv_a'][j], p['w_conv_b'][j], p['b_conv_b'][j].reshape(1, -1),
                      p['ln_b_g'][j].reshape(1, -1), p['ln_b_b'][j].reshape(1, -1), p['w_out'][j])
            ffn_w = (p['w_ffn_gate'][j], p['w_ffn_up'][j], p['w_ffn_down'][j])
            u_p = _in_proj(g_p, x_p, mod3, i, gmix, p['w_in'][j])
            x_p, na, nb = _conv_seq(g_p, u_p, x_p, mod3, i, *conv_w)
            new['pa'].append(na)
            new['pb'].append(nb)
            x_p = _ffn(g_p, x_p, mod3, i, gffn, *ffn_w)
            u_s = _in_proj(g_s, x_s, mod3, i, gmix, p['w_in'][j])
            x_s, na, nb = _conv_step(g_s, u_s, st_a[j], st_b[j], x_s, mod3, i, *conv_w)
            new['sa'].append(na)
            new['sb'].append(nb)
            x_s = _ffn(g_s, x_s, mod3, i, gffn, *ffn_w)
        else:
            ctx = p['pool_ctx']
            pool_w = (p['w_pool'][j], p['pool_scale'][j].reshape(1, d), p['wr_hi'][j], p['wr_lo'][j], p['br'][j])
            x3_p, h3_p, mi_p, mf_p, nc, cnt1 = _pool_seq(g_p, x_p, mod3, i, gmix, gffn, *pool_w, ctx)
            new['pc'].append(nc)
            x3_s, h3_s, mi_s, mf_s, nc, cnt = _pool_step(g_s, x_s, st_c[j], cnt1, mod3, i, gmix, gffn, *pool_w, ctx)
            new['sc'].append(nc)
            x_p, x_s = _moe_layer((g_p, g_s), (x3_p, x3_s), (h3_p, h3_s), (mi_p, mi_s), (mf_p, mf_s), cnt1, cnt,
                                  mod3, i, p['w_exp_gate'][j], p['w_exp_up'][j], p['w_exp_down'][j],
                                  p['norm_out'].reshape(1, d), i == depth - 1)
    return x_p, x_s, new


def kernel(x_prompt, x_sample, state_a, state_b, state_c, c_prompt, c_sample, w_ada, b_ada, norm_mix, norm_ffn, norm_out, w_in, w_conv_a, w_conv_b, b_conv_b, ln_b_g, ln_b_b, w_out, w_ffn_gate, w_ffn_up, w_ffn_down, w_pool, pool_scale, w_router, b_router, w_exp_gate, w_exp_up, w_exp_down):
    n_p, seq, d = x_prompt.shape
    n_s, dec_seq, _ = x_sample.shape
    depth = w_ada.shape[0]
    n_e = w_router.shape[2]
    assert dec_seq == 1 and depth % 2 == 0 and n_s % SUBLANES == 0 and n_e <= LANES
    n_even, n_odd = state_a.shape[0], state_c.shape[0]
    ctx = state_c.shape[2]

    wr = jnp.pad(w_router, ((0, 0), (0, 0), (0, LANES - n_e)))
    wr_hi = wr.astype(BF16)
    wr_lo = (wr - wr_hi.astype(F32)).astype(BF16)
    br = jnp.pad(b_router, ((0, 0), (0, LANES - n_e)), constant_values=NEG_BIG).reshape(n_odd, 1, LANES)

    p = {
        'w_ada': w_ada, 'norm_mix': norm_mix, 'norm_ffn': norm_ffn, 'norm_out': norm_out,
        'w_in': w_in.astype(BF16), 'w_conv_a': w_conv_a, 'w_conv_b': w_conv_b, 'b_conv_b': b_conv_b,
        'ln_b_g': ln_b_g, 'ln_b_b': ln_b_b, 'w_out': w_out.astype(BF16),
        'w_ffn_gate': w_ffn_gate.astype(BF16), 'w_ffn_up': w_ffn_up.astype(BF16),
        'w_ffn_down': w_ffn_down.astype(BF16),
        'w_pool': w_pool.astype(BF16), 'pool_scale': pool_scale, 'wr_hi': wr_hi, 'wr_lo': wr_lo, 'br': br,
        'w_exp_gate': w_exp_gate, 'w_exp_up': w_exp_up, 'w_exp_down': w_exp_down, 'pool_ctx': ctx,
    }

    mod3 = _ada(jnp.concatenate([c_sample, c_prompt], axis=0), w_ada, b_ada)

    g_prompt = _Group(n_p, seq, False, n_s, n_s)
    g_sample = _Group(n_s, 1, True, 0, n_s)

    y_p, y_s, new = _layers(
        g_prompt, g_sample, x_prompt.reshape(n_p * seq, d), x_sample.reshape(n_s, d), mod3,
        state_a, state_b, state_c, p)

    return (y_p.reshape(n_p, seq, d), y_s.reshape(n_s, 1, d),
            jnp.stack(new['pa']), jnp.stack(new['sa']), jnp.stack(new['pb']), jnp.stack(new['sb']),
            jnp.stack(new['pc']), jnp.stack(new['sc']))
```
